```python
import jax, jax.numpy as jnp
from jax import lax
import numpy as np

D_MODEL = 1024
BATCH = 16
SEQ = 4096
DEPTH = 4
DEC_BATCH = 32
DEC_SEQ = 16
PAST_LEN = 4096

CHUNK = 64
MIX_WIDTH = D_MODEL
RET_HEADS = 4
RET_HEAD_DIM = 64
RET_WIDTH = RET_HEADS * RET_HEAD_DIM
FOX_HEADS = 8
FOX_HEAD_DIM = 64
FOX_WIDTH = FOX_HEADS * FOX_HEAD_DIM
CONV_DIM = MIX_WIDTH - RET_WIDTH - FOX_WIDTH
CONV_W = 3
FFN_CONV_W = 3
D_FF = 256 * ((8 * D_MODEL // 3 + 255) // 256)
FOX_QBLOCK = 128
ROPE_BASE = 10000.0
NORM_EPS = 1e-6
PROJ_SIZES = (RET_WIDTH,) * 4 + (FOX_WIDTH,) * 3 + (FOX_HEADS,) + (CONV_DIM,) * 3
PROJ_DIM = sum(PROJ_SIZES)

kernel_name = "hybrid_streaming_retention_fox_conv"


def _rms_norm(x, g):
    xf = x.astype(jnp.float32)
    y = xf * lax.rsqrt(jnp.mean(xf * xf, axis=-1, keepdims=True) + NORM_EPS)
    return (y * g.astype(jnp.float32)).astype(x.dtype)


def _head_group_norm(o, g):
    mu = jnp.mean(o, axis=-1, keepdims=True)
    d = o - mu
    return d * lax.rsqrt(jnp.mean(d * d, axis=-1, keepdims=True) + NORM_EPS) * g.astype(jnp.float32)


def _rotary(x, pos):
    half = x.shape[-1] // 2
    inv_freq = ROPE_BASE ** (-jnp.arange(half, dtype=jnp.float32) / half)
    ang = pos.astype(jnp.float32)[:, None] * inv_freq[None, :]
    cos = jnp.cos(ang)[None, :, None, :]
    sin = jnp.sin(ang)[None, :, None, :]
    xf = x.astype(jnp.float32)
    x1, x2 = xf[..., :half], xf[..., half:]
    return jnp.concatenate([x1 * cos - x2 * sin, x1 * sin + x2 * cos], axis=-1).astype(x.dtype)


def _causal_dwconv(u, hist, w):
    width = w.shape[0]
    s = u.shape[1]
    full = jnp.concatenate([hist.astype(u.dtype), u], axis=1)
    y = full[:, 0:s] * w[0]
    for j in range(1, width):
        y = y + full[:, j:j + s] * w[j]
    return y.astype(u.dtype), full[:, s:]


def _retention(q, k, v, s0):
    L = q.shape[2]
    log_g = jnp.log1p(-jnp.exp2(-5.0 - jnp.arange(RET_HEADS, dtype=jnp.float32)))
    i = jnp.arange(L, dtype=jnp.float32)
    diff = i[:, None] - i[None, :]
    decay = jnp.where(diff[None] >= 0.0,
                      jnp.exp(jnp.maximum(diff, 0.0)[None] * log_g[:, None, None]), 0.0)
    scores = jnp.einsum('bcihd,bcjhd->bchij', q, k) * decay
    o_inner = jnp.einsum('bchij,bcjhe->bcihe', scores, v)
    k_w = k * jnp.exp((L - 1.0 - i)[:, None] * log_g[None, :])[:, :, None]
    ds = jnp.einsum('bcjhd,bcjhe->bchde', k_w, v)
    g_block = jnp.exp(L * log_g)[:, None, None]

    def step(s, ds_c):
        return g_block * s + ds_c, s

    s_fin, s_prev = lax.scan(step, s0, jnp.moveaxis(ds, 1, 0))
    s_prev = jnp.moveaxis(s_prev, 0, 1)
    q_w = q * jnp.exp((i + 1.0)[:, None] * log_g[None, :])[:, :, None]
    o_cross = jnp.einsum('bcihd,bchde->bcihe', q_w, s_prev)
    return o_inner + o_cross, s_fin


def _fox_prompt(q, k, v, logf):
    b, s, h, d = q.shape
    nb = s // FOX_QBLOCK
    scale = d ** -0.5
    c_t = jnp.transpose(jnp.cumsum(logf, axis=1), (0, 2, 1))
    kpos = jnp.arange(s)
    qb = jnp.moveaxis(q.reshape(b, nb, FOX_QBLOCK, h, d), 1, 0)
    cqb = jnp.moveaxis(c_t.reshape(b, h, nb, FOX_QBLOCK), 2, 0)
    qpos = kpos.reshape(nb, FOX_QBLOCK)

    def block(args):
        qi, cqi, pi = args
        sc = jnp.einsum('bqhd,bkhd->bhqk', qi, k, preferred_element_type=jnp.float32) * scale
        sc = sc + cqi[..., None] - c_t[:, :, None, :]
        sc = jnp.where(kpos[None, :] <= pi[:, None], sc, -jnp.inf)
        p = jax.nn.softmax(sc, axis=-1)
        return jnp.einsum('bhqk,bkhd->bqhd', p.astype(v.dtype), v)

    o = lax.map(block, (qb, cqb, qpos))
    return jnp.moveaxis(o, 0, 1).reshape(b, s, h, d)


def _fox_sample(q, k, v, logf, ck, cv, clogf):
    p_len = ck.shape[1]
    n = q.shape[1]
    scale = q.shape[-1] ** -0.5
    k_all = jnp.concatenate([ck.astype(k.dtype), k], axis=1)
    v_all = jnp.concatenate([cv.astype(v.dtype), v], axis=1)
    lf_all = jnp.concatenate([clogf.astype(jnp.float32), logf], axis=1)
    c_t = jnp.transpose(jnp.cumsum(lf_all, axis=1), (0, 2, 1))
    cq = c_t[:, :, p_len:]
    sc = jnp.einsum('bqhd,bkhd->bhqk', q, k_all, preferred_element_type=jnp.float32) * scale
    sc = sc + cq[..., None] - c_t[:, :, None, :]
    mask = jnp.arange(p_len + n)[None, :] <= (p_len + jnp.arange(n))[:, None]
    sc = jnp.where(mask, sc, -jnp.inf)
    p = jax.nn.softmax(sc, axis=-1)
    return jnp.einsum('bhqk,bkhd->bqhd', p.astype(v_all.dtype), v_all)


def _layer(x, pos, hist, prm, prompt):
    (norm_mix, w_in, ret_gn_gain, fox_q_gain, fox_k_gain, fox_f_bias, conv_w,
     w_out, norm_ffn, w_gate, w_up, ffn_conv_w, w_down) = prm
    f32 = jnp.float32
    b, s, _ = x.shape
    h = _rms_norm(x, norm_mix)
    proj = h @ w_in
    split_at = np.cumsum(PROJ_SIZES)[:-1].tolist()
    rq, rk, rv, rg, fq, fk, fv, fl, cb, cc, ch = jnp.split(proj, split_at, axis=-1)

    q = _rotary(rq.reshape(b, s, RET_HEADS, RET_HEAD_DIM), pos).astype(f32)
    k = _rotary(rk.reshape(b, s, RET_HEADS, RET_HEAD_DIM), pos).astype(f32) * RET_HEAD_DIM ** -0.5
    v = rv.reshape(b, s, RET_HEADS, RET_HEAD_DIM).astype(f32)
    if prompt:
        n_c = s // CHUNK
        s0 = jnp.zeros((b, RET_HEADS, RET_HEAD_DIM, RET_HEAD_DIM), f32)
    else:
        n_c = 1
        s0 = hist[3].astype(f32)
    blk = lambda t: t.reshape(b, n_c, s // n_c, RET_HEADS, RET_HEAD_DIM)
    o_ret, ret_state = _retention(blk(q), blk(k), blk(v), s0)
    o_ret = _head_group_norm(o_ret.reshape(b, s, RET_HEADS, RET_HEAD_DIM),
                             ret_gn_gain.reshape(RET_HEADS, RET_HEAD_DIM))
    o_ret = (o_ret.reshape(b, s, RET_WIDTH) * jax.nn.silu(rg.astype(f32))).astype(x.dtype)

    fqh = _rms_norm(fq.reshape(b, s, FOX_HEADS, FOX_HEAD_DIM), fox_q_gain)
    fkh = _rms_norm(fk.reshape(b, s, FOX_HEADS, FOX_HEAD_DIM), fox_k_gain)
    fvh = fv.reshape(b, s, FOX_HEADS, FOX_HEAD_DIM)
    logf = jax.nn.log_sigmoid(fl.astype(f32) + fox_f_bias.astype(f32))
    if prompt:
        o_fox = _fox_prompt(fqh, fkh, fvh, logf)
    else:
        o_fox = _fox_sample(fqh, fkh, fvh, logf, hist[0], hist[1], hist[2])
    o_fox = o_fox.reshape(b, s, FOX_WIDTH).astype(x.dtype)

    u = cc * ch
    conv_hist = jnp.zeros((b, CONV_W - 1, CONV_DIM), u.dtype) if prompt else hist[4]
    y_conv, new_conv = _causal_dwconv(u, conv_hist, conv_w)
    o_conv = (cb * y_conv).astype(x.dtype)

    x = x + jnp.concatenate([o_ret, o_fox, o_conv], axis=-1) @ w_out

    h2 = _rms_norm(x, norm_ffn)
    gate_pre = h2 @ w_gate
    ffn_hist = jnp.zeros((b, FFN_CONV_W - 1, D_FF), gate_pre.dtype) if prompt else hist[5]
    gate_c, new_ffn = _causal_dwconv(gate_pre, ffn_hist, ffn_conv_w)
    x = x + (jax.nn.silu(gate_c) * (h2 @ w_up)) @ w_down
    return x, (fkh, fvh, logf, ret_state, new_conv, new_ffn)


def setup_inputs(seed: int = 0) -> dict:
    key = jax.random.key(seed)
    ks = jax.random.split(key, 24)
    f32 = jnp.float32
    nrm = lambda k, shape, sc: jax.random.normal(k, shape, f32) * sc
    return {
        "x_prompt": nrm(ks[0], (BATCH, SEQ, D_MODEL), 1.0),
        "x_sample": nrm(ks[1], (DEC_BATCH, DEC_SEQ, D_MODEL), 1.0),
        "cache_fox_k": nrm(ks[2], (DEPTH, DEC_BATCH, PAST_LEN, FOX_HEADS, FOX_HEAD_DIM), 1.0),
        "cache_fox_v": nrm(ks[3], (DEPTH, DEC_BATCH, PAST_LEN, FOX_HEADS, FOX_HEAD_DIM), 1.0),
        "cache_fox_logf": jax.nn.log_sigmoid(jax.random.uniform(ks[4], (DEPTH, DEC_BATCH, PAST_LEN, FOX_HEADS), f32, 1.0, 6.0)),
        "state_ret": nrm(ks[5], (DEPTH, DEC_BATCH, RET_HEADS, RET_HEAD_DIM, RET_HEAD_DIM), 0.3),
        "state_conv": nrm(ks[6], (DEPTH, DEC_BATCH, CONV_W - 1, CONV_DIM), 1.0),
        "state_ffn_conv": nrm(ks[7], (DEPTH, DEC_BATCH, FFN_CONV_W - 1, D_FF), 1.0),
        "norm_mix": 1.0 + nrm(ks[8], (DEPTH, D_MODEL), 0.02),
        "w_in": nrm(ks[9], (DEPTH, D_MODEL, PROJ_DIM), D_MODEL ** -0.5),
        "ret_gn_gain": 1.0 + nrm(ks[10], (DEPTH, RET_WIDTH), 0.02),
        "fox_q_gain": 1.0 + nrm(ks[11], (DEPTH, FOX_HEADS, FOX_HEAD_DIM), 0.02),
        "fox_k_gain": 1.0 + nrm(ks[12], (DEPTH, FOX_HEADS, FOX_HEAD_DIM), 0.02),
        "fox_f_bias": jax.random.uniform(ks[13], (DEPTH, FOX_HEADS), f32, 1.0, 6.0),
        "conv_w": nrm(ks[14], (DEPTH, CONV_W, CONV_DIM), CONV_W ** -0.5),
        "w_out": nrm(ks[15], (DEPTH, MIX_WIDTH, D_MODEL), MIX_WIDTH ** -0.5),
        "norm_ffn": 1.0 + nrm(ks[16], (DEPTH, D_MODEL), 0.02),
        "w_gate": nrm(ks[17], (DEPTH, D_MODEL, D_FF), D_MODEL ** -0.5),
        "w_up": nrm(ks[18], (DEPTH, D_MODEL, D_FF), D_MODEL ** -0.5),
        "ffn_conv_w": nrm(ks[19], (DEPTH, FFN_CONV_W, D_FF), FFN_CONV_W ** -0.5),
        "w_down": nrm(ks[20], (DEPTH, D_FF, D_MODEL), D_FF ** -0.5),
    }


def reference(x_prompt, x_sample, cache_fox_k, cache_fox_v, cache_fox_logf, state_ret,
              state_conv, state_ffn_conv, norm_mix, w_in, ret_gn_gain, fox_q_gain, fox_k_gain,
              fox_f_bias, conv_w, w_out, norm_ffn, w_gate, w_up, ffn_conv_w, w_down):
    pos_p = jnp.arange(x_prompt.shape[1], dtype=jnp.int32)
    pos_s = PAST_LEN + jnp.arange(x_sample.shape[1], dtype=jnp.int32)
    y_prompt, y_sample = x_prompt, x_sample
    p_st, s_st = [], []
    for l in range(DEPTH):
        prm = (norm_mix[l], w_in[l], ret_gn_gain[l], fox_q_gain[l], fox_k_gain[l], fox_f_bias[l],
               conv_w[l], w_out[l], norm_ffn[l], w_gate[l], w_up[l], ffn_conv_w[l], w_down[l])
        y_prompt, st = _layer(y_prompt, pos_p, None, prm, True)
        p_st.append(st)
        hist = (cache_fox_k[l], cache_fox_v[l], cache_fox_logf[l], state_ret[l],
                state_conv[l], state_ffn_conv[l])
        y_sample, st = _layer(y_sample, pos_s, hist, prm, False)
        s_st.append(st)
    stk = lambda sts, i: jnp.stack([t[i] for t in sts], axis=0)
    return (y_prompt, y_sample,
            stk(p_st, 0), stk(p_st, 1), stk(p_st, 2), stk(p_st, 3), stk(p_st, 4), stk(p_st, 5),
            stk(s_st, 0), stk(s_st, 1), stk(s_st, 2), stk(s_st, 3), stk(s_st, 4), stk(s_st, 5))
```

```python
import functools

import numpy as np
import jax
import jax.numpy as jnp
from jax import lax
from jax.experimental import pallas as pl
from jax.experimental.pallas import tpu as pltpu

F32 = jnp.float32
BF16 = jnp.bfloat16

D_MODEL = 1024
RET_HEADS = 4
RET_WIDTH = 256
FOX_HEADS = 8
FOX_HEAD_DIM = 64
FOX_WIDTH = 512
CONV_DIM = 256
D_FF = 2816
PROJ_PAD = 3456
ROPE_BASE = 10000.0
NORM_EPS = 1e-6
LANES = 128
SUBLANES = 8
VMEM_LIMIT = 56 * 1024 * 1024

TOKEN_TILE = 512
RET_BLOCK = 256
FOX_TILE = 512
MXU_DEPTH = 256
FF_CHUNKS = ((0, 5 * MXU_DEPTH), (5 * MXU_DEPTH, D_FF))


def _dot(a, b):
    return jnp.dot(a, b, preferred_element_type=F32)


def _dot_nt(a, b):
    return lax.dot_general(a, b, (((1,), (1,)), ((), ())), preferred_element_type=F32)


def _split2(x):
    hi = x.astype(BF16)
    lo = (x - hi.astype(F32)).astype(BF16)
    return hi, lo


def _group_sum(x, ones_bd):
    hi, lo = _split2(x)
    return _dot(hi, ones_bd) + _dot(lo, ones_bd)


def _rms(x, g):
    ms = jnp.mean(x * x, axis=-1, keepdims=True)
    return x * lax.rsqrt(ms + NORM_EPS) * g


def _dwconv_carry(u, w_ref, carry_ref, first):
    tm = u.shape[0]

    @pl.when(first)
    def _():
        carry_ref[...] = jnp.zeros(carry_ref.shape, F32)

    row = lax.broadcasted_iota(jnp.int32, u.shape, 0)
    c6 = carry_ref[6:7, :]
    c7 = carry_ref[7:8, :]
    um1 = jnp.where(row == 0, c7, pltpu.roll(u, 1, axis=0))
    um2 = jnp.where(row == 0, c6, jnp.where(row == 1, c7, pltpu.roll(u, 2, axis=0)))
    y = w_ref[0:1, :] * um2 + w_ref[1:2, :] * um1 + w_ref[2:3, :] * u
    carry_ref[...] = u[tm - SUBLANES:tm, :]
    return y


def _dwconv_tmajor(u, w_ref, hist, nb):
    tm = u.shape[0]
    full = jnp.concatenate([hist, u], axis=0)
    y = (w_ref[0:1, :] * full[0:tm] + w_ref[1:2, :] * full[nb:nb + tm]
         + w_ref[2:3, :] * full[2 * nb:2 * nb + tm])
    return y, full[tm:tm + 2 * nb]


def _inproj_kernel(*refs, tmajor, nb):
    if tmajor:
        (x_ref, nw_ref, w_ref, cos_ref, sin_ref, gq_ref, gk_ref, fb_ref, cw_ref, bd_ref, hist_ref,
         ret_ref, fq_ref, fk32_ref, fv32_ref, fkb_ref, fvb_ref, lf_ref, oc_ref, st_ref) = refs
    else:
        (x_ref, nw_ref, w_ref, cos_ref, sin_ref, gq_ref, gk_ref, fb_ref, cw_ref, bd_ref,
         _, _, ret_ref, fq_ref, fk32_ref, fv32_ref, fkb_ref, fvb_ref, lf_ref, oc_ref, st_ref,
         carry_ref) = refs
    h = _rms(x_ref[...], nw_ref[...]).astype(BF16)

    a = _dot(h, w_ref[:, 0:1024])
    cos = cos_ref[...]
    sin = sin_ref[...]
    q1, q2, k1, k2 = a[:, 0:128], a[:, 128:256], a[:, 256:384], a[:, 384:512]
    ret_ref[:, 0:128] = (q1 * cos - q2 * sin).astype(BF16)
    ret_ref[:, 128:256] = (q1 * sin + q2 * cos).astype(BF16)
    ret_ref[:, 256:384] = ((k1 * cos - k2 * sin) * 0.125).astype(BF16)
    ret_ref[:, 384:512] = ((k1 * sin + k2 * cos) * 0.125).astype(BF16)
    ret_ref[:, 512:1024] = a[:, 512:1024].astype(BF16)

    f = _dot(h, w_ref[:, 1024:2560])
    fq, fk, fv = f[:, 0:512], f[:, 512:1024], f[:, 1024:1536]
    bd = bd_ref[...]
    inv_d = 1.0 / FOX_HEAD_DIM
    ssq = _group_sum(fq * fq, bd)
    fq_ref[...] = (fq * lax.rsqrt(ssq * inv_d + NORM_EPS) * gq_ref[...] * 0.125).astype(BF16)
    ssk = _group_sum(fk * fk, bd)
    fkn = fk * lax.rsqrt(ssk * inv_d + NORM_EPS) * gk_ref[...]
    fk32_ref[...] = fkn
    fkb_ref[...] = fkn.astype(BF16)
    fv32_ref[...] = fv
    fvb_ref[...] = fv.astype(BF16)

    c = _dot(h, w_ref[:, 2560:3328])
    cb, cc, ch = c[:, 0:256], c[:, 256:512], c[:, 512:768]
    u = cc * ch
    if tmajor:
        y, new_hist = _dwconv_tmajor(u, cw_ref, hist_ref[...], nb)
        st_ref[...] = new_hist
    else:
        y = _dwconv_carry(u, cw_ref, carry_ref, pl.program_id(1) == 0)
        st_ref[...] = carry_ref[6:8, :]
    oc_ref[...] = (cb * y).astype(BF16)

    z = _dot(h, w_ref[:, 3328:3456]) + fb_ref[...]
    ls = jnp.minimum(z, 0.0) - jnp.log1p(jnp.exp(-jnp.abs(z)))
    lf_ref[...] = ls[:, 0:FOX_HEADS]


def _inproj(l, x, prm, cos, sin, bd, *, kbuf=None, vbuf=None, hist=None, nb=0):
    tmajor = hist is not None
    g, r, _ = x.shape
    tm = min(TOKEN_TILE, r)
    grid = (g, r // tm)
    depth = prm["w_in"].shape[0]
    lsel = lambda b, i: (l, 0, 0)
    in_specs = [
        pl.BlockSpec((None, tm, D_MODEL), lambda b, i: (b, i, 0)),
        pl.BlockSpec((None, 1, D_MODEL), lsel),
        pl.BlockSpec((None, D_MODEL, PROJ_PAD), lsel, pipeline_mode=pl.Buffered(1)),
        pl.BlockSpec((tm, LANES), lambda b, i: (i, 0)),
        pl.BlockSpec((tm, LANES), lambda b, i: (i, 0)),
        pl.BlockSpec((None, 1, FOX_WIDTH), lsel),
        pl.BlockSpec((None, 1, FOX_WIDTH), lsel),
        pl.BlockSpec((None, 1, LANES), lsel),
        pl.BlockSpec((None, 3, CONV_DIM), lsel),
        pl.BlockSpec((FOX_WIDTH, FOX_WIDTH), lambda b, i: (0, 0)),
    ]
    args = [x, prm["norm_mix"], prm["w_in"], cos, sin, prm["gq"], prm["gk"], prm["fb"], prm["conv_w"], bd]
    tok = lambda width: pl.BlockSpec((None, tm, width), lambda b, i: (b, i, 0))
    if tmajor:
        in_specs.append(pl.BlockSpec((2 * nb, CONV_DIM), lambda b, i: (0, 0)))
        args.append(hist)
        kv_shape = jax.ShapeDtypeStruct((g, r, FOX_WIDTH), F32)
        kv_spec = tok(FOX_WIDTH)
        st_shape = jax.ShapeDtypeStruct((2 * nb, CONV_DIM), F32)
        st_spec = pl.BlockSpec((2 * nb, CONV_DIM), lambda b, i: (0, 0))
        aliases = {}
        scratch = []
    else:
        in_specs += [pl.BlockSpec(memory_space=pl.ANY), pl.BlockSpec(memory_space=pl.ANY)]
        args += [kbuf, vbuf]
        kv_shape = jax.ShapeDtypeStruct((depth, g, r, FOX_WIDTH), F32)
        kv_spec = pl.BlockSpec((None, None, tm, FOX_WIDTH), lambda b, i: (l, b, i, 0))
        st_shape = jax.ShapeDtypeStruct((g, 2, CONV_DIM), F32)
        st_spec = pl.BlockSpec((None, 2, CONV_DIM), lambda b, i: (b, 0, 0))
        aliases = {10: 2, 11: 3}
        scratch = [pltpu.VMEM((SUBLANES, CONV_DIM), F32)]
    out_shape = [
        jax.ShapeDtypeStruct((g, r, 1024), BF16),
        jax.ShapeDtypeStruct((g, r, FOX_WIDTH), BF16),
        kv_shape, kv_shape,
        jax.ShapeDtypeStruct((g, r, FOX_WIDTH), BF16),
        jax.ShapeDtypeStruct((g, r, FOX_WIDTH), BF16),
        jax.ShapeDtypeStruct((g, r, FOX_HEADS), F32),
        jax.ShapeDtypeStruct((g, r, CONV_DIM), BF16),
        st_shape,
    ]
    out_specs = [tok(1024), tok(FOX_WIDTH), kv_spec, kv_spec, tok(FOX_WIDTH), tok(FOX_WIDTH),
                 tok(FOX_HEADS), tok(CONV_DIM), st_spec]
    return pl.pallas_call(
        functools.partial(_inproj_kernel, tmajor=tmajor, nb=nb),
        grid=grid, in_specs=in_specs, out_specs=out_specs, out_shape=out_shape,
        scratch_shapes=scratch, input_output_aliases=aliases,
        compiler_params=pltpu.CompilerParams(dimension_semantics=("arbitrary", "arbitrary"),
                                             vmem_limit_bytes=VMEM_LIMIT),
        name="inproj",
    )(*args)


def _ret_kernel(x_ref, s0_ref, gain_ref, dec_ref, qs_ref, ks_ref, gbm_ref, bm_ref, bdv_ref,
                o_ref, sout_ref, s_ref, *, blk):
    @pl.when(pl.program_id(1) == 0)
    def _():
        s_ref[...] = s0_ref[...]

    inv_d = 1.0 / 64.0
    bdv = bdv_ref[...]
    lane = lax.broadcasted_iota(jnp.int32, (1, RET_WIDTH), 1)
    head_k = (lane % 128) // 32
    head_v = lane // 64
    for sb in range(x_ref.shape[0] // blk):
        rows = slice(sb * blk, (sb + 1) * blk)
        q = x_ref[rows, 0:256]
        k = x_ref[rows, 256:512]
        v = x_ref[rows, 512:768]
        gate = x_ref[rows, 768:1024].astype(F32)
        state = s_ref[...]
        qw = (q.astype(F32) * qs_ref[...]).astype(BF16)
        o = _dot(qw, state.astype(BF16))
        for hd in range(RET_HEADS):
            qh = jnp.where(head_k == hd, q, jnp.zeros_like(q))
            a = _dot_nt(qh, k) * dec_ref[hd]
            o = o + jnp.where(head_v == hd, _dot(a.astype(BF16), v), 0.0)
        kw_t = (k.astype(F32) * ks_ref[...]).T.astype(BF16)
        s_ref[...] = gbm_ref[...] * state + bm_ref[...] * _dot(kw_t, v)
        mu = _group_sum(o, bdv) * inv_d
        d = o - mu
        var = _group_sum(d * d, bdv) * inv_d
        on = d * lax.rsqrt(var + NORM_EPS) * gain_ref[...]
        o_ref[rows, :] = (on * (gate / (1.0 + jnp.exp(-gate)))).astype(BF16)
    sout_ref[...] = s_ref[...]


def _ret_constants(blk, chunk_len):
    log_g = jnp.log1p(-jnp.exp2(-5.0 - jnp.arange(RET_HEADS, dtype=F32)))
    lane = np.arange(RET_WIDTH)
    head_k = (lane % 128) // 32
    head_v = lane // 64
    i = jnp.arange(blk, dtype=F32)
    diff = i[:, None] - i[None, :]
    dec = jnp.where(diff[None] >= 0.0, jnp.exp(jnp.maximum(diff, 0.0)[None] * log_g[:, None, None]), 0.0)
    lg_k = log_g[head_k]
    qs = jnp.exp((i + 1.0)[:, None] * lg_k[None, :])
    ks = jnp.exp((chunk_len - 1.0 - i)[:, None] * lg_k[None, :])
    gbm = jnp.broadcast_to(jnp.exp(chunk_len * lg_k)[:, None], (RET_WIDTH, RET_WIDTH))
    bm = jnp.asarray(head_k[:, None] == head_v[None, :], F32)
    bdv = jnp.asarray(head_v[:, None] == head_v[None, :], BF16)
    return dec, qs, ks, gbm, bm, bdv


def _retention(l, ret_in, s0, gain, consts, blk, rt):
    g, r, _ = ret_in.shape
    grid = (g, r // rt)
    const2 = lambda shape: pl.BlockSpec(shape, lambda b, i: (0,) * len(shape))
    dec, qs, ks, gbm, bm, bdv = consts
    in_specs = [
        pl.BlockSpec((None, rt, 1024), lambda b, i: (b, i, 0)),
        pl.BlockSpec((None, RET_WIDTH, RET_WIDTH), lambda b, i: (b, 0, 0)),
        pl.BlockSpec((None, 1, RET_WIDTH), lambda b, i: (l, 0, 0)),
        const2(dec.shape), const2(qs.shape), const2(ks.shape), const2(gbm.shape), const2(bm.shape),
        const2(bdv.shape),
    ]
    return pl.pallas_call(
        functools.partial(_ret_kernel, blk=blk),
        grid=grid, in_specs=in_specs,
        out_specs=[pl.BlockSpec((None, rt, RET_WIDTH), lambda b, i: (b, i, 0)),
                   pl.BlockSpec((None, RET_WIDTH, RET_WIDTH), lambda b, i: (b, 0, 0))],
        out_shape=[jax.ShapeDtypeStruct((g, r, RET_WIDTH), BF16),
                   jax.ShapeDtypeStruct((g, RET_WIDTH, RET_WIDTH), F32)],
        scratch_shapes=[pltpu.VMEM((RET_WIDTH, RET_WIDTH), F32)],
        compiler_params=pltpu.CompilerParams(dimension_semantics=("arbitrary", "arbitrary"),
                                             vmem_limit_bytes=VMEM_LIMIT),
        name="retention",
    )(ret_in, s0, gain, dec, qs, ks, gbm, bm, bdv)


def _state_to_bd(st):
    g = st.shape[0]
    t = jnp.einsum("bhpie,hg->bphige", st.reshape(g, RET_HEADS, 2, 32, 64), jnp.eye(RET_HEADS, dtype=st.dtype))
    return t.reshape(g, RET_WIDTH, RET_WIDTH)


def _bd_to_state(sbd):
    g = sbd.shape[0]
    t = jnp.einsum("bphihe->bhpie", sbd.reshape(g, 2, RET_HEADS, 32, RET_HEADS, 64))
    return t.reshape(g, RET_HEADS, 64, 64)


def _cumsum_kernel(x_ref, u_ref, o_ref):
    nc = x_ref.shape[1] // LANES
    xs = jnp.concatenate([x_ref[:, LANES * j:LANES * (j + 1)] for j in range(nc)], axis=0)
    x1 = xs.astype(BF16)
    r1 = xs - x1.astype(F32)
    x2 = r1.astype(BF16)
    x3 = (r1 - x2.astype(F32)).astype(BF16)
    tri = u_ref[...]
    loc = _dot(x1, tri) + _dot(x2, tri) + _dot(x3, tri)
    off = jnp.zeros((SUBLANES, 1), F32)
    for j in range(nc):
        lj = loc[SUBLANES * j:SUBLANES * (j + 1), :]
        o_ref[:, LANES * j:LANES * (j + 1)] = lj + off
        off = off + lj[:, LANES - 1:LANES]


def _cumsum_lanes(x):
    n, h, w = x.shape
    tri = jnp.asarray(np.arange(LANES)[:, None] <= np.arange(LANES)[None, :], BF16)
    return pl.pallas_call(
        _cumsum_kernel,
        grid=(n,),
        in_specs=[pl.BlockSpec((None, h, w), lambda b: (b, 0, 0)),
                  pl.BlockSpec((LANES, LANES), lambda b: (0, 0))],
        out_specs=pl.BlockSpec((None, h, w), lambda b: (b, 0, 0)),
        out_shape=jax.ShapeDtypeStruct((n, h, w), F32),
        compiler_params=pltpu.CompilerParams(dimension_semantics=("arbitrary",)),
        name="forget_cumsum",
    )(x, tri)


def _fox_prompt_kernel(q_ref, k_ref, v_ref, c_ref, o_ref, m_ref, l_ref, acc_ref, *, tile):
    s_len = q_ref.shape[0]
    lane = lax.broadcasted_iota(jnp.int32, (1, LANES), 1)
    row = lax.broadcasted_iota(jnp.int32, (tile, tile), 0)
    col = lax.broadcasted_iota(jnp.int32, (tile, tile), 1)

    def update(s, vt):
        m_prev = m_ref[...]
        m_new = jnp.maximum(m_prev, jnp.max(s, axis=-1, keepdims=True))
        alpha = jnp.exp(m_prev - m_new)
        p = jnp.exp(s - m_new[:, 0:1])
        l_ref[...] = alpha * l_ref[...] + jnp.sum(p, axis=-1, keepdims=True)
        acc_ref[...] = alpha * acc_ref[...] + _dot(p.astype(BF16), vt)
        m_ref[...] = m_new

    def q_body(qi, carry):
        q0 = pl.multiple_of(qi * tile, tile)
        q = q_ref[pl.ds(q0, tile), :]
        outs = []
        for hd in range(2):
            qh = jnp.where(lane // FOX_HEAD_DIM == hd, q, jnp.zeros_like(q))
            m_ref[...] = jnp.full(m_ref.shape, -jnp.inf, F32)
            l_ref[...] = jnp.zeros(l_ref.shape, F32)
            acc_ref[...] = jnp.zeros(acc_ref.shape, F32)

            def kv_body(kj, c2):
                k0 = pl.multiple_of(kj * tile, tile)
                s = _dot_nt(qh, k_ref[pl.ds(k0, tile), :]) - c_ref[kj, hd:hd + 1, :]
                update(s, v_ref[pl.ds(k0, tile), :])
                return c2

            lax.fori_loop(0, qi, kv_body, 0)
            s = _dot_nt(qh, k_ref[pl.ds(q0, tile), :]) - c_ref[qi, hd:hd + 1, :]
            update(jnp.where(col <= row, s, -jnp.inf), v_ref[pl.ds(q0, tile), :])
            outs.append(acc_ref[...] / l_ref[...])
        o_ref[pl.ds(q0, tile), :] = jnp.where(lane < FOX_HEAD_DIM, outs[0], outs[1]).astype(BF16)
        return carry

    lax.fori_loop(0, s_len // tile, q_body, 0)


def _fox_prompt(q, k, v, c_tiles, tile):
    b, s, _ = q.shape
    nt = s // tile
    qkv_spec = pl.BlockSpec((None, s, LANES), lambda bi, hp: (bi, 0, hp))
    return pl.pallas_call(
        functools.partial(_fox_prompt_kernel, tile=tile),
        grid=(b, FOX_HEADS // 2),
        in_specs=[qkv_spec, qkv_spec, qkv_spec,
                  pl.BlockSpec((None, None, nt, 2, tile), lambda bi, hp: (bi, hp, 0, 0, 0))],
        out_specs=qkv_spec,
        out_shape=jax.ShapeDtypeStruct((b, s, FOX_WIDTH), BF16),
        scratch_shapes=[pltpu.VMEM((tile, LANES), F32), pltpu.VMEM((tile, LANES), F32),
                        pltpu.VMEM((tile, LANES), F32)],
        compiler_params=pltpu.CompilerParams(dimension_semantics=("arbitrary", "arbitrary"),
                                             vmem_limit_bytes=VMEM_LIMIT),
        name="fox_prompt",
    )(q, k, v, c_tiles)


def _fox_sample_kernel(q_ref, kc_ref, vc_ref, kn_ref, vn_ref, c_ref, o_ref, *, past, n):
    lane = lax.broadcasted_iota(jnp.int32, (1, LANES), 1)
    row = lax.broadcasted_iota(jnp.int32, (n, n), 0)
    col = lax.broadcasted_iota(jnp.int32, (n, n), 1)
    q = q_ref[...]
    kc = kc_ref[...].astype(BF16)
    vc = vc_ref[...].astype(BF16)
    kn = kn_ref[...]
    vn = vn_ref[...]
    outs = []
    for hd in range(2):
        qh = jnp.where(lane // FOX_HEAD_DIM == hd, q, jnp.zeros_like(q))
        ctot = c_ref[hd:hd + 1, past - 1:past]
        sc = _dot_nt(qh, kc) + (ctot - c_ref[hd:hd + 1, 0:past])
        sn = _dot_nt(qh, kn) + (ctot - c_ref[hd:hd + 1, past:past + n])
        sn = jnp.where(col <= row, sn, -jnp.inf)
        m = jnp.maximum(jnp.max(sc, axis=-1, keepdims=True), jnp.max(sn, axis=-1, keepdims=True))
        pc = jnp.exp(sc - m)
        pn = jnp.exp(sn - m)
        denom = jnp.sum(pc, axis=-1, keepdims=True) + jnp.sum(pn, axis=-1, keepdims=True)
        outs.append((_dot(pc.astype(BF16), vc) + _dot(pn.astype(BF16), vn)) / denom)
    o_ref[...] = jnp.where(lane < FOX_HEAD_DIM, outs[0], outs[1]).astype(BF16)


def _fox_sample(l, q, kn, vn, cache_k, cache_v, c_all):
    nb, n, _ = q.shape
    past = cache_k.shape[2]
    new_spec = pl.BlockSpec((None, n, LANES), lambda b, hp: (b, 0, hp))
    cache_spec = pl.BlockSpec((None, None, past, LANES), lambda b, hp: (l, b, 0, hp))
    return pl.pallas_call(
        functools.partial(_fox_sample_kernel, past=past, n=n),
        grid=(nb, FOX_HEADS // 2),
        in_specs=[new_spec, cache_spec, cache_spec, new_spec, new_spec,
                  pl.BlockSpec((None, None, 2, c_all.shape[-1]), lambda b, hp: (b, hp, 0, 0))],
        out_specs=new_spec,
        out_shape=jax.ShapeDtypeStruct((nb, n, FOX_WIDTH), BF16),
        compiler_params=pltpu.CompilerParams(dimension_semantics=("arbitrary", "arbitrary"),
                                             vmem_limit_bytes=VMEM_LIMIT),
        name="fox_sample",
    )(q, cache_k, cache_v, kn, vn, c_all)


def _mlp_kernel(*refs, tmajor, nb):
    if tmajor:
        (x_ref, oret_ref, ofox_ref, oconv_ref, wout_ref, nw_ref, wg_ref, wu_ref, wd_ref, fcw_ref, hist_ref,
         y_ref, st_ref) = refs
    else:
        (x_ref, oret_ref, ofox_ref, oconv_ref, wout_ref, nw_ref, wg_ref, wu_ref, wd_ref, fcw_ref,
         y_ref, st_ref, carry_ref) = refs
    x1 = (x_ref[...] + _dot(oret_ref[...], wout_ref[0:256, :]) + _dot(ofox_ref[...], wout_ref[256:768, :])
          + _dot(oconv_ref[...], wout_ref[768:1024, :]))
    h2 = _rms(x1, nw_ref[...]).astype(BF16)
    down = None
    for c0, c1 in FF_CHUNKS:
        cols = slice(c0, c1)
        gate_pre = _dot(h2, wg_ref[:, cols])
        if tmajor:
            gate_c, new_hist = _dwconv_tmajor(gate_pre, fcw_ref.at[:, cols], hist_ref[:, cols], nb)
            st_ref[:, cols] = new_hist
        else:
            gate_c = _dwconv_carry(gate_pre, fcw_ref.at[:, cols], carry_ref.at[:, cols], pl.program_id(1) == 0)
            st_ref[:, cols] = carry_ref[6:8, cols]
        up = _dot(h2, wu_ref[:, cols])
        act = (gate_c / (1.0 + jnp.exp(-gate_c)) * up).astype(BF16)
        part = _dot(act, wd_ref[cols, :])
        down = part if down is None else down + part
    y_ref[...] = x1 + down


def _mlp(l, x, oret, ofox, oconv, prm, *, hist=None, nb=0):
    tmajor = hist is not None
    g, r, _ = x.shape
    tm = min(TOKEN_TILE, r)
    grid = (g, r // tm)
    lsel = lambda b, i: (l, 0, 0)
    tok = lambda width: pl.BlockSpec((None, tm, width), lambda b, i: (b, i, 0))
    wspec = lambda rows, cols: pl.BlockSpec((None, rows, cols), lsel, pipeline_mode=pl.Buffered(1))
    in_specs = [tok(D_MODEL), tok(RET_WIDTH), tok(FOX_WIDTH), tok(CONV_DIM),
                wspec(D_MODEL, D_MODEL), pl.BlockSpec((None, 1, D_MODEL), lsel),
                wspec(D_MODEL, D_FF), wspec(D_MODEL, D_FF), wspec(D_FF, D_MODEL),
                pl.BlockSpec((None, 3, D_FF), lsel)]
    args = [x, oret, ofox, oconv, prm["w_out"], prm["norm_ffn"], prm["w_gate"], prm["w_up"], prm["w_down"],
            prm["ffn_conv_w"]]
    if tmajor:
        in_specs.append(pl.BlockSpec((2 * nb, D_FF), lambda b, i: (0, 0)))
        args.append(hist)
        st_shape = jax.ShapeDtypeStruct((2 * nb, D_FF), F32)
        st_spec = pl.BlockSpec((2 * nb, D_FF), lambda b, i: (0, 0))
        scratch = []
    else:
        st_shape = jax.ShapeDtypeStruct((g, 2, D_FF), F32)
        st_spec = pl.BlockSpec((None, 2, D_FF), lambda b, i: (b, 0, 0))
        scratch = [pltpu.VMEM((SUBLANES, D_FF), F32)]
    return pl.pallas_call(
        functools.partial(_mlp_kernel, tmajor=tmajor, nb=nb),
        grid=grid, in_specs=in_specs,
        out_specs=[tok(D_MODEL), st_spec],
        out_shape=[jax.ShapeDtypeStruct((g, r, D_MODEL), F32), st_shape],
        scratch_shapes=scratch,
        compiler_params=pltpu.CompilerParams(dimension_semantics=("arbitrary", "arbitrary"),
                                             vmem_limit_bytes=VMEM_LIMIT),
        name="outproj_mlp",
    )(*args)


def _win_permutation():
    idx = []
    for base in (0, 256):
        for half in range(2):
            for hd in range(RET_HEADS):
                idx += [base + 64 * hd + 32 * half + i for i in range(32)]
    idx += list(range(512, 1024))
    idx += list(range(1024, 2560))
    idx += list(range(2568, 3336))
    idx += list(range(2560, 2568))
    return np.asarray(idx, np.int32)


def _rope_tables(pos):
    half = 32
    inv_freq = ROPE_BASE ** (-jnp.arange(half, dtype=F32) / half)
    ang = pos.astype(F32)[:, None] * inv_freq[None, :]
    return jnp.tile(jnp.cos(ang), (1, RET_HEADS)), jnp.tile(jnp.sin(ang), (1, RET_HEADS))


def _to_bmajor(a, n, nb):
    return a.reshape(n, nb, a.shape[-1]).transpose(1, 0, 2)


def _to_tmajor(a):
    nb, n, c = a.shape
    return a.transpose(1, 0, 2).reshape(1, n * nb, c)


def kernel(x_prompt, x_sample, cache_fox_k, cache_fox_v, cache_fox_logf, state_ret, state_conv, state_ffn_conv,
           norm_mix, w_in, ret_gn_gain, fox_q_gain, fox_k_gain, fox_f_bias, conv_w, w_out, norm_ffn, w_gate,
           w_up, ffn_conv_w, w_down):
    depth = w_in.shape[0]
    b, s, _ = x_prompt.shape
    nb, n, _ = x_sample.shape
    past = cache_fox_k.shape[2]

    w_perm = jnp.take(w_in, _win_permutation(), axis=2)
    w_perm = jnp.pad(w_perm, ((0, 0), (0, 0), (0, PROJ_PAD - w_perm.shape[2]))).astype(BF16)
    prm = {
        "w_in": w_perm,
        "norm_mix": norm_mix.reshape(depth, 1, D_MODEL),
        "gq": fox_q_gain.reshape(depth, 1, FOX_WIDTH),
        "gk": fox_k_gain.reshape(depth, 1, FOX_WIDTH),
        "fb": jnp.pad(fox_f_bias, ((0, 0), (0, LANES - FOX_HEADS))).reshape(depth, 1, LANES),
        "conv_w": conv_w,
        "w_out": w_out.astype(BF16),
        "norm_ffn": norm_ffn.reshape(depth, 1, D_MODEL),
        "w_gate": w_gate.astype(BF16),
        "w_up": w_up.astype(BF16),
        "w_down": w_down.astype(BF16),
        "ffn_conv_w": ffn_conv_w,
    }
    gain = ret_gn_gain.reshape(depth, 1, RET_WIDTH)
    lane = np.arange(FOX_WIDTH)
    bd = jnp.asarray(lane[:, None] // FOX_HEAD_DIM == lane[None, :] // FOX_HEAD_DIM, BF16)

    cos_p, sin_p = _rope_tables(jnp.arange(s, dtype=jnp.int32))
    pos_s = past + jnp.arange(n, dtype=jnp.int32)
    cos_s, sin_s = _rope_tables(jnp.repeat(pos_s, nb))

    ret_blk = min(RET_BLOCK, s)
    ret_rt = min(TOKEN_TILE, s)
    consts_p = _ret_constants(ret_blk, float(ret_blk))
    samp_rows = LANES
    consts_s = _ret_constants(samp_rows, float(n))
    fox_tile = min(FOX_TILE, s)

    cache_k = cache_fox_k.reshape(depth, nb, past, FOX_WIDTH)
    cache_v = cache_fox_v.reshape(depth, nb, past, FOX_WIDTH)
    cache_lf_t = jnp.transpose(cache_fox_logf, (0, 1, 3, 2))
    c_width = -(-(past + n) // LANES) * LANES

    kbuf = jnp.zeros((depth, b, s, FOX_WIDTH), F32)
    vbuf = jnp.zeros((depth, b, s, FOX_WIDTH), F32)
    zero_state = jnp.zeros((b, RET_WIDTH, RET_WIDTH), F32)

    xp = x_prompt
    xs = jnp.transpose(x_sample, (1, 0, 2)).reshape(1, n * nb, D_MODEL)
    p_lf, p_ret, p_conv, p_ffn = [], [], [], []
    s_k, s_v, s_lf, s_ret, s_conv, s_ffn = [], [], [], [], [], []
    for l in range(depth):
        ret_in, fq, kbuf, vbuf, fkb, fvb, lf, oconv, conv_st = _inproj(
            l, xp, prm, cos_p, sin_p, bd, kbuf=kbuf, vbuf=vbuf)
        oret, sbd = _retention(l, ret_in, zero_state, gain, consts_p, ret_blk, ret_rt)
        c_rows = _cumsum_lanes(jnp.transpose(lf, (0, 2, 1)))
        c_tiles = c_rows.reshape(b, FOX_HEADS // 2, 2, s // fox_tile, fox_tile).transpose(0, 1, 3, 2, 4)
        ofox = _fox_prompt(fq, fkb, fvb, c_tiles, fox_tile)
        xp, ffn_st = _mlp(l, xp, oret, ofox, oconv, prm)
        p_lf.append(lf)
        p_ret.append(_bd_to_state(sbd))
        p_conv.append(conv_st)
        p_ffn.append(ffn_st)

        hist_c = jnp.transpose(state_conv[l], (1, 0, 2)).reshape(2 * nb, CONV_DIM)
        ret_in, fq, fk32, fv32, fkb, fvb, lf, oconv, conv_st = _inproj(
            l, xs, prm, cos_s, sin_s, bd, hist=hist_c, nb=nb)
        ret_b = jnp.pad(_to_bmajor(ret_in, n, nb), ((0, 0), (0, samp_rows - n), (0, 0)))
        oret, sbd = _retention(l, ret_b, _state_to_bd(state_ret[l]), gain, consts_s, samp_rows, samp_rows)
        lf_b = _to_bmajor(lf, n, nb)
        lf_all = jnp.concatenate(
            [cache_lf_t[l], jnp.transpose(lf_b, (0, 2, 1)),
             jnp.zeros((nb, FOX_HEADS, c_width - past - n), F32)], axis=2)
        c_all = _cumsum_lanes(lf_all).reshape(nb, FOX_HEADS // 2, 2, c_width)
        ofox = _fox_sample(l, _to_bmajor(fq, n, nb), _to_bmajor(fkb, n, nb), _to_bmajor(fvb, n, nb),
                           cache_k, cache_v, c_all)
        hist_f = jnp.transpose(state_ffn_conv[l], (1, 0, 2)).reshape(2 * nb, D_FF)
        xs, ffn_st = _mlp(l, xs, _to_tmajor(oret[:, :n]), _to_tmajor(ofox), oconv, prm, hist=hist_f, nb=nb)
        s_k.append(_to_bmajor(fk32, n, nb).reshape(nb, n, FOX_HEADS, FOX_HEAD_DIM))
        s_v.append(_to_bmajor(fv32, n, nb).reshape(nb, n, FOX_HEADS, FOX_HEAD_DIM))
        s_lf.append(lf_b)
        s_ret.append(_bd_to_state(sbd))
        s_conv.append(conv_st.reshape(2, nb, CONV_DIM).transpose(1, 0, 2))
        s_ffn.append(ffn_st.reshape(2, nb, D_FF).transpose(1, 0, 2))

    y_sample = xs.reshape(n, nb, D_MODEL).transpose(1, 0, 2)
    stk = lambda ts: jnp.stack(ts, axis=0)
    return (xp, y_sample,
            kbuf.reshape(depth, b, s, FOX_HEADS, FOX_HEAD_DIM), vbuf.reshape(depth, b, s, FOX_HEADS, FOX_HEAD_DIM),
            stk(p_lf), stk(p_ret), stk(p_conv), stk(p_ffn),
            stk(s_k), stk(s_v), stk(s_lf), stk(s_ret), stk(s_conv), stk(s_ffn))
```

```python
import functools

import numpy as np
import jax
import jax.numpy as jnp
from jax import lax
from jax.experimental import pallas as pl
from jax.experimental.pallas import tpu as pltpu

F32 = jnp.float32
BF16 = jnp.bfloat16

D_MODEL = 1024
RET_HEADS = 4
RET_WIDTH = 256
FOX_HEADS = 8
FOX_HEAD_DIM = 64
FOX_WIDTH = 512
CONV_DIM = 256
D_FF = 2816
PROJ_PAD = 3456
ROPE_BASE = 10000.0
NORM_EPS = 1e-6
LANES = 128
SUBLANES = 8
VMEM_LIMIT = 56 * 1024 * 1024

TOKEN_TILE = 512
RET_BLOCK = 256
FOX_TILE = 512
LOG2E = 1.4426950408889634
FOX_Q_SCALE = FOX_HEAD_DIM ** -0.5 * LOG2E
MXU_DEPTH = 256
FF_CHUNKS = ((0, 5 * MXU_DEPTH), (5 * MXU_DEPTH, D_FF))


def _dot(a, b):
    return jnp.dot(a, b, preferred_element_type=F32)


def _dot_nt(a, b):
    return lax.dot_general(a, b, (((1,), (1,)), ((), ())), preferred_element_type=F32)


def _split2(x):
    hi = x.astype(BF16)
    lo = (x - hi.astype(F32)).astype(BF16)
    return hi, lo


def _group_sum(x, ones_bd):
    hi, lo = _split2(x)
    return _dot(hi, ones_bd) + _dot(lo, ones_bd)


def _rms(x, g):
    ms = jnp.mean(x * x, axis=-1, keepdims=True)
    return x * lax.rsqrt(ms + NORM_EPS) * g


def _dwconv_carry(u, w_ref, carry_ref, first):
    tm = u.shape[0]

    @pl.when(first)
    def _():
        carry_ref[...] = jnp.zeros(carry_ref.shape, F32)

    row = lax.broadcasted_iota(jnp.int32, u.shape, 0)
    c6 = carry_ref[6:7, :]
    c7 = carry_ref[7:8, :]
    um1 = jnp.where(row == 0, c7, pltpu.roll(u, 1, axis=0))
    um2 = jnp.where(row == 0, c6, jnp.where(row == 1, c7, pltpu.roll(u, 2, axis=0)))
    y = w_ref[0:1, :] * um2 + w_ref[1:2, :] * um1 + w_ref[2:3, :] * u
    carry_ref[...] = u[tm - SUBLANES:tm, :]
    return y


def _dwconv_tmajor(u, w_ref, hist, nb):
    tm = u.shape[0]
    full = jnp.concatenate([hist, u], axis=0)
    y = (w_ref[0:1, :] * full[0:tm] + w_ref[1:2, :] * full[nb:nb + tm]
         + w_ref[2:3, :] * full[2 * nb:2 * nb + tm])
    return y, full[tm:tm + 2 * nb]


def _inproj_kernel(*refs, tmajor, nb):
    if tmajor:
        (x_ref, nw_ref, w_ref, cos_ref, sin_ref, gq_ref, gk_ref, fb_ref, cw_ref, bd_ref, hist_ref,
         ret_ref, fq_ref, fk32_ref, fv32_ref, fkb_ref, fvb_ref, lf_ref, oc_ref, st_ref) = refs
    else:
        (x_ref, nw_ref, w_ref, cos_ref, sin_ref, gq_ref, gk_ref, fb_ref, cw_ref, bd_ref,
         _, _, ret_ref, fq_ref, fk32_ref, fv32_ref, fkb_ref, fvb_ref, lf_ref, oc_ref, st_ref,
         carry_ref) = refs
    h = _rms(x_ref[...], nw_ref[...]).astype(BF16)

    a = _dot(h, w_ref[:, 0:1024])
    cos = cos_ref[...]
    sin = sin_ref[...]
    q1, q2, k1, k2 = a[:, 0:128], a[:, 128:256], a[:, 256:384], a[:, 384:512]
    ret_ref[:, 0:128] = (q1 * cos - q2 * sin).astype(BF16)
    ret_ref[:, 128:256] = (q1 * sin + q2 * cos).astype(BF16)
    ret_ref[:, 256:384] = ((k1 * cos - k2 * sin) * 0.125).astype(BF16)
    ret_ref[:, 384:512] = ((k1 * sin + k2 * cos) * 0.125).astype(BF16)
    ret_ref[:, 512:1024] = a[:, 512:1024].astype(BF16)

    f = _dot(h, w_ref[:, 1024:2560])
    fq, fk, fv = f[:, 0:512], f[:, 512:1024], f[:, 1024:1536]
    bd = bd_ref[...]
    inv_d = 1.0 / FOX_HEAD_DIM
    ssq = _group_sum(fq * fq, bd)
    fq_ref[...] = (fq * lax.rsqrt(ssq * inv_d + NORM_EPS) * gq_ref[...] * FOX_Q_SCALE).astype(BF16)
    ssk = _group_sum(fk * fk, bd)
    fkn = fk * lax.rsqrt(ssk * inv_d + NORM_EPS) * gk_ref[...]
    fkb_ref[...] = fkn.astype(BF16)
    fvb_ref[...] = fv.astype(BF16)
    if tmajor:
        fk32_ref[...] = fkn
        fv32_ref[...] = fv
    else:
        tm = fkn.shape[0]
        for hd in range(FOX_HEADS):
            cols = slice(FOX_HEAD_DIM * hd, FOX_HEAD_DIM * (hd + 1))
            fk32_ref[pl.ds(hd, tm, stride=FOX_HEADS), :] = fkn[:, cols]
            fv32_ref[pl.ds(hd, tm, stride=FOX_HEADS), :] = fv[:, cols]

    c = _dot(h, w_ref[:, 2560:3328])
    cb, cc, ch = c[:, 0:256], c[:, 256:512], c[:, 512:768]
    u = cc * ch
    if tmajor:
        y, new_hist = _dwconv_tmajor(u, cw_ref, hist_ref[...], nb)
        st_ref[...] = new_hist
    else:
        y = _dwconv_carry(u, cw_ref, carry_ref, pl.program_id(1) == 0)
        st_ref[...] = carry_ref[6:8, :]
    oc_ref[...] = (cb * y).astype(BF16)

    z = _dot(h, w_ref[:, 3328:3456]) + fb_ref[...]
    ls = jnp.minimum(z, 0.0) - jnp.log1p(jnp.exp(-jnp.abs(z)))
    lf_ref[...] = ls[:, 0:FOX_HEADS]


def _inproj(l, x, prm, cos, sin, bd, *, kbuf=None, vbuf=None, hist=None, nb=0):
    tmajor = hist is not None
    g, r, _ = x.shape
    tm = min(TOKEN_TILE, r)
    grid = (g, r // tm)
    depth = prm["w_in"].shape[0]
    lsel = lambda b, i: (l, 0, 0)
    in_specs = [
        pl.BlockSpec((None, tm, D_MODEL), lambda b, i: (b, i, 0)),
        pl.BlockSpec((None, 1, D_MODEL), lsel),
        pl.BlockSpec((None, D_MODEL, PROJ_PAD), lsel, pipeline_mode=pl.Buffered(1)),
        pl.BlockSpec((tm, LANES), lambda b, i: (i, 0)),
        pl.BlockSpec((tm, LANES), lambda b, i: (i, 0)),
        pl.BlockSpec((None, 1, FOX_WIDTH), lsel),
        pl.BlockSpec((None, 1, FOX_WIDTH), lsel),
        pl.BlockSpec((None, 1, LANES), lsel),
        pl.BlockSpec((None, 3, CONV_DIM), lsel),
        pl.BlockSpec((FOX_WIDTH, FOX_WIDTH), lambda b, i: (0, 0)),
    ]
    args = [x, prm["norm_mix"], prm["w_in"], cos, sin, prm["gq"], prm["gk"], prm["fb"], prm["conv_w"], bd]
    tok = lambda width: pl.BlockSpec((None, tm, width), lambda b, i: (b, i, 0))
    if tmajor:
        in_specs.append(pl.BlockSpec((2 * nb, CONV_DIM), lambda b, i: (0, 0)))
        args.append(hist)
        kv_shape = jax.ShapeDtypeStruct((g, r, FOX_WIDTH), F32)
        kv_spec = tok(FOX_WIDTH)
        st_shape = jax.ShapeDtypeStruct((2 * nb, CONV_DIM), F32)
        st_spec = pl.BlockSpec((2 * nb, CONV_DIM), lambda b, i: (0, 0))
        aliases = {}
        scratch = []
    else:
        in_specs += [pl.BlockSpec(memory_space=pl.ANY), pl.BlockSpec(memory_space=pl.ANY)]
        args += [kbuf, vbuf]
        kv_shape = jax.ShapeDtypeStruct((depth, g, r * FOX_HEADS, FOX_HEAD_DIM), F32)
        kv_spec = pl.BlockSpec((None, None, tm * FOX_HEADS, FOX_HEAD_DIM), lambda b, i: (l, b, i, 0))
        st_shape = jax.ShapeDtypeStruct((g, 2, CONV_DIM), F32)
        st_spec = pl.BlockSpec((None, 2, CONV_DIM), lambda b, i: (b, 0, 0))
        aliases = {10: 2, 11: 3}
        scratch = [pltpu.VMEM((SUBLANES, CONV_DIM), F32)]
    out_shape = [
        jax.ShapeDtypeStruct((g, r, 1024), BF16),
        jax.ShapeDtypeStruct((g, r, FOX_WIDTH), BF16),
        kv_shape, kv_shape,
        jax.ShapeDtypeStruct((g, r, FOX_WIDTH), BF16),
        jax.ShapeDtypeStruct((g, r, FOX_WIDTH), BF16),
        jax.ShapeDtypeStruct((g, r, FOX_HEADS), F32),
        jax.ShapeDtypeStruct((g, r, CONV_DIM), BF16),
        st_shape,
    ]
    out_specs = [tok(1024), tok(FOX_WIDTH), kv_spec, kv_spec, tok(FOX_WIDTH), tok(FOX_WIDTH),
                 tok(FOX_HEADS), tok(CONV_DIM), st_spec]
    return pl.pallas_call(
        functools.partial(_inproj_kernel, tmajor=tmajor, nb=nb),
        grid=grid, in_specs=in_specs, out_specs=out_specs, out_shape=out_shape,
        scratch_shapes=scratch, input_output_aliases=aliases,
        compiler_params=pltpu.CompilerParams(dimension_semantics=("arbitrary", "arbitrary"),
                                             vmem_limit_bytes=VMEM_LIMIT),
        name="inproj",
    )(*args)


def _ret_kernel(x_ref, s0_ref, gain_ref, dec_ref, qs_ref, ks_ref, gbm_ref, bm_ref, bdv_ref,
                o_ref, sout_ref, s_ref, *, blk):
    @pl.when(pl.program_id(1) == 0)
    def _():
        s_ref[...] = s0_ref[...]

    inv_d = 1.0 / 64.0
    bdv = bdv_ref[...]
    lane = lax.broadcasted_iota(jnp.int32, (1, RET_WIDTH), 1)
    head_k = (lane % 128) // 32
    head_v = lane // 64
    for sb in range(x_ref.shape[0] // blk):
        rows = slice(sb * blk, (sb + 1) * blk)
        q = x_ref[rows, 0:256]
        k = x_ref[rows, 256:512]
        v = x_ref[rows, 512:768]
        gate = x_ref[rows, 768:1024].astype(F32)
        state = s_ref[...]
        qw = (q.astype(F32) * qs_ref[...]).astype(BF16)
        o = _dot(qw, state.astype(BF16))
        for hd in range(RET_HEADS):
            qh = jnp.where(head_k == hd, q, jnp.zeros_like(q))
            a = _dot_nt(qh, k) * dec_ref[hd]
            o = o + jnp.where(head_v == hd, _dot(a.astype(BF16), v), 0.0)
        kw_t = (k.astype(F32) * ks_ref[...]).T.astype(BF16)
        s_ref[...] = gbm_ref[...] * state + bm_ref[...] * _dot(kw_t, v)
        mu = _group_sum(o, bdv) * inv_d
        d = o - mu
        var = _group_sum(d * d, bdv) * inv_d
        on = d * lax.rsqrt(var + NORM_EPS) * gain_ref[...]
        o_ref[rows, :] = (on * (gate / (1.0 + jnp.exp(-gate)))).astype(BF16)
    sout_ref[...] = s_ref[...]


def _ret_constants(blk, chunk_len):
    log_g = jnp.log1p(-jnp.exp2(-5.0 - jnp.arange(RET_HEADS, dtype=F32)))
    lane = np.arange(RET_WIDTH)
    head_k = (lane % 128) // 32
    head_v = lane // 64
    i = jnp.arange(blk, dtype=F32)
    diff = i[:, None] - i[None, :]
    dec = jnp.where(diff[None] >= 0.0, jnp.exp(jnp.maximum(diff, 0.0)[None] * log_g[:, None, None]), 0.0)
    lg_k = log_g[head_k]
    qs = jnp.exp((i + 1.0)[:, None] * lg_k[None, :])
    ks = jnp.exp((chunk_len - 1.0 - i)[:, None] * lg_k[None, :])
    gbm = jnp.broadcast_to(jnp.exp(chunk_len * lg_k)[:, None], (RET_WIDTH, RET_WIDTH))
    bm = jnp.asarray(head_k[:, None] == head_v[None, :], F32)
    bdv = jnp.asarray(head_v[:, None] == head_v[None, :], BF16)
    return dec, qs, ks, gbm, bm, bdv


def _retention(l, ret_in, s0, gain, consts, blk, rt):
    g, r, _ = ret_in.shape
    grid = (g, r // rt)
    const2 = lambda shape: pl.BlockSpec(shape, lambda b, i: (0,) * len(shape))
    dec, qs, ks, gbm, bm, bdv = consts
    in_specs = [
        pl.BlockSpec((None, rt, 1024), lambda b, i: (b, i, 0)),
        pl.BlockSpec((None, RET_WIDTH, RET_WIDTH), lambda b, i: (b, 0, 0)),
        pl.BlockSpec((None, 1, RET_WIDTH), lambda b, i: (l, 0, 0)),
        const2(dec.shape), const2(qs.shape), const2(ks.shape), const2(gbm.shape), const2(bm.shape),
        const2(bdv.shape),
    ]
    return pl.pallas_call(
        functools.partial(_ret_kernel, blk=blk),
        grid=grid, in_specs=in_specs,
        out_specs=[pl.BlockSpec((None, rt, RET_WIDTH), lambda b, i: (b, i, 0)),
                   pl.BlockSpec((None, RET_WIDTH, RET_WIDTH), lambda b, i: (b, 0, 0))],
        out_shape=[jax.ShapeDtypeStruct((g, r, RET_WIDTH), BF16),
                   jax.ShapeDtypeStruct((g, RET_WIDTH, RET_WIDTH), F32)],
        scratch_shapes=[pltpu.VMEM((RET_WIDTH, RET_WIDTH), F32)],
        compiler_params=pltpu.CompilerParams(dimension_semantics=("arbitrary", "arbitrary"),
                                             vmem_limit_bytes=VMEM_LIMIT),
        name="retention",
    )(ret_in, s0, gain, dec, qs, ks, gbm, bm, bdv)


def _state_to_bd(st):
    g = st.shape[0]
    t = jnp.einsum("bhpie,hg->bphige", st.reshape(g, RET_HEADS, 2, 32, 64), jnp.eye(RET_HEADS, dtype=st.dtype))
    return t.reshape(g, RET_WIDTH, RET_WIDTH)


def _bd_to_state(sbd):
    g = sbd.shape[0]
    t = jnp.einsum("bphihe->bhpie", sbd.reshape(g, 2, RET_HEADS, 32, RET_HEADS, 64))
    return t.reshape(g, RET_HEADS, 64, 64)


def _cumsum_kernel(x_ref, u_ref, o_ref):
    nc = x_ref.shape[1] // LANES
    xs = jnp.concatenate([x_ref[:, LANES * j:LANES * (j + 1)] for j in range(nc)], axis=0)
    x1 = xs.astype(BF16)
    r1 = xs - x1.astype(F32)
    x2 = r1.astype(BF16)
    x3 = (r1 - x2.astype(F32)).astype(BF16)
    tri = u_ref[...]
    loc = _dot(x1, tri) + _dot(x2, tri) + _dot(x3, tri)
    off = jnp.zeros((SUBLANES, 1), F32)
    for j in range(nc):
        lj = loc[SUBLANES * j:SUBLANES * (j + 1), :]
        o_ref[:, LANES * j:LANES * (j + 1)] = (lj + off) * (-LOG2E)
        off = off + lj[:, LANES - 1:LANES]


def _cumsum_lanes(x):
    n, h, w = x.shape
    tri = jnp.asarray(np.arange(LANES)[:, None] <= np.arange(LANES)[None, :], BF16)
    return pl.pallas_call(
        _cumsum_kernel,
        grid=(n,),
        in_specs=[pl.BlockSpec((None, h, w), lambda b: (b, 0, 0)),
                  pl.BlockSpec((LANES, LANES), lambda b: (0, 0))],
        out_specs=pl.BlockSpec((None, h, w), lambda b: (b, 0, 0)),
        out_shape=jax.ShapeDtypeStruct((n, h, w), F32),
        compiler_params=pltpu.CompilerParams(dimension_semantics=("arbitrary",)),
        name="forget_cumsum",
    )(x, tri)


def _fox_prompt_kernel(q_ref, k_ref, v_ref, nc_ref, o_ref, va_ref, qh_ref, sa_ref, sb_ref, m_ref, acc_ref, *,
                       tile):
    s_len = q_ref.shape[0]
    lane = lax.broadcasted_iota(jnp.int32, (1, LANES), 1)
    row = lax.broadcasted_iota(jnp.int32, (tile, tile), 0)
    col = lax.broadcasted_iota(jnp.int32, (tile, tile), 1)
    v_all = v_ref[...]
    for hd in range(2):
        va_ref[hd] = jnp.where(lane // FOX_HEAD_DIM == hd, v_all, jnp.ones_like(v_all))

    def scores(kj, causal):
        kt = k_ref[pl.ds(pl.multiple_of(kj * tile, tile), tile), :]
        out = []
        for hd in range(2):
            s = _dot_nt(qh_ref[hd], kt) + nc_ref[kj, hd:hd + 1, :]
            out.append(jnp.where(col <= row, s, -jnp.inf) if causal else s)
        return out

    def park(dst_ref, kj, causal):
        sn = scores(kj, causal)
        for hd in range(2):
            dst_ref[hd] = sn[hd]

    def consume(src_ref, kj):
        k0 = pl.multiple_of(kj * tile, tile)
        for hd in range(2):
            s = src_ref[hd]
            m_prev = m_ref[hd]
            m_new = jnp.maximum(m_prev, jnp.max(s, axis=-1, keepdims=True))
            alpha = jnp.exp2(m_prev - m_new)
            p = jnp.exp2(s - m_new[:, 0:1]).astype(BF16)
            acc_ref[hd] = alpha * acc_ref[hd] + _dot(p, va_ref[hd, pl.ds(k0, tile), :])
            m_ref[hd] = m_new

    def q_body(qi, carry):
        q0 = pl.multiple_of(qi * tile, tile)
        q = q_ref[pl.ds(q0, tile), :]
        for hd in range(2):
            qh_ref[hd] = jnp.where(lane // FOX_HEAD_DIM == hd, q, jnp.zeros_like(q))
        m_ref[...] = jnp.full(m_ref.shape, -jnp.inf, F32)
        acc_ref[...] = jnp.zeros(acc_ref.shape, F32)
        park(sa_ref, qi, True)

        def kv_pair(jp, c2):
            j = 2 * jp
            park(sb_ref, j, False)
            consume(sa_ref, jnp.where(jp == 0, qi, j - 1))
            park(sa_ref, j + 1, False)
            consume(sb_ref, j)
            return c2

        lax.fori_loop(0, qi // 2, kv_pair, 0)

        @pl.when(qi % 2 == 1)
        def _():
            park(sb_ref, qi - 1, False)
            consume(sa_ref, jnp.where(qi == 1, qi, qi - 2))
            consume(sb_ref, qi - 1)

        @pl.when(qi % 2 == 0)
        def _():
            consume(sa_ref, jnp.maximum(qi - 1, 0))
        outs = [acc_ref[hd] / pltpu.roll(acc_ref[hd], FOX_HEAD_DIM, axis=1) for hd in range(2)]
        o_ref[pl.ds(q0, tile), :] = jnp.where(lane < FOX_HEAD_DIM, outs[0], outs[1]).astype(BF16)
        return carry

    lax.fori_loop(0, s_len // tile, q_body, 0)


def _fox_prompt(q, k, v, c_tiles, tile):
    b, s, _ = q.shape
    nt = s // tile
    qkv_spec = pl.BlockSpec((None, s, LANES), lambda bi, hp: (bi, 0, hp))
    return pl.pallas_call(
        functools.partial(_fox_prompt_kernel, tile=tile),
        grid=(b, FOX_HEADS // 2),
        in_specs=[qkv_spec, qkv_spec, qkv_spec,
                  pl.BlockSpec((None, None, nt, 2, tile), lambda bi, hp: (bi, hp, 0, 0, 0))],
        out_specs=qkv_spec,
        out_shape=jax.ShapeDtypeStruct((b, s, FOX_WIDTH), BF16),
        scratch_shapes=[pltpu.VMEM((2, s, LANES), BF16), pltpu.VMEM((2, tile, LANES), BF16),
                        pltpu.VMEM((2, tile, tile), F32), pltpu.VMEM((2, tile, tile), F32),
                        pltpu.VMEM((2, tile, LANES), F32), pltpu.VMEM((2, tile, LANES), F32)],
        compiler_params=pltpu.CompilerParams(dimension_semantics=("arbitrary", "arbitrary"),
                                             vmem_limit_bytes=VMEM_LIMIT),
        name="fox_prompt",
    )(q, k, v, c_tiles)


def _fox_sample_kernel(q_ref, kc_ref, vc_ref, kn_ref, vn_ref, c_ref, o_ref, *, past, n):
    lane = lax.broadcasted_iota(jnp.int32, (1, LANES), 1)
    row = lax.broadcasted_iota(jnp.int32, (n, n), 0)
    col = lax.broadcasted_iota(jnp.int32, (n, n), 1)
    q = q_ref[...]
    kc = kc_ref[...].astype(BF16)
    vc = vc_ref[...].astype(BF16)
    kn = kn_ref[...]
    vn = vn_ref[...]
    outs = []
    for hd in range(2):
        qh = jnp.where(lane // FOX_HEAD_DIM == hd, q, jnp.zeros_like(q))
        ctot = c_ref[hd:hd + 1, past - 1:past]
        sc = _dot_nt(qh, kc) + (c_ref[hd:hd + 1, 0:past] - ctot)
        sn = _dot_nt(qh, kn) + (c_ref[hd:hd + 1, past:past + n] - ctot)
        sn = jnp.where(col <= row, sn, -jnp.inf)
        m = jnp.maximum(jnp.max(sc, axis=-1, keepdims=True), jnp.max(sn, axis=-1, keepdims=True))
        pc = jnp.exp2(sc - m)
        pn = jnp.exp2(sn - m)
        denom = jnp.sum(pc, axis=-1, keepdims=True) + jnp.sum(pn, axis=-1, keepdims=True)
        outs.append((_dot(pc.astype(BF16), vc) + _dot(pn.astype(BF16), vn)) / denom)
    o_ref[...] = jnp.where(lane < FOX_HEAD_DIM, outs[0], outs[1]).astype(BF16)


def _fox_sample(l, q, kn, vn, cache_k, cache_v, c_all):
    nb, n, _ = q.shape
    past = cache_k.shape[2]
    new_spec = pl.BlockSpec((None, n, LANES), lambda b, hp: (b, 0, hp))
    cache_spec = pl.BlockSpec((None, None, past, LANES), lambda b, hp: (l, b, 0, hp))
    return pl.pallas_call(
        functools.partial(_fox_sample_kernel, past=past, n=n),
        grid=(nb, FOX_HEADS // 2),
        in_specs=[new_spec, cache_spec, cache_spec, new_spec, new_spec,
                  pl.BlockSpec((None, None, 2, c_all.shape[-1]), lambda b, hp: (b, hp, 0, 0))],
        out_specs=new_spec,
        out_shape=jax.ShapeDtypeStruct((nb, n, FOX_WIDTH), BF16),
        compiler_params=pltpu.CompilerParams(dimension_semantics=("arbitrary", "arbitrary"),
                                             vmem_limit_bytes=VMEM_LIMIT),
        name="fox_sample",
    )(q, cache_k, cache_v, kn, vn, c_all)


def _mlp_kernel(*refs, tmajor, nb):
    if tmajor:
        (x_ref, oret_ref, ofox_ref, oconv_ref, wout_ref, nw_ref, wg_ref, wu_ref, wd_ref, fcw_ref, hist_ref,
         y_ref, st_ref) = refs
    else:
        (x_ref, oret_ref, ofox_ref, oconv_ref, wout_ref, nw_ref, wg_ref, wu_ref, wd_ref, fcw_ref,
         y_ref, st_ref, carry_ref) = refs
    x1 = (x_ref[...] + _dot(oret_ref[...], wout_ref[0:256, :]) + _dot(ofox_ref[...], wout_ref[256:768, :])
          + _dot(oconv_ref[...], wout_ref[768:1024, :]))
    h2 = _rms(x1, nw_ref[...]).astype(BF16)
    down = None
    for c0, c1 in FF_CHUNKS:
        cols = slice(c0, c1)
        gate_pre = _dot(h2, wg_ref[:, cols])
        if tmajor:
            gate_c, new_hist = _dwconv_tmajor(gate_pre, fcw_ref.at[:, cols], hist_ref[:, cols], nb)
            st_ref[:, cols] = new_hist
        else:
            gate_c = _dwconv_carry(gate_pre, fcw_ref.at[:, cols], carry_ref.at[:, cols], pl.program_id(1) == 0)
            st_ref[:, cols] = carry_ref[6:8, cols]
        up = _dot(h2, wu_ref[:, cols])
        act = (gate_c / (1.0 + jnp.exp(-gate_c)) * up).astype(BF16)
        part = _dot(act, wd_ref[cols, :])
        down = part if down is None else down + part
    y_ref[...] = x1 + down


def _mlp(l, x, oret, ofox, oconv, prm, *, hist=None, nb=0):
    tmajor = hist is not None
    g, r, _ = x.shape
    tm = min(TOKEN_TILE, r)
    grid = (g, r // tm)
    lsel = lambda b, i: (l, 0, 0)
    tok = lambda width: pl.BlockSpec((None, tm, width), lambda b, i: (b, i, 0))
    wspec = lambda rows, cols: pl.BlockSpec((None, rows, cols), lsel, pipeline_mode=pl.Buffered(1))
    in_specs = [tok(D_MODEL), tok(RET_WIDTH), tok(FOX_WIDTH), tok(CONV_DIM),
                wspec(D_MODEL, D_MODEL), pl.BlockSpec((None, 1, D_MODEL), lsel),
                wspec(D_MODEL, D_FF), wspec(D_MODEL, D_FF), wspec(D_FF, D_MODEL),
                pl.BlockSpec((None, 3, D_FF), lsel)]
    args = [x, oret, ofox, oconv, prm["w_out"], prm["norm_ffn"], prm["w_gate"], prm["w_up"], prm["w_down"],
            prm["ffn_conv_w"]]
    if tmajor:
        in_specs.append(pl.BlockSpec((2 * nb, D_FF), lambda b, i: (0, 0)))
        args.append(hist)
        st_shape = jax.ShapeDtypeStruct((2 * nb, D_FF), F32)
        st_spec = pl.BlockSpec((2 * nb, D_FF), lambda b, i: (0, 0))
        scratch = []
    else:
        st_shape = jax.ShapeDtypeStruct((g, 2, D_FF), F32)
        st_spec = pl.BlockSpec((None, 2, D_FF), lambda b, i: (b, 0, 0))
        scratch = [pltpu.VMEM((SUBLANES, D_FF), F32)]
    return pl.pallas_call(
        functools.partial(_mlp_kernel, tmajor=tmajor, nb=nb),
        grid=grid, in_specs=in_specs,
        out_specs=[tok(D_MODEL), st_spec],
        out_shape=[jax.ShapeDtypeStruct((g, r, D_MODEL), F32), st_shape],
        scratch_shapes=scratch,
        compiler_params=pltpu.CompilerParams(dimension_semantics=("arbitrary", "arbitrary"),
                                             vmem_limit_bytes=VMEM_LIMIT),
        name="outproj_mlp",
    )(*args)


def _win_permutation():
    idx = []
    for base in (0, 256):
        for half in range(2):
            for hd in range(RET_HEADS):
                idx += [base + 64 * hd + 32 * half + i for i in range(32)]
    idx += list(range(512, 1024))
    idx += list(range(1024, 2560))
    idx += list(range(2568, 3336))
    idx += list(range(2560, 2568))
    return np.asarray(idx, np.int32)


def _rope_tables(pos):
    half = 32
    inv_freq = ROPE_BASE ** (-jnp.arange(half, dtype=F32) / half)
    ang = pos.astype(F32)[:, None] * inv_freq[None, :]
    return jnp.tile(jnp.cos(ang), (1, RET_HEADS)), jnp.tile(jnp.sin(ang), (1, RET_HEADS))


def _to_bmajor(a, n, nb):
    return a.reshape(n, nb, a.shape[-1]).transpose(1, 0, 2)


def _to_tmajor(a):
    nb, n, c = a.shape
    return a.transpose(1, 0, 2).reshape(1, n * nb, c)


def kernel(x_prompt, x_sample, cache_fox_k, cache_fox_v, cache_fox_logf, state_ret, state_conv, state_ffn_conv,
           norm_mix, w_in, ret_gn_gain, fox_q_gain, fox_k_gain, fox_f_bias, conv_w, w_out, norm_ffn, w_gate,
           w_up, ffn_conv_w, w_down):
    depth = w_in.shape[0]
    b, s, _ = x_prompt.shape
    nb, n, _ = x_sample.shape
    past = cache_fox_k.shape[2]

    w_perm = jnp.take(w_in, _win_permutation(), axis=2)
    w_perm = jnp.pad(w_perm, ((0, 0), (0, 0), (0, PROJ_PAD - w_perm.shape[2]))).astype(BF16)
    prm = {
        "w_in": w_perm,
        "norm_mix": norm_mix.reshape(depth, 1, D_MODEL),
        "gq": fox_q_gain.reshape(depth, 1, FOX_WIDTH),
        "gk": fox_k_gain.reshape(depth, 1, FOX_WIDTH),
        "fb": jnp.pad(fox_f_bias, ((0, 0), (0, LANES - FOX_HEADS))).reshape(depth, 1, LANES),
        "conv_w": conv_w,
        "w_out": w_out.astype(BF16),
        "norm_ffn": norm_ffn.reshape(depth, 1, D_MODEL),
        "w_gate": w_gate.astype(BF16),
        "w_up": w_up.astype(BF16),
        "w_down": w_down.astype(BF16),
        "ffn_conv_w": ffn_conv_w,
    }
    gain = ret_gn_gain.reshape(depth, 1, RET_WIDTH)
    lane = np.arange(FOX_WIDTH)
    bd = jnp.asarray(lane[:, None] // FOX_HEAD_DIM == lane[None, :] // FOX_HEAD_DIM, BF16)

    cos_p, sin_p = _rope_tables(jnp.arange(s, dtype=jnp.int32))
    pos_s = past + jnp.arange(n, dtype=jnp.int32)
    cos_s, sin_s = _rope_tables(jnp.repeat(pos_s, nb))

    ret_blk = min(RET_BLOCK, s)
    ret_rt = min(TOKEN_TILE, s)
    consts_p = _ret_constants(ret_blk, float(ret_blk))
    samp_rows = LANES
    consts_s = _ret_constants(samp_rows, float(n))
    fox_tile = min(FOX_TILE, s)

    cache_k = cache_fox_k.reshape(depth, nb, past, FOX_WIDTH)
    cache_v = cache_fox_v.reshape(depth, nb, past, FOX_WIDTH)
    cache_lf_t = jnp.transpose(cache_fox_logf, (0, 1, 3, 2))
    c_width = -(-(past + n) // LANES) * LANES

    kbuf = jnp.zeros((depth, b, s * FOX_HEADS, FOX_HEAD_DIM), F32)
    vbuf = jnp.zeros((depth, b, s * FOX_HEADS, FOX_HEAD_DIM), F32)
    zero_state = jnp.zeros((b, RET_WIDTH, RET_WIDTH), F32)

    xp = x_prompt
    xs = jnp.transpose(x_sample, (1, 0, 2)).reshape(1, n * nb, D_MODEL)
    p_lf, p_ret, p_conv, p_ffn = [], [], [], []
    s_k, s_v, s_lf, s_ret, s_conv, s_ffn = [], [], [], [], [], []
    for l in range(depth):
        ret_in, fq, kbuf, vbuf, fkb, fvb, lf, oconv, conv_st = _inproj(
            l, xp, prm, cos_p, sin_p, bd, kbuf=kbuf, vbuf=vbuf)
        oret, sbd = _retention(l, ret_in, zero_state, gain, consts_p, ret_blk, ret_rt)
        c_rows = _cumsum_lanes(jnp.transpose(lf, (0, 2, 1)))
        c_tiles = c_rows.reshape(b, FOX_HEADS // 2, 2, s // fox_tile, fox_tile).transpose(0, 1, 3, 2, 4)
        ofox = _fox_prompt(fq, fkb, fvb, c_tiles, fox_tile)
        xp, ffn_st = _mlp(l, xp, oret, ofox, oconv, prm)
        p_lf.append(lf)
        p_ret.append(_bd_to_state(sbd))
        p_conv.append(conv_st)
        p_ffn.append(ffn_st)

        hist_c = jnp.transpose(state_conv[l], (1, 0, 2)).reshape(2 * nb, CONV_DIM)
        ret_in, fq, fk32, fv32, fkb, fvb, lf, oconv, conv_st = _inproj(
            l, xs, prm, cos_s, sin_s, bd, hist=hist_c, nb=nb)
        ret_b = jnp.pad(_to_bmajor(ret_in, n, nb), ((0, 0), (0, samp_rows - n), (0, 0)))
        oret, sbd = _retention(l, ret_b, _state_to_bd(state_ret[l]), gain, consts_s, samp_rows, samp_rows)
        lf_b = _to_bmajor(lf, n, nb)
        lf_all = jnp.concatenate(
            [cache_lf_t[l], jnp.transpose(lf_b, (0, 2, 1)),
             jnp.zeros((nb, FOX_HEADS, c_width - past - n), F32)], axis=2)
        c_all = _cumsum_lanes(lf_all).reshape(nb, FOX_HEADS // 2, 2, c_width)
        ofox = _fox_sample(l, _to_bmajor(fq, n, nb), _to_bmajor(fkb, n, nb), _to_bmajor(fvb, n, nb),
                           cache_k, cache_v, c_all)
        hist_f = jnp.transpose(state_ffn_conv[l], (1, 0, 2)).reshape(2 * nb, D_FF)
        xs, ffn_st = _mlp(l, xs, _to_tmajor(oret[:, :n]), _to_tmajor(ofox), oconv, prm, hist=hist_f, nb=nb)
        s_k.append(_to_bmajor(fk32, n, nb).reshape(nb, n, FOX_HEADS, FOX_HEAD_DIM))
        s_v.append(_to_bmajor(fv32, n, nb).reshape(nb, n, FOX_HEADS, FOX_HEAD_DIM))
        s_lf.append(lf_b)
        s_ret.append(_bd_to_state(sbd))
        s_conv.append(conv_st.reshape(2, nb, CONV_DIM).transpose(1, 0, 2))
        s_ffn.append(ffn_st.reshape(2, nb, D_FF).transpose(1, 0, 2))

    y_sample = xs.reshape(n, nb, D_MODEL).transpose(1, 0, 2)
    stk = lambda ts: jnp.stack(ts, axis=0)
    return (xp, y_sample,
            kbuf.reshape(depth, b, s, FOX_HEADS, FOX_HEAD_DIM), vbuf.reshape(depth, b, s, FOX_HEADS, FOX_HEAD_DIM),
            stk(p_lf), stk(p_ret), stk(p_conv), stk(p_ffn),
            stk(s_k), stk(s_v), stk(s_lf), stk(s_ret), stk(s_conv), stk(s_ffn))
```

```python
import functools

import numpy as np
import jax
import jax.numpy as jnp
from jax import lax
from jax.experimental import pallas as pl
from jax.experimental.pallas import tpu as pltpu

F32 = jnp.float32
BF16 = jnp.bfloat16

D_MODEL = 1024
RET_HEADS = 4
RET_WIDTH = 256
FOX_HEADS = 8
FOX_HEAD_DIM = 64
FOX_WIDTH = 512
CONV_DIM = 256
D_FF = 2816
PROJ_PAD = 3456
ROPE_BASE = 10000.0
NORM_EPS = 1e-6
LANES = 128
SUBLANES = 8
VMEM_LIMIT = 56 * 1024 * 1024

TOKEN_TILE = 512
RET_BLOCK = 256
FOX_TILE = 512
FOX_SAMPLE_CHUNK = 1024
LOG2E = 1.4426950408889634
FOX_Q_SCALE = FOX_HEAD_DIM ** -0.5 * LOG2E
MXU_DEPTH = 256
FF_CHUNKS = ((0, 5 * MXU_DEPTH), (5 * MXU_DEPTH, D_FF))


def _dot(a, b):
    return jnp.dot(a, b, preferred_element_type=F32)


def _dot_nt(a, b):
    return lax.dot_general(a, b, (((1,), (1,)), ((), ())), preferred_element_type=F32)


def _split2(x):
    hi = x.astype(BF16)
    lo = (x - hi.astype(F32)).astype(BF16)
    return hi, lo


def _group_sum(x, ones_bd):
    hi, lo = _split2(x)
    return _dot(hi, ones_bd) + _dot(lo, ones_bd)


def _rms(x, g):
    ms = jnp.mean(x * x, axis=-1, keepdims=True)
    return x * lax.rsqrt(ms + NORM_EPS) * g


def _dwconv_carry(u, w_ref, carry_ref, first):
    tm = u.shape[0]

    @pl.when(first)
    def _():
        carry_ref[...] = jnp.zeros(carry_ref.shape, F32)

    row = lax.broadcasted_iota(jnp.int32, u.shape, 0)
    c6 = carry_ref[6:7, :]
    c7 = carry_ref[7:8, :]
    um1 = jnp.where(row == 0, c7, pltpu.roll(u, 1, axis=0))
    um2 = jnp.where(row == 0, c6, jnp.where(row == 1, c7, pltpu.roll(u, 2, axis=0)))
    y = w_ref[0:1, :] * um2 + w_ref[1:2, :] * um1 + w_ref[2:3, :] * u
    carry_ref[...] = u[tm - SUBLANES:tm, :]
    return y


def _dwconv_tmajor(u, w_ref, hist, nb):
    tm = u.shape[0]
    full = jnp.concatenate([hist, u], axis=0)
    y = (w_ref[0:1, :] * full[0:tm] + w_ref[1:2, :] * full[nb:nb + tm]
         + w_ref[2:3, :] * full[2 * nb:2 * nb + tm])
    return y, full[tm:tm + 2 * nb]


def _inproj_kernel(*refs, tmajor, nb):
    if tmajor:
        (x_ref, nw_ref, w_ref, cos_ref, sin_ref, gq_ref, gk_ref, fb_ref, cw_ref, bd_ref, hist_ref,
         ret_ref, fq_ref, fk32_ref, fv32_ref, fkb_ref, fvb_ref, lf_ref, oc_ref, st_ref) = refs
    else:
        (x_ref, nw_ref, w_ref, cos_ref, sin_ref, gq_ref, gk_ref, fb_ref, cw_ref, bd_ref,
         _, _, ret_ref, fq_ref, fk32_ref, fv32_ref, fkb_ref, fvb_ref, lf_ref, oc_ref, st_ref,
         carry_ref) = refs
    h = _rms(x_ref[...], nw_ref[...]).astype(BF16)

    a = _dot(h, w_ref[:, 0:1024])
    cos = cos_ref[...]
    sin = sin_ref[...]
    q1, q2, k1, k2 = a[:, 0:128], a[:, 128:256], a[:, 256:384], a[:, 384:512]
    ret_ref[:, 0:128] = (q1 * cos - q2 * sin).astype(BF16)
    ret_ref[:, 128:256] = (q1 * sin + q2 * cos).astype(BF16)
    ret_ref[:, 256:384] = ((k1 * cos - k2 * sin) * 0.125).astype(BF16)
    ret_ref[:, 384:512] = ((k1 * sin + k2 * cos) * 0.125).astype(BF16)
    ret_ref[:, 512:1024] = a[:, 512:1024].astype(BF16)

    f = _dot(h, w_ref[:, 1024:2560])
    fq, fk, fv = f[:, 0:512], f[:, 512:1024], f[:, 1024:1536]
    bd = bd_ref[...]
    inv_d = 1.0 / FOX_HEAD_DIM
    ssq = _group_sum(fq * fq, bd)
    fq_ref[...] = (fq * lax.rsqrt(ssq * inv_d + NORM_EPS) * gq_ref[...] * FOX_Q_SCALE).astype(BF16)
    ssk = _group_sum(fk * fk, bd)
    fkn = fk * lax.rsqrt(ssk * inv_d + NORM_EPS) * gk_ref[...]
    fkb_ref[...] = fkn.astype(BF16)
    fvb_ref[...] = fv.astype(BF16)
    if tmajor:
        fk32_ref[...] = fkn
        fv32_ref[...] = fv
    else:
        tm = fkn.shape[0]
        for hd in range(FOX_HEADS):
            cols = slice(FOX_HEAD_DIM * hd, FOX_HEAD_DIM * (hd + 1))
            fk32_ref[pl.ds(hd, tm, stride=FOX_HEADS), :] = fkn[:, cols]
            fv32_ref[pl.ds(hd, tm, stride=FOX_HEADS), :] = fv[:, cols]

    c = _dot(h, w_ref[:, 2560:3328])
    cb, cc, ch = c[:, 0:256], c[:, 256:512], c[:, 512:768]
    u = cc * ch
    if tmajor:
        y, new_hist = _dwconv_tmajor(u, cw_ref, hist_ref[...], nb)
        st_ref[...] = new_hist
    else:
        y = _dwconv_carry(u, cw_ref, carry_ref, pl.program_id(1) == 0)
        st_ref[...] = carry_ref[6:8, :]
    oc_ref[...] = (cb * y).astype(BF16)

    z = _dot(h, w_ref[:, 3328:3456]) + fb_ref[...]
    ls = jnp.minimum(z, 0.0) - jnp.log1p(jnp.exp(-jnp.abs(z)))
    lf_ref[...] = ls[:, 0:FOX_HEADS]


def _inproj(l, x, prm, cos, sin, bd, *, kbuf=None, vbuf=None, hist=None, nb=0):
    tmajor = hist is not None
    g, r, _ = x.shape
    tm = min(TOKEN_TILE, r)
    grid = (g, r // tm)
    depth = prm["w_in"].shape[0]
    lsel = lambda b, i: (l, 0, 0)
    in_specs = [
        pl.BlockSpec((None, tm, D_MODEL), lambda b, i: (b, i, 0)),
        pl.BlockSpec((None, 1, D_MODEL), lsel),
        pl.BlockSpec((None, D_MODEL, PROJ_PAD), lsel, pipeline_mode=pl.Buffered(1)),
        pl.BlockSpec((tm, LANES), lambda b, i: (i, 0)),
        pl.BlockSpec((tm, LANES), lambda b, i: (i, 0)),
        pl.BlockSpec((None, 1, FOX_WIDTH), lsel),
        pl.BlockSpec((None, 1, FOX_WIDTH), lsel),
        pl.BlockSpec((None, 1, LANES), lsel),
        pl.BlockSpec((None, 3, CONV_DIM), lsel),
        pl.BlockSpec((FOX_WIDTH, FOX_WIDTH), lambda b, i: (0, 0)),
    ]
    args = [x, prm["norm_mix"], prm["w_in"], cos, sin, prm["gq"], prm["gk"], prm["fb"], prm["conv_w"], bd]
    tok = lambda width: pl.BlockSpec((None, tm, width), lambda b, i: (b, i, 0))
    if tmajor:
        in_specs.append(pl.BlockSpec((2 * nb, CONV_DIM), lambda b, i: (0, 0)))
        args.append(hist)
        kv_shape = jax.ShapeDtypeStruct((g, r, FOX_WIDTH), F32)
        kv_spec = tok(FOX_WIDTH)
        st_shape = jax.ShapeDtypeStruct((2 * nb, CONV_DIM), F32)
        st_spec = pl.BlockSpec((2 * nb, CONV_DIM), lambda b, i: (0, 0))
        aliases = {}
        scratch = []
    else:
        in_specs += [pl.BlockSpec(memory_space=pl.ANY), pl.BlockSpec(memory_space=pl.ANY)]
        args += [kbuf, vbuf]
        kv_shape = jax.ShapeDtypeStruct((depth, g, r * FOX_HEADS, FOX_HEAD_DIM), F32)
        kv_spec = pl.BlockSpec((None, None, tm * FOX_HEADS, FOX_HEAD_DIM), lambda b, i: (l, b, i, 0))
        st_shape = jax.ShapeDtypeStruct((g, 2, CONV_DIM), F32)
        st_spec = pl.BlockSpec((None, 2, CONV_DIM), lambda b, i: (b, 0, 0))
        aliases = {10: 2, 11: 3}
        scratch = [pltpu.VMEM((SUBLANES, CONV_DIM), F32)]
    out_shape = [
        jax.ShapeDtypeStruct((g, r, 1024), BF16),
        jax.ShapeDtypeStruct((g, r, FOX_WIDTH), BF16),
        kv_shape, kv_shape,
        jax.ShapeDtypeStruct((g, r, FOX_WIDTH), BF16),
        jax.ShapeDtypeStruct((g, r, FOX_WIDTH), BF16),
        jax.ShapeDtypeStruct((g, r, FOX_HEADS), F32),
        jax.ShapeDtypeStruct((g, r, CONV_DIM), BF16),
        st_shape,
    ]
    out_specs = [tok(1024), tok(FOX_WIDTH), kv_spec, kv_spec, tok(FOX_WIDTH), tok(FOX_WIDTH),
                 tok(FOX_HEADS), tok(CONV_DIM), st_spec]
    return pl.pallas_call(
        functools.partial(_inproj_kernel, tmajor=tmajor, nb=nb),
        grid=grid, in_specs=in_specs, out_specs=out_specs, out_shape=out_shape,
        scratch_shapes=scratch, input_output_aliases=aliases,
        compiler_params=pltpu.CompilerParams(dimension_semantics=("arbitrary", "arbitrary"),
                                             vmem_limit_bytes=VMEM_LIMIT),
        name="inproj",
    )(*args)


def _ret_kernel(x_ref, s0_ref, gain_ref, dec_ref, qs_ref, ks_ref, gbm_ref, bm_ref, bdv_ref,
                o_ref, sout_ref, s_ref, *, blk):
    @pl.when(pl.program_id(1) == 0)
    def _():
        s_ref[...] = s0_ref[...]

    inv_d = 1.0 / 64.0
    bdv = bdv_ref[...]
    lane = lax.broadcasted_iota(jnp.int32, (1, RET_WIDTH), 1)
    head_k = (lane % 128) // 32
    head_v = lane // 64
    for sb in range(x_ref.shape[0] // blk):
        rows = slice(sb * blk, (sb + 1) * blk)
        q = x_ref[rows, 0:256]
        k = x_ref[rows, 256:512]
        v = x_ref[rows, 512:768]
        gate = x_ref[rows, 768:1024].astype(F32)
        state = s_ref[...]
        qw = (q.astype(F32) * qs_ref[...]).astype(BF16)
        o = _dot(qw, state.astype(BF16))
        for hd in range(RET_HEADS):
            qh = jnp.where(head_k == hd, q, jnp.zeros_like(q))
            a = _dot_nt(qh, k) * dec_ref[hd]
            o = o + jnp.where(head_v == hd, _dot(a.astype(BF16), v), 0.0)
        kw_t = (k.astype(F32) * ks_ref[...]).T.astype(BF16)
        s_ref[...] = gbm_ref[...] * state + bm_ref[...] * _dot(kw_t, v)
        mu = _group_sum(o, bdv) * inv_d
        d = o - mu
        var = _group_sum(d * d, bdv) * inv_d
        on = d * lax.rsqrt(var + NORM_EPS) * gain_ref[...]
        o_ref[rows, :] = (on * (gate / (1.0 + jnp.exp(-gate)))).astype(BF16)
    sout_ref[...] = s_ref[...]


def _ret_constants(blk, chunk_len):
    log_g = jnp.log1p(-jnp.exp2(-5.0 - jnp.arange(RET_HEADS, dtype=F32)))
    lane = np.arange(RET_WIDTH)
    head_k = (lane % 128) // 32
    head_v = lane // 64
    i = jnp.arange(blk, dtype=F32)
    diff = i[:, None] - i[None, :]
    dec = jnp.where(diff[None] >= 0.0, jnp.exp(jnp.maximum(diff, 0.0)[None] * log_g[:, None, None]), 0.0)
    lg_k = log_g[head_k]
    qs = jnp.exp((i + 1.0)[:, None] * lg_k[None, :])
    ks = jnp.exp((chunk_len - 1.0 - i)[:, None] * lg_k[None, :])
    gbm = jnp.broadcast_to(jnp.exp(chunk_len * lg_k)[:, None], (RET_WIDTH, RET_WIDTH))
    bm = jnp.asarray(head_k[:, None] == head_v[None, :], F32)
    bdv = jnp.asarray(head_v[:, None] == head_v[None, :], BF16)
    return dec, qs, ks, gbm, bm, bdv


def _retention(l, ret_in, s0, gain, consts, blk, rt):
    g, r, _ = ret_in.shape
    grid = (g, r // rt)
    const2 = lambda shape: pl.BlockSpec(shape, lambda b, i: (0,) * len(shape))
    dec, qs, ks, gbm, bm, bdv = consts
    in_specs = [
        pl.BlockSpec((None, rt, 1024), lambda b, i: (b, i, 0)),
        pl.BlockSpec((None, RET_WIDTH, RET_WIDTH), lambda b, i: (b, 0, 0)),
        pl.BlockSpec((None, 1, RET_WIDTH), lambda b, i: (l, 0, 0)),
        const2(dec.shape), const2(qs.shape), const2(ks.shape), const2(gbm.shape), const2(bm.shape),
        const2(bdv.shape),
    ]
    return pl.pallas_call(
        functools.partial(_ret_kernel, blk=blk),
        grid=grid, in_specs=in_specs,
        out_specs=[pl.BlockSpec((None, rt, RET_WIDTH), lambda b, i: (b, i, 0)),
                   pl.BlockSpec((None, RET_WIDTH, RET_WIDTH), lambda b, i: (b, 0, 0))],
        out_shape=[jax.ShapeDtypeStruct((g, r, RET_WIDTH), BF16),
                   jax.ShapeDtypeStruct((g, RET_WIDTH, RET_WIDTH), F32)],
        scratch_shapes=[pltpu.VMEM((RET_WIDTH, RET_WIDTH), F32)],
        compiler_params=pltpu.CompilerParams(dimension_semantics=("arbitrary", "arbitrary"),
                                             vmem_limit_bytes=VMEM_LIMIT),
        name="retention",
    )(ret_in, s0, gain, dec, qs, ks, gbm, bm, bdv)


def _state_to_bd(st):
    g = st.shape[0]
    t = jnp.einsum("bhpie,hg->bphige", st.reshape(g, RET_HEADS, 2, 32, 64), jnp.eye(RET_HEADS, dtype=st.dtype))
    return t.reshape(g, RET_WIDTH, RET_WIDTH)


def _bd_to_state(sbd):
    g = sbd.shape[0]
    t = jnp.einsum("bphihe->bhpie", sbd.reshape(g, 2, RET_HEADS, 32, RET_HEADS, 64))
    return t.reshape(g, RET_HEADS, 64, 64)


def _cumsum_kernel(x_ref, u_ref, o_ref):
    nc = x_ref.shape[1] // LANES
    xs = jnp.concatenate([x_ref[:, LANES * j:LANES * (j + 1)] for j in range(nc)], axis=0)
    x1 = xs.astype(BF16)
    r1 = xs - x1.astype(F32)
    x2 = r1.astype(BF16)
    x3 = (r1 - x2.astype(F32)).astype(BF16)
    tri = u_ref[...]
    loc = _dot(x1, tri) + _dot(x2, tri) + _dot(x3, tri)
    off = jnp.zeros((SUBLANES, 1), F32)
    for j in range(nc):
        lj = loc[SUBLANES * j:SUBLANES * (j + 1), :]
        o_ref[:, LANES * j:LANES * (j + 1)] = (lj + off) * (-LOG2E)
        off = off + lj[:, LANES - 1:LANES]


def _cumsum_lanes(x):
    n, h, w = x.shape
    tri = jnp.asarray(np.arange(LANES)[:, None] <= np.arange(LANES)[None, :], BF16)
    return pl.pallas_call(
        _cumsum_kernel,
        grid=(n,),
        in_specs=[pl.BlockSpec((None, h, w), lambda b: (b, 0, 0)),
                  pl.BlockSpec((LANES, LANES), lambda b: (0, 0))],
        out_specs=pl.BlockSpec((None, h, w), lambda b: (b, 0, 0)),
        out_shape=jax.ShapeDtypeStruct((n, h, w), F32),
        compiler_params=pltpu.CompilerParams(dimension_semantics=("arbitrary",)),
        name="forget_cumsum",
    )(x, tri)


def _fox_prompt_kernel(q_ref, k_ref, v_ref, nc_ref, o_ref, va_ref, qh_ref, sd_ref, sa_ref, sb_ref, m_ref,
                       acc_ref, *, tile):
    nq = q_ref.shape[0] // tile
    lane = lax.broadcasted_iota(jnp.int32, (1, LANES), 1)
    row = lax.broadcasted_iota(jnp.int32, (tile, tile), 0)
    col = lax.broadcasted_iota(jnp.int32, (tile, tile), 1)
    v_all = v_ref[...]
    for hd in range(2):
        va_ref[hd] = jnp.where(lane // FOX_HEAD_DIM == hd, v_all, jnp.ones_like(v_all))

    def set_q(slot, qi):
        q = q_ref[pl.ds(pl.multiple_of(qi * tile, tile), tile), :]
        for hd in range(2):
            qh_ref[slot, hd] = jnp.where(lane // FOX_HEAD_DIM == hd, q, jnp.zeros_like(q))

    def park(dst_ref, qslot, kj, causal):
        kt = k_ref[pl.ds(pl.multiple_of(kj * tile, tile), tile), :]
        for hd in range(2):
            s = _dot_nt(qh_ref[qslot, hd], kt) + nc_ref[kj, hd:hd + 1, :]
            dst_ref[hd] = jnp.where(col <= row, s, -jnp.inf) if causal else s

    def consume(src_ref, kj):
        k0 = pl.multiple_of(kj * tile, tile)
        for hd in range(2):
            s = src_ref[hd]
            m_prev = m_ref[hd]
            m_new = jnp.maximum(m_prev, jnp.max(s, axis=-1, keepdims=True))
            alpha = jnp.exp2(m_prev - m_new)
            p = jnp.exp2(s - m_new[:, 0:1]).astype(BF16)
            acc_ref[hd] = alpha * acc_ref[hd] + _dot(p, va_ref[hd, pl.ds(k0, tile), :])
            m_ref[hd] = m_new

    def park_next_diag(qi):
        nxt = jnp.minimum(qi + 1, nq - 1)
        set_q(1 - qi % 2, nxt)
        park(sd_ref, 1 - qi % 2, nxt, True)

    set_q(0, 0)
    park(sd_ref, 0, 0, True)

    def q_body(qi, carry):
        qs = qi % 2
        m_ref[...] = jnp.full(m_ref.shape, -jnp.inf, F32)
        acc_ref[...] = jnp.zeros(acc_ref.shape, F32)

        @pl.when(qi == 0)
        def _():
            consume(sd_ref, 0)
            park_next_diag(qi)

        @pl.when(qi > 0)
        def _():
            park(sa_ref, qs, 0, False)
            consume(sd_ref, qi)

            def kv_pair(jp, c2):
                j = 2 * jp + 1
                park(sb_ref, qs, j, False)
                consume(sa_ref, j - 1)
                park(sa_ref, qs, j + 1, False)
                consume(sb_ref, j)
                return c2

            lax.fori_loop(0, (qi - 1) // 2, kv_pair, 0)

            @pl.when(qi % 2 == 0)
            def _():
                park(sb_ref, qs, qi - 1, False)
                consume(sa_ref, qi - 2)
                park_next_diag(qi)
                consume(sb_ref, qi - 1)

            @pl.when(qi % 2 == 1)
            def _():
                park_next_diag(qi)
                consume(sa_ref, qi - 1)

        outs = [acc_ref[hd] / pltpu.roll(acc_ref[hd], FOX_HEAD_DIM, axis=1) for hd in range(2)]
        q0 = pl.multiple_of(qi * tile, tile)
        o_ref[pl.ds(q0, tile), :] = jnp.where(lane < FOX_HEAD_DIM, outs[0], outs[1]).astype(BF16)
        return carry

    lax.fori_loop(0, nq, q_body, 0)


def _fox_prompt(q, k, v, c_tiles, tile):
    b, s, _ = q.shape
    nt = s // tile
    qkv_spec = pl.BlockSpec((None, s, LANES), lambda bi, hp: (bi, 0, hp))
    return pl.pallas_call(
        functools.partial(_fox_prompt_kernel, tile=tile),
        grid=(b, FOX_HEADS // 2),
        in_specs=[qkv_spec, qkv_spec, qkv_spec,
                  pl.BlockSpec((None, None, nt, 2, tile), lambda bi, hp: (bi, hp, 0, 0, 0))],
        out_specs=qkv_spec,
        out_shape=jax.ShapeDtypeStruct((b, s, FOX_WIDTH), BF16),
        scratch_shapes=[pltpu.VMEM((2, s, LANES), BF16), pltpu.VMEM((2, 2, tile, LANES), BF16),
                        pltpu.VMEM((2, tile, tile), F32), pltpu.VMEM((2, tile, tile), F32),
                        pltpu.VMEM((2, tile, tile), F32),
                        pltpu.VMEM((2, tile, LANES), F32), pltpu.VMEM((2, tile, LANES), F32)],
        compiler_params=pltpu.CompilerParams(dimension_semantics=("arbitrary", "arbitrary"),
                                             vmem_limit_bytes=VMEM_LIMIT),
        name="fox_prompt",
    )(q, k, v, c_tiles)


def _fox_sample_kernel(q_ref, kc_ref, vc_ref, kn_ref, vn_ref, ncc_ref, ncn_ref, o_ref,
                       msk_ref, m_ref, l_ref, acc_ref):
    chunk = pl.program_id(1)
    nrow = q_ref.shape[0]
    ncol = kc_ref.shape[0]

    @pl.when(jnp.logical_and(pl.program_id(0) == 0, chunk == 0))
    def _():
        r = lax.broadcasted_iota(jnp.int32, (nrow, ncol), 0)
        c = lax.broadcasted_iota(jnp.int32, (nrow, ncol), 1)
        msk_ref[...] = jnp.where(r % FOX_HEADS == c % FOX_HEADS, 0.0, -jnp.inf)

    @pl.when(chunk == 0)
    def _():
        m_ref[...] = jnp.full(m_ref.shape, -jnp.inf, F32)
        l_ref[...] = jnp.zeros(l_ref.shape, F32)
        acc_ref[...] = jnp.zeros(acc_ref.shape, F32)

    q = q_ref[...]

    def update(s, v):
        m_prev = m_ref[...]
        m_new = jnp.maximum(m_prev, jnp.max(s, axis=-1, keepdims=True))
        alpha = jnp.exp2(m_prev - m_new)
        p = jnp.exp2(s - m_new[:, 0:1])
        l_ref[...] = alpha * l_ref[...] + jnp.sum(p, axis=-1, keepdims=True)
        acc_ref[...] = alpha[:, 0:FOX_HEAD_DIM] * acc_ref[...] + _dot(p.astype(BF16), v)
        m_ref[...] = m_new

    update(_dot_nt(q, kc_ref[...].astype(BF16)) + ncc_ref[...] + msk_ref[...], vc_ref[...].astype(BF16))

    @pl.when(chunk == pl.num_programs(1) - 1)
    def _():
        r = lax.broadcasted_iota(jnp.int32, (nrow, nrow), 0)
        c = lax.broadcasted_iota(jnp.int32, (nrow, nrow), 1)
        ok = jnp.logical_and(r % FOX_HEADS == c % FOX_HEADS, c // FOX_HEADS <= r // FOX_HEADS)
        sn = _dot_nt(q, kn_ref[...].astype(BF16)) + ncn_ref[...]
        update(jnp.where(ok, sn, -jnp.inf), vn_ref[...].astype(BF16))
        o_ref[...] = (acc_ref[...] / l_ref[:, 0:FOX_HEAD_DIM]).astype(BF16)


def _fox_sample(l, q, kn, vn, cache_k, cache_v, nc):
    nb, nrow, _ = q.shape
    past8 = cache_k.shape[2]
    ck = min(FOX_SAMPLE_CHUNK * FOX_HEADS, past8)
    new_spec = pl.BlockSpec((None, nrow, FOX_HEAD_DIM), lambda b, c: (b, 0, 0))
    cache_spec = pl.BlockSpec((None, None, ck, FOX_HEAD_DIM), lambda b, c: (l, b, c, 0))
    return pl.pallas_call(
        _fox_sample_kernel,
        grid=(nb, past8 // ck),
        in_specs=[new_spec, cache_spec, cache_spec, new_spec, new_spec,
                  pl.BlockSpec((None, 1, ck), lambda b, c: (b, 0, c)),
                  pl.BlockSpec((None, 1, nrow), lambda b, c: (b, 0, past8 // nrow))],
        out_specs=new_spec,
        out_shape=jax.ShapeDtypeStruct((nb, nrow, FOX_HEAD_DIM), BF16),
        scratch_shapes=[pltpu.VMEM((nrow, ck), F32), pltpu.VMEM((nrow, LANES), F32),
                        pltpu.VMEM((nrow, LANES), F32), pltpu.VMEM((nrow, FOX_HEAD_DIM), F32)],
        compiler_params=pltpu.CompilerParams(dimension_semantics=("arbitrary", "arbitrary"),
                                             vmem_limit_bytes=VMEM_LIMIT),
        name="fox_sample",
    )(q, cache_k, cache_v, kn, vn, nc, nc)


def _mlp_kernel(*refs, tmajor, nb):
    if tmajor:
        (x_ref, oret_ref, ofox_ref, oconv_ref, wout_ref, nw_ref, wg_ref, wu_ref, wd_ref, fcw_ref, hist_ref,
         y_ref, st_ref) = refs
    else:
        (x_ref, oret_ref, ofox_ref, oconv_ref, wout_ref, nw_ref, wg_ref, wu_ref, wd_ref, fcw_ref,
         y_ref, st_ref, carry_ref) = refs
    x1 = (x_ref[...] + _dot(oret_ref[...], wout_ref[0:256, :]) + _dot(ofox_ref[...], wout_ref[256:768, :])
          + _dot(oconv_ref[...], wout_ref[768:1024, :]))
    h2 = _rms(x1, nw_ref[...]).astype(BF16)
    down = None
    for c0, c1 in FF_CHUNKS:
        cols = slice(c0, c1)
        gate_pre = _dot(h2, wg_ref[:, cols])
        if tmajor:
            gate_c, new_hist = _dwconv_tmajor(gate_pre, fcw_ref.at[:, cols], hist_ref[:, cols], nb)
            st_ref[:, cols] = new_hist
        else:
            gate_c = _dwconv_carry(gate_pre, fcw_ref.at[:, cols], carry_ref.at[:, cols], pl.program_id(1) == 0)
            st_ref[:, cols] = carry_ref[6:8, cols]
        up = _dot(h2, wu_ref[:, cols])
        act = (gate_c / (1.0 + jnp.exp(-gate_c)) * up).astype(BF16)
        part = _dot(act, wd_ref[cols, :])
        down = part if down is None else down + part
    y_ref[...] = x1 + down


def _mlp(l, x, oret, ofox, oconv, prm, *, hist=None, nb=0):
    tmajor = hist is not None
    g, r, _ = x.shape
    tm = min(TOKEN_TILE, r)
    grid = (g, r // tm)
    lsel = lambda b, i: (l, 0, 0)
    tok = lambda width: pl.BlockSpec((None, tm, width), lambda b, i: (b, i, 0))
    wspec = lambda rows, cols: pl.BlockSpec((None, rows, cols), lsel, pipeline_mode=pl.Buffered(1))
    in_specs = [tok(D_MODEL), tok(RET_WIDTH), tok(FOX_WIDTH), tok(CONV_DIM),
                wspec(D_MODEL, D_MODEL), pl.BlockSpec((None, 1, D_MODEL), lsel),
                wspec(D_MODEL, D_FF), wspec(D_MODEL, D_FF), wspec(D_FF, D_MODEL),
                pl.BlockSpec((None, 3, D_FF), lsel)]
    args = [x, oret, ofox, oconv, prm["w_out"], prm["norm_ffn"], prm["w_gate"], prm["w_up"], prm["w_down"],
            prm["ffn_conv_w"]]
    if tmajor:
        in_specs.append(pl.BlockSpec((2 * nb, D_FF), lambda b, i: (0, 0)))
        args.append(hist)
        st_shape = jax.ShapeDtypeStruct((2 * nb, D_FF), F32)
        st_spec = pl.BlockSpec((2 * nb, D_FF), lambda b, i: (0, 0))
        scratch = []
    else:
        st_shape = jax.ShapeDtypeStruct((g, 2, D_FF), F32)
        st_spec = pl.BlockSpec((None, 2, D_FF), lambda b, i: (b, 0, 0))
        scratch = [pltpu.VMEM((SUBLANES, D_FF), F32)]
    return pl.pallas_call(
        functools.partial(_mlp_kernel, tmajor=tmajor, nb=nb),
        grid=grid, in_specs=in_specs,
        out_specs=[tok(D_MODEL), st_spec],
        out_shape=[jax.ShapeDtypeStruct((g, r, D_MODEL), F32), st_shape],
        scratch_shapes=scratch,
        compiler_params=pltpu.CompilerParams(dimension_semantics=("arbitrary", "arbitrary"),
                                             vmem_limit_bytes=VMEM_LIMIT),
        name="outproj_mlp",
    )(*args)


def _win_permutation():
    idx = []
    for base in (0, 256):
        for half in range(2):
            for hd in range(RET_HEADS):
                idx += [base + 64 * hd + 32 * half + i for i in range(32)]
    idx += list(range(512, 1024))
    idx += list(range(1024, 2560))
    idx += list(range(2568, 3336))
    idx += list(range(2560, 2568))
    return np.asarray(idx, np.int32)


def _rope_tables(pos):
    half = 32
    inv_freq = ROPE_BASE ** (-jnp.arange(half, dtype=F32) / half)
    ang = pos.astype(F32)[:, None] * inv_freq[None, :]
    return jnp.tile(jnp.cos(ang), (1, RET_HEADS)), jnp.tile(jnp.sin(ang), (1, RET_HEADS))


def _to_bmajor(a, n, nb):
    return a.reshape(n, nb, a.shape[-1]).transpose(1, 0, 2)


def _to_tmajor(a):
    nb, n, c = a.shape
    return a.transpose(1, 0, 2).reshape(1, n * nb, c)


def kernel(x_prompt, x_sample, cache_fox_k, cache_fox_v, cache_fox_logf, state_ret, state_conv, state_ffn_conv,
           norm_mix, w_in, ret_gn_gain, fox_q_gain, fox_k_gain, fox_f_bias, conv_w, w_out, norm_ffn, w_gate,
           w_up, ffn_conv_w, w_down):
    depth = w_in.shape[0]
    b, s, _ = x_prompt.shape
    nb, n, _ = x_sample.shape
    past = cache_fox_k.shape[2]

    w_perm = jnp.take(w_in, _win_permutation(), axis=2)
    w_perm = jnp.pad(w_perm, ((0, 0), (0, 0), (0, PROJ_PAD - w_perm.shape[2]))).astype(BF16)
    prm = {
        "w_in": w_perm,
        "norm_mix": norm_mix.reshape(depth, 1, D_MODEL),
        "gq": fox_q_gain.reshape(depth, 1, FOX_WIDTH),
        "gk": fox_k_gain.reshape(depth, 1, FOX_WIDTH),
        "fb": jnp.pad(fox_f_bias, ((0, 0), (0, LANES - FOX_HEADS))).reshape(depth, 1, LANES),
        "conv_w": conv_w,
        "w_out": w_out.astype(BF16),
        "norm_ffn": norm_ffn.reshape(depth, 1, D_MODEL),
        "w_gate": w_gate.astype(BF16),
        "w_up": w_up.astype(BF16),
        "w_down": w_down.astype(BF16),
        "ffn_conv_w": ffn_conv_w,
    }
    gain = ret_gn_gain.reshape(depth, 1, RET_WIDTH)
    lane = np.arange(FOX_WIDTH)
    bd = jnp.asarray(lane[:, None] // FOX_HEAD_DIM == lane[None, :] // FOX_HEAD_DIM, BF16)

    cos_p, sin_p = _rope_tables(jnp.arange(s, dtype=jnp.int32))
    pos_s = past + jnp.arange(n, dtype=jnp.int32)
    cos_s, sin_s = _rope_tables(jnp.repeat(pos_s, nb))

    ret_blk = min(RET_BLOCK, s)
    ret_rt = min(TOKEN_TILE, s)
    consts_p = _ret_constants(ret_blk, float(ret_blk))
    samp_rows = LANES
    consts_s = _ret_constants(samp_rows, float(n))
    fox_tile = min(FOX_TILE, s)

    cache_k = cache_fox_k.reshape(depth, nb, past * FOX_HEADS, FOX_HEAD_DIM)
    cache_v = cache_fox_v.reshape(depth, nb, past * FOX_HEADS, FOX_HEAD_DIM)
    cache_lf_t = jnp.transpose(cache_fox_logf, (0, 1, 3, 2))
    c_width = -(-(past + n) // LANES) * LANES

    kbuf = jnp.zeros((depth, b, s * FOX_HEADS, FOX_HEAD_DIM), F32)
    vbuf = jnp.zeros((depth, b, s * FOX_HEADS, FOX_HEAD_DIM), F32)
    zero_state = jnp.zeros((b, RET_WIDTH, RET_WIDTH), F32)

    xp = x_prompt
    xs = jnp.transpose(x_sample, (1, 0, 2)).reshape(1, n * nb, D_MODEL)
    p_lf, p_ret, p_conv, p_ffn = [], [], [], []
    s_k, s_v, s_lf, s_ret, s_conv, s_ffn = [], [], [], [], [], []
    for l in range(depth):
        ret_in, fq, kbuf, vbuf, fkb, fvb, lf, oconv, conv_st = _inproj(
            l, xp, prm, cos_p, sin_p, bd, kbuf=kbuf, vbuf=vbuf)
        oret, sbd = _retention(l, ret_in, zero_state, gain, consts_p, ret_blk, ret_rt)
        c_rows = _cumsum_lanes(jnp.transpose(lf, (0, 2, 1)))
        c_tiles = c_rows.reshape(b, FOX_HEADS // 2, 2, s // fox_tile, fox_tile).transpose(0, 1, 3, 2, 4)
        ofox = _fox_prompt(fq, fkb, fvb, c_tiles, fox_tile)
        xp, ffn_st = _mlp(l, xp, oret, ofox, oconv, prm)
        p_lf.append(lf)
        p_ret.append(_bd_to_state(sbd))
        p_conv.append(conv_st)
        p_ffn.append(ffn_st)

        hist_c = jnp.transpose(state_conv[l], (1, 0, 2)).reshape(2 * nb, CONV_DIM)
        ret_in, fq, fk32, fv32, fkb, fvb, lf, oconv, conv_st = _inproj(
            l, xs, prm, cos_s, sin_s, bd, hist=hist_c, nb=nb)
        ret_b = jnp.pad(_to_bmajor(ret_in, n, nb), ((0, 0), (0, samp_rows - n), (0, 0)))
        oret, sbd = _retention(l, ret_b, _state_to_bd(state_ret[l]), gain, consts_s, samp_rows, samp_rows)
        lf_b = _to_bmajor(lf, n, nb)
        lf_all = jnp.concatenate(
            [cache_lf_t[l], jnp.transpose(lf_b, (0, 2, 1)),
             jnp.zeros((nb, FOX_HEADS, c_width - past - n), F32)], axis=2)
        nc = jnp.transpose(_cumsum_lanes(lf_all), (0, 2, 1)).reshape(nb, 1, c_width * FOX_HEADS)
        s_k.append(_to_bmajor(fk32, n, nb).reshape(nb, n, FOX_HEADS, FOX_HEAD_DIM))
        s_v.append(_to_bmajor(fv32, n, nb).reshape(nb, n, FOX_HEADS, FOX_HEAD_DIM))
        rows_th = lambda a: a.reshape(nb, n * FOX_HEADS, FOX_HEAD_DIM)
        ofox = _fox_sample(l, rows_th(_to_bmajor(fq, n, nb)), rows_th(s_k[-1]), rows_th(s_v[-1]),
                           cache_k, cache_v, nc).reshape(nb, n, FOX_WIDTH)
        hist_f = jnp.transpose(state_ffn_conv[l], (1, 0, 2)).reshape(2 * nb, D_FF)
        xs, ffn_st = _mlp(l, xs, _to_tmajor(oret[:, :n]), _to_tmajor(ofox), oconv, prm, hist=hist_f, nb=nb)
        s_lf.append(lf_b)
        s_ret.append(_bd_to_state(sbd))
        s_conv.append(conv_st.reshape(2, nb, CONV_DIM).transpose(1, 0, 2))
        s_ffn.append(ffn_st.reshape(2, nb, D_FF).transpose(1, 0, 2))

    y_sample = xs.reshape(n, nb, D_MODEL).transpose(1, 0, 2)
    stk = lambda ts: jnp.stack(ts, axis=0)
    return (xp, y_sample,
            kbuf.reshape(depth, b, s, FOX_HEADS, FOX_HEAD_DIM), vbuf.reshape(depth, b, s, FOX_HEADS, FOX_HEAD_DIM),
            stk(p_lf), stk(p_ret), stk(p_conv), stk(p_ffn),
            stk(s_k), stk(s_v), stk(s_lf), stk(s_ret), stk(s_conv), stk(s_ffn))
```

```python
import functools

import numpy as np
import jax
import jax.numpy as jnp
from jax import lax
from jax.experimental import pallas as pl
from jax.experimental.pallas import tpu as pltpu

F32 = jnp.float32
BF16 = jnp.bfloat16

D_MODEL = 1024
RET_HEADS = 4
RET_WIDTH = 256
FOX_HEADS = 8
FOX_HEAD_DIM = 64
FOX_WIDTH = 512
CONV_DIM = 256
D_FF = 2816
PROJ_PAD = 3456
ROPE_BASE = 10000.0
NORM_EPS = 1e-6
LANES = 128
SUBLANES = 8
VMEM_LIMIT = 56 * 1024 * 1024

TOKEN_TILE = 512
RET_BLOCK = 256
FOX_TILE = 512
FOX_SAMPLE_CHUNK = 1024
LOG2E = 1.4426950408889634
FOX_Q_SCALE = FOX_HEAD_DIM ** -0.5 * LOG2E
MXU_DEPTH = 256
FF_CHUNKS = ((0, 5 * MXU_DEPTH), (5 * MXU_DEPTH, D_FF))


def _dot(a, b):
    return jnp.dot(a, b, preferred_element_type=F32)


def _dot_nt(a, b):
    return lax.dot_general(a, b, (((1,), (1,)), ((), ())), preferred_element_type=F32)


def _split2(x):
    hi = x.astype(BF16)
    lo = (x - hi.astype(F32)).astype(BF16)
    return hi, lo


def _group_sum(x, ones_bd):
    hi, lo = _split2(x)
    return _dot(hi, ones_bd) + _dot(lo, ones_bd)


def _rms(x, g):
    ms = jnp.mean(x * x, axis=-1, keepdims=True)
    return x * lax.rsqrt(ms + NORM_EPS) * g


def _dwconv_carry(u, w_ref, carry_ref, first):
    tm = u.shape[0]

    @pl.when(first)
    def _():
        carry_ref[...] = jnp.zeros(carry_ref.shape, F32)

    row = lax.broadcasted_iota(jnp.int32, u.shape, 0)
    c6 = carry_ref[6:7, :]
    c7 = carry_ref[7:8, :]
    um1 = jnp.where(row == 0, c7, pltpu.roll(u, 1, axis=0))
    um2 = jnp.where(row == 0, c6, jnp.where(row == 1, c7, pltpu.roll(u, 2, axis=0)))
    y = w_ref[0:1, :] * um2 + w_ref[1:2, :] * um1 + w_ref[2:3, :] * u
    carry_ref[...] = u[tm - SUBLANES:tm, :]
    return y


def _dwconv_tmajor(u, w_ref, hist, nb):
    tm = u.shape[0]
    full = jnp.concatenate([hist, u], axis=0)
    y = (w_ref[0:1, :] * full[0:tm] + w_ref[1:2, :] * full[nb:nb + tm]
         + w_ref[2:3, :] * full[2 * nb:2 * nb + tm])
    return y, full[tm:tm + 2 * nb]


def _inproj_kernel(*refs, tmajor, nb):
    if tmajor:
        (x_ref, nw_ref, w_ref, cos_ref, sin_ref, gq_ref, gk_ref, fb_ref, cw_ref, bd_ref, hist_ref,
         ret_ref, fq_ref, fk32_ref, fv32_ref, fkb_ref, fvb_ref, lf_ref, oc_ref, st_ref) = refs
    else:
        (x_ref, nw_ref, w_ref, cos_ref, sin_ref, gq_ref, gk_ref, fb_ref, cw_ref, bd_ref,
         _, _, ret_ref, fq_ref, fk32_ref, fv32_ref, fkb_ref, fvb_ref, lf_ref, oc_ref, st_ref,
         carry_ref) = refs
    h = _rms(x_ref[...], nw_ref[...]).astype(BF16)

    a = _dot(h, w_ref[:, 0:1024])
    cos = cos_ref[...]
    sin = sin_ref[...]
    q1, q2, k1, k2 = a[:, 0:128], a[:, 128:256], a[:, 256:384], a[:, 384:512]
    ret_ref[:, 0:128] = (q1 * cos - q2 * sin).astype(BF16)
    ret_ref[:, 128:256] = (q1 * sin + q2 * cos).astype(BF16)
    ret_ref[:, 256:384] = ((k1 * cos - k2 * sin) * 0.125).astype(BF16)
    ret_ref[:, 384:512] = ((k1 * sin + k2 * cos) * 0.125).astype(BF16)
    ret_ref[:, 512:1024] = a[:, 512:1024].astype(BF16)

    f = _dot(h, w_ref[:, 1024:2560])
    fq, fk, fv = f[:, 0:512], f[:, 512:1024], f[:, 1024:1536]
    bd = bd_ref[...]
    inv_d = 1.0 / FOX_HEAD_DIM
    ssq = _group_sum(fq * fq, bd)
    fq_ref[...] = (fq * lax.rsqrt(ssq * inv_d + NORM_EPS) * gq_ref[...] * FOX_Q_SCALE).astype(BF16)
    ssk = _group_sum(fk * fk, bd)
    fkn = fk * lax.rsqrt(ssk * inv_d + NORM_EPS) * gk_ref[...]
    fkb_ref[...] = fkn.astype(BF16)
    fvb_ref[...] = fv.astype(BF16)
    if tmajor:
        fk32_ref[...] = fkn
        fv32_ref[...] = fv
    else:
        fk32_ref[...] = fkn.T
        fv32_ref[...] = fv.T

    c = _dot(h, w_ref[:, 2560:3328])
    cb, cc, ch = c[:, 0:256], c[:, 256:512], c[:, 512:768]
    u = cc * ch
    if tmajor:
        y, new_hist = _dwconv_tmajor(u, cw_ref, hist_ref[...], nb)
        st_ref[...] = new_hist
    else:
        y = _dwconv_carry(u, cw_ref, carry_ref, pl.program_id(1) == 0)
        st_ref[...] = carry_ref[6:8, :]
    oc_ref[...] = (cb * y).astype(BF16)

    z = _dot(h, w_ref[:, 3328:3456]) + fb_ref[...]
    ls = jnp.minimum(z, 0.0) - jnp.log1p(jnp.exp(-jnp.abs(z)))
    lf_ref[...] = ls[:, 0:FOX_HEADS]


def _inproj(l, x, prm, cos, sin, bd, *, kbuf=None, vbuf=None, hist=None, nb=0):
    tmajor = hist is not None
    g, r, _ = x.shape
    tm = min(TOKEN_TILE, r)
    grid = (g, r // tm)
    depth = prm["w_in"].shape[0]
    lsel = lambda b, i: (l, 0, 0)
    in_specs = [
        pl.BlockSpec((None, tm, D_MODEL), lambda b, i: (b, i, 0)),
        pl.BlockSpec((None, 1, D_MODEL), lsel),
        pl.BlockSpec((None, D_MODEL, PROJ_PAD), lsel, pipeline_mode=pl.Buffered(1)),
        pl.BlockSpec((tm, LANES), lambda b, i: (i, 0)),
        pl.BlockSpec((tm, LANES), lambda b, i: (i, 0)),
        pl.BlockSpec((None, 1, FOX_WIDTH), lsel),
        pl.BlockSpec((None, 1, FOX_WIDTH), lsel),
        pl.BlockSpec((None, 1, LANES), lsel),
        pl.BlockSpec((None, 3, CONV_DIM), lsel),
        pl.BlockSpec((FOX_WIDTH, FOX_WIDTH), lambda b, i: (0, 0)),
    ]
    args = [x, prm["norm_mix"], prm["w_in"], cos, sin, prm["gq"], prm["gk"], prm["fb"], prm["conv_w"], bd]
    tok = lambda width: pl.BlockSpec((None, tm, width), lambda b, i: (b, i, 0))
    if tmajor:
        in_specs.append(pl.BlockSpec((2 * nb, CONV_DIM), lambda b, i: (0, 0)))
        args.append(hist)
        kv_shape = jax.ShapeDtypeStruct((g, r, FOX_WIDTH), F32)
        kv_spec = tok(FOX_WIDTH)
        st_shape = jax.ShapeDtypeStruct((2 * nb, CONV_DIM), F32)
        st_spec = pl.BlockSpec((2 * nb, CONV_DIM), lambda b, i: (0, 0))
        aliases = {}
        scratch = []
    else:
        in_specs += [pl.BlockSpec(memory_space=pl.ANY), pl.BlockSpec(memory_space=pl.ANY)]
        args += [kbuf, vbuf]
        kv_shape = jax.ShapeDtypeStruct((depth, g, FOX_WIDTH, r), F32)
        kv_spec = pl.BlockSpec((None, None, FOX_WIDTH, tm), lambda b, i: (l, b, 0, i))
        st_shape = jax.ShapeDtypeStruct((g, 2, CONV_DIM), F32)
        st_spec = pl.BlockSpec((None, 2, CONV_DIM), lambda b, i: (b, 0, 0))
        aliases = {10: 2, 11: 3}
        scratch = [pltpu.VMEM((SUBLANES, CONV_DIM), F32)]
    out_shape = [
        jax.ShapeDtypeStruct((g, r, 1024), BF16),
        jax.ShapeDtypeStruct((g, r, FOX_WIDTH), BF16),
        kv_shape, kv_shape,
        jax.ShapeDtypeStruct((g, r, FOX_WIDTH), BF16),
        jax.ShapeDtypeStruct((g, r, FOX_WIDTH), BF16),
        jax.ShapeDtypeStruct((g, r, FOX_HEADS), F32),
        jax.ShapeDtypeStruct((g, r, CONV_DIM), BF16),
        st_shape,
    ]
    out_specs = [tok(1024), tok(FOX_WIDTH), kv_spec, kv_spec, tok(FOX_WIDTH), tok(FOX_WIDTH),
                 tok(FOX_HEADS), tok(CONV_DIM), st_spec]
    return pl.pallas_call(
        functools.partial(_inproj_kernel, tmajor=tmajor, nb=nb),
        grid=grid, in_specs=in_specs, out_specs=out_specs, out_shape=out_shape,
        scratch_shapes=scratch, input_output_aliases=aliases,
        compiler_params=pltpu.CompilerParams(dimension_semantics=("arbitrary", "arbitrary"),
                                             vmem_limit_bytes=VMEM_LIMIT),
        name="inproj",
    )(*args)


def _ret_kernel(x_ref, s0_ref, gain_ref, dec_ref, qs_ref, ks_ref, gbm_ref, bm_ref, bdv_ref,
                o_ref, sout_ref, s_ref, *, blk):
    @pl.when(pl.program_id(1) == 0)
    def _():
        s_ref[...] = s0_ref[...]

    inv_d = 1.0 / 64.0
    bdv = bdv_ref[...]
    lane = lax.broadcasted_iota(jnp.int32, (1, RET_WIDTH), 1)
    head_k = (lane % 128) // 32
    head_v = lane // 64
    for sb in range(x_ref.shape[0] // blk):
        rows = slice(sb * blk, (sb + 1) * blk)
        q = x_ref[rows, 0:256]
        k = x_ref[rows, 256:512]
        v = x_ref[rows, 512:768]
        gate = x_ref[rows, 768:1024].astype(F32)
        state = s_ref[...]
        qw = (q.astype(F32) * qs_ref[...]).astype(BF16)
        o = _dot(qw, state.astype(BF16))
        for hd in range(RET_HEADS):
            qh = jnp.where(head_k == hd, q, jnp.zeros_like(q))
            a = _dot_nt(qh, k) * dec_ref[hd]
            o = o + jnp.where(head_v == hd, _dot(a.astype(BF16), v), 0.0)
        kw_t = (k.astype(F32) * ks_ref[...]).T.astype(BF16)
        s_ref[...] = gbm_ref[...] * state + bm_ref[...] * _dot(kw_t, v)
        mu = _group_sum(o, bdv) * inv_d
        d = o - mu
        var = _group_sum(d * d, bdv) * inv_d
        on = d * lax.rsqrt(var + NORM_EPS) * gain_ref[...]
        o_ref[rows, :] = (on * (gate / (1.0 + jnp.exp(-gate)))).astype(BF16)
    sout_ref[...] = s_ref[...]


def _ret_constants(blk, chunk_len):
    log_g = jnp.log1p(-jnp.exp2(-5.0 - jnp.arange(RET_HEADS, dtype=F32)))
    lane = np.arange(RET_WIDTH)
    head_k = (lane % 128) // 32
    head_v = lane // 64
    i = jnp.arange(blk, dtype=F32)
    diff = i[:, None] - i[None, :]
    dec = jnp.where(diff[None] >= 0.0, jnp.exp(jnp.maximum(diff, 0.0)[None] * log_g[:, None, None]), 0.0)
    lg_k = log_g[head_k]
    qs = jnp.exp((i + 1.0)[:, None] * lg_k[None, :])
    ks = jnp.exp((chunk_len - 1.0 - i)[:, None] * lg_k[None, :])
    gbm = jnp.broadcast_to(jnp.exp(chunk_len * lg_k)[:, None], (RET_WIDTH, RET_WIDTH))
    bm = jnp.asarray(head_k[:, None] == head_v[None, :], F32)
    bdv = jnp.asarray(head_v[:, None] == head_v[None, :], BF16)
    return dec, qs, ks, gbm, bm, bdv


def _retention(l, ret_in, s0, gain, consts, blk, rt):
    g, r, _ = ret_in.shape
    grid = (g, r // rt)
    const2 = lambda shape: pl.BlockSpec(shape, lambda b, i: (0,) * len(shape))
    dec, qs, ks, gbm, bm, bdv = consts
    in_specs = [
        pl.BlockSpec((None, rt, 1024), lambda b, i: (b, i, 0)),
        pl.BlockSpec((None, RET_WIDTH, RET_WIDTH), lambda b, i: (b, 0, 0)),
        pl.BlockSpec((None, 1, RET_WIDTH), lambda b, i: (l, 0, 0)),
        const2(dec.shape), const2(qs.shape), const2(ks.shape), const2(gbm.shape), const2(bm.shape),
        const2(bdv.shape),
    ]
    return pl.pallas_call(
        functools.partial(_ret_kernel, blk=blk),
        grid=grid, in_specs=in_specs,
        out_specs=[pl.BlockSpec((None, rt, RET_WIDTH), lambda b, i: (b, i, 0)),
                   pl.BlockSpec((None, RET_WIDTH, RET_WIDTH), lambda b, i: (b, 0, 0))],
        out_shape=[jax.ShapeDtypeStruct((g, r, RET_WIDTH), BF16),
                   jax.ShapeDtypeStruct((g, RET_WIDTH, RET_WIDTH), F32)],
        scratch_shapes=[pltpu.VMEM((RET_WIDTH, RET_WIDTH), F32)],
        compiler_params=pltpu.CompilerParams(dimension_semantics=("arbitrary", "arbitrary"),
                                             vmem_limit_bytes=VMEM_LIMIT),
        name="retention",
    )(ret_in, s0, gain, dec, qs, ks, gbm, bm, bdv)


def _state_to_bd(st):
    g = st.shape[0]
    t = jnp.einsum("bhpie,hg->bphige", st.reshape(g, RET_HEADS, 2, 32, 64), jnp.eye(RET_HEADS, dtype=st.dtype))
    return t.reshape(g, RET_WIDTH, RET_WIDTH)


def _bd_to_state(sbd):
    g = sbd.shape[0]
    t = jnp.einsum("bphihe->bhpie", sbd.reshape(g, 2, RET_HEADS, 32, RET_HEADS, 64))
    return t.reshape(g, RET_HEADS, 64, 64)


def _cumsum_kernel(x_ref, u_ref, o_ref):
    nc = x_ref.shape[1] // LANES
    xs = jnp.concatenate([x_ref[:, LANES * j:LANES * (j + 1)] for j in range(nc)], axis=0)
    x1 = xs.astype(BF16)
    r1 = xs - x1.astype(F32)
    x2 = r1.astype(BF16)
    x3 = (r1 - x2.astype(F32)).astype(BF16)
    tri = u_ref[...]
    loc = _dot(x1, tri) + _dot(x2, tri) + _dot(x3, tri)
    off = jnp.zeros((SUBLANES, 1), F32)
    for j in range(nc):
        lj = loc[SUBLANES * j:SUBLANES * (j + 1), :]
        o_ref[:, LANES * j:LANES * (j + 1)] = (lj + off) * (-LOG2E)
        off = off + lj[:, LANES - 1:LANES]


def _cumsum_lanes(x):
    n, h, w = x.shape
    tri = jnp.asarray(np.arange(LANES)[:, None] <= np.arange(LANES)[None, :], BF16)
    return pl.pallas_call(
        _cumsum_kernel,
        grid=(n,),
        in_specs=[pl.BlockSpec((None, h, w), lambda b: (b, 0, 0)),
                  pl.BlockSpec((LANES, LANES), lambda b: (0, 0))],
        out_specs=pl.BlockSpec((None, h, w), lambda b: (b, 0, 0)),
        out_shape=jax.ShapeDtypeStruct((n, h, w), F32),
        compiler_params=pltpu.CompilerParams(dimension_semantics=("arbitrary",)),
        name="forget_cumsum",
    )(x, tri)


def _fox_prompt_kernel(q_ref, k_ref, v_ref, nc_ref, o_ref, va_ref, qh_ref, sd_ref, sa_ref, sb_ref, m_ref,
                       acc_ref, *, tile):
    nq = q_ref.shape[0] // tile
    lane = lax.broadcasted_iota(jnp.int32, (1, LANES), 1)
    row = lax.broadcasted_iota(jnp.int32, (tile, tile), 0)
    col = lax.broadcasted_iota(jnp.int32, (tile, tile), 1)
    v_all = v_ref[...]
    for hd in range(2):
        va_ref[hd] = jnp.where(lane // FOX_HEAD_DIM == hd, v_all, jnp.ones_like(v_all))

    def set_q(slot, qi):
        q = q_ref[pl.ds(pl.multiple_of(qi * tile, tile), tile), :]
        for hd in range(2):
            qh_ref[slot, hd] = jnp.where(lane // FOX_HEAD_DIM == hd, q, jnp.zeros_like(q))

    def park(dst_ref, qslot, kj, causal):
        kt = k_ref[pl.ds(pl.multiple_of(kj * tile, tile), tile), :]
        for hd in range(2):
            s = _dot_nt(qh_ref[qslot, hd], kt) + nc_ref[kj, hd:hd + 1, :]
            dst_ref[hd] = jnp.where(col <= row, s, -jnp.inf) if causal else s

    def consume(src_ref, kj):
        k0 = pl.multiple_of(kj * tile, tile)
        for hd in range(2):
            s = src_ref[hd]
            m_prev = m_ref[hd]
            m_new = jnp.maximum(m_prev, jnp.max(s, axis=-1, keepdims=True))
            alpha = jnp.exp2(m_prev - m_new)
            p = jnp.exp2(s - m_new[:, 0:1]).astype(BF16)
            acc_ref[hd] = alpha * acc_ref[hd] + _dot(p, va_ref[hd, pl.ds(k0, tile), :])
            m_ref[hd] = m_new

    def park_next_diag(qi):
        nxt = jnp.minimum(qi + 1, nq - 1)
        set_q(1 - qi % 2, nxt)
        park(sd_ref, 1 - qi % 2, nxt, True)

    set_q(0, 0)
    park(sd_ref, 0, 0, True)

    def q_body(qi, carry):
        qs = qi % 2
        m_ref[...] = jnp.full(m_ref.shape, -jnp.inf, F32)
        acc_ref[...] = jnp.zeros(acc_ref.shape, F32)

        @pl.when(qi == 0)
        def _():
            consume(sd_ref, 0)
            park_next_diag(qi)

        @pl.when(qi > 0)
        def _():
            park(sa_ref, qs, 0, False)
            consume(sd_ref, qi)

            def kv_pair(jp, c2):
                j = 2 * jp + 1
                park(sb_ref, qs, j, False)
                consume(sa_ref, j - 1)
                park(sa_ref, qs, j + 1, False)
                consume(sb_ref, j)
                return c2

            lax.fori_loop(0, (qi - 1) // 2, kv_pair, 0)

            @pl.when(qi % 2 == 0)
            def _():
                park(sb_ref, qs, qi - 1, False)
                consume(sa_ref, qi - 2)
                park_next_diag(qi)
                consume(sb_ref, qi - 1)

            @pl.when(qi % 2 == 1)
            def _():
                park_next_diag(qi)
                consume(sa_ref, qi - 1)

        outs = [acc_ref[hd] / pltpu.roll(acc_ref[hd], FOX_HEAD_DIM, axis=1) for hd in range(2)]
        q0 = pl.multiple_of(qi * tile, tile)
        o_ref[pl.ds(q0, tile), :] = jnp.where(lane < FOX_HEAD_DIM, outs[0], outs[1]).astype(BF16)
        return carry

    lax.fori_loop(0, nq, q_body, 0)


def _fox_prompt(q, k, v, c_tiles, tile):
    b, s, _ = q.shape
    nt = s // tile
    qkv_spec = pl.BlockSpec((None, s, LANES), lambda bi, hp: (bi, 0, hp))
    return pl.pallas_call(
        functools.partial(_fox_prompt_kernel, tile=tile),
        grid=(b, FOX_HEADS // 2),
        in_specs=[qkv_spec, qkv_spec, qkv_spec,
                  pl.BlockSpec((None, None, nt, 2, tile), lambda bi, hp: (bi, hp, 0, 0, 0))],
        out_specs=qkv_spec,
        out_shape=jax.ShapeDtypeStruct((b, s, FOX_WIDTH), BF16),
        scratch_shapes=[pltpu.VMEM((2, s, LANES), BF16), pltpu.VMEM((2, 2, tile, LANES), BF16),
                        pltpu.VMEM((2, tile, tile), F32), pltpu.VMEM((2, tile, tile), F32),
                        pltpu.VMEM((2, tile, tile), F32),
                        pltpu.VMEM((2, tile, LANES), F32), pltpu.VMEM((2, tile, LANES), F32)],
        compiler_params=pltpu.CompilerParams(dimension_semantics=("arbitrary", "arbitrary"),
                                             vmem_limit_bytes=VMEM_LIMIT),
        name="fox_prompt",
    )(q, k, v, c_tiles)


def _fox_sample_kernel(q_ref, kt_ref, vt_ref, knt_ref, vnt_ref, ncc_ref, ncn_ref, o_ref,
                       m_ref, l_ref, acc_ref):
    chunk = pl.program_id(1)
    n = q_ref.shape[0]
    lane = lax.broadcasted_iota(jnp.int32, (1, FOX_WIDTH), 1)

    @pl.when(chunk == 0)
    def _():
        m_ref[...] = jnp.full(m_ref.shape, -jnp.inf, F32)
        l_ref[...] = jnp.zeros(l_ref.shape, F32)
        acc_ref[...] = jnp.zeros(acc_ref.shape, F32)

    q = q_ref[...]
    qbd = jnp.concatenate(
        [jnp.where(lane // FOX_HEAD_DIM == hd, q, jnp.zeros_like(q)) for hd in range(FOX_HEADS)], axis=0)

    def head_rows(nc):
        return jnp.concatenate(
            [jnp.broadcast_to(nc[hd:hd + 1, :], (n, nc.shape[1])) for hd in range(FOX_HEADS)], axis=0)

    def update(s, vt):
        m_prev = m_ref[...]
        m_new = jnp.maximum(m_prev, jnp.max(s, axis=-1, keepdims=True))
        alpha = jnp.exp2(m_prev - m_new)
        p = jnp.exp2(s - m_new[:, 0:1])
        l_ref[...] = alpha * l_ref[...] + jnp.sum(p, axis=-1, keepdims=True)
        acc_ref[...] = alpha[:, 0:1] * acc_ref[...] + _dot_nt(p.astype(BF16), vt)
        m_ref[...] = m_new

    update(_dot(qbd, kt_ref[...].astype(BF16)) + head_rows(ncc_ref[...]), vt_ref[...].astype(BF16))

    @pl.when(chunk == pl.num_programs(1) - 1)
    def _():
        r = lax.broadcasted_iota(jnp.int32, (FOX_HEADS * n, n), 0)
        c = lax.broadcasted_iota(jnp.int32, (FOX_HEADS * n, n), 1)
        sn = _dot(qbd, knt_ref[...].astype(BF16)) + head_rows(ncn_ref[:, 0:n])
        update(jnp.where(c <= r % n, sn, -jnp.inf), vnt_ref[...].astype(BF16))
        o_all = acc_ref[...] / l_ref[:, 0:1]
        o = jnp.zeros((n, FOX_WIDTH), F32)
        for hd in range(FOX_HEADS):
            o = o + jnp.where(lane // FOX_HEAD_DIM == hd, o_all[n * hd:n * (hd + 1), :], 0.0)
        o_ref[...] = o.astype(BF16)


def _fox_sample(l, q, knt, vnt, cache_kt, cache_vt, nc):
    nb, n, _ = q.shape
    past = cache_kt.shape[3]
    ck = min(FOX_SAMPLE_CHUNK, past)
    q_spec = pl.BlockSpec((None, n, FOX_WIDTH), lambda b, c: (b, 0, 0))
    new_spec = pl.BlockSpec((None, FOX_WIDTH, n), lambda b, c: (b, 0, 0))
    cache_spec = pl.BlockSpec((None, None, FOX_WIDTH, ck), lambda b, c: (l, b, 0, c))
    rows = FOX_HEADS * n
    return pl.pallas_call(
        _fox_sample_kernel,
        grid=(nb, past // ck),
        in_specs=[q_spec, cache_spec, cache_spec, new_spec, new_spec,
                  pl.BlockSpec((None, FOX_HEADS, ck), lambda b, c: (b, 0, c)),
                  pl.BlockSpec((None, FOX_HEADS, LANES), lambda b, c: (b, 0, past // LANES))],
        out_specs=q_spec,
        out_shape=jax.ShapeDtypeStruct((nb, n, FOX_WIDTH), BF16),
        scratch_shapes=[pltpu.VMEM((rows, LANES), F32), pltpu.VMEM((rows, LANES), F32),
                        pltpu.VMEM((rows, FOX_WIDTH), F32)],
        compiler_params=pltpu.CompilerParams(dimension_semantics=("arbitrary", "arbitrary"),
                                             vmem_limit_bytes=VMEM_LIMIT),
        name="fox_sample",
    )(q, cache_kt, cache_vt, knt, vnt, nc, nc)


def _mlp_kernel(*refs, tmajor, nb):
    if tmajor:
        (x_ref, oret_ref, ofox_ref, oconv_ref, wout_ref, nw_ref, wg_ref, wu_ref, wd_ref, fcw_ref, hist_ref,
         y_ref, st_ref) = refs
    else:
        (x_ref, oret_ref, ofox_ref, oconv_ref, wout_ref, nw_ref, wg_ref, wu_ref, wd_ref, fcw_ref,
         y_ref, st_ref, carry_ref) = refs
    x1 = (x_ref[...] + _dot(oret_ref[...], wout_ref[0:256, :]) + _dot(ofox_ref[...], wout_ref[256:768, :])
          + _dot(oconv_ref[...], wout_ref[768:1024, :]))
    h2 = _rms(x1, nw_ref[...]).astype(BF16)
    down = None
    for c0, c1 in FF_CHUNKS:
        cols = slice(c0, c1)
        gate_pre = _dot(h2, wg_ref[:, cols])
        if tmajor:
            gate_c, new_hist = _dwconv_tmajor(gate_pre, fcw_ref.at[:, cols], hist_ref[:, cols], nb)
            st_ref[:, cols] = new_hist
        else:
            gate_c = _dwconv_carry(gate_pre, fcw_ref.at[:, cols], carry_ref.at[:, cols], pl.program_id(1) == 0)
            st_ref[:, cols] = carry_ref[6:8, cols]
        up = _dot(h2, wu_ref[:, cols])
        act = (gate_c / (1.0 + jnp.exp(-gate_c)) * up).astype(BF16)
        part = _dot(act, wd_ref[cols, :])
        down = part if down is None else down + part
    y_ref[...] = x1 + down


def _mlp(l, x, oret, ofox, oconv, prm, *, hist=None, nb=0):
    tmajor = hist is not None
    g, r, _ = x.shape
    tm = min(TOKEN_TILE, r)
    grid = (g, r // tm)
    lsel = lambda b, i: (l, 0, 0)
    tok = lambda width: pl.BlockSpec((None, tm, width), lambda b, i: (b, i, 0))
    wspec = lambda rows, cols: pl.BlockSpec((None, rows, cols), lsel, pipeline_mode=pl.Buffered(1))
    in_specs = [tok(D_MODEL), tok(RET_WIDTH), tok(FOX_WIDTH), tok(CONV_DIM),
                wspec(D_MODEL, D_MODEL), pl.BlockSpec((None, 1, D_MODEL), lsel),
                wspec(D_MODEL, D_FF), wspec(D_MODEL, D_FF), wspec(D_FF, D_MODEL),
                pl.BlockSpec((None, 3, D_FF), lsel)]
    args = [x, oret, ofox, oconv, prm["w_out"], prm["norm_ffn"], prm["w_gate"], prm["w_up"], prm["w_down"],
            prm["ffn_conv_w"]]
    if tmajor:
        in_specs.append(pl.BlockSpec((2 * nb, D_FF), lambda b, i: (0, 0)))
        args.append(hist)
        st_shape = jax.ShapeDtypeStruct((2 * nb, D_FF), F32)
        st_spec = pl.BlockSpec((2 * nb, D_FF), lambda b, i: (0, 0))
        scratch = []
    else:
        st_shape = jax.ShapeDtypeStruct((g, 2, D_FF), F32)
        st_spec = pl.BlockSpec((None, 2, D_FF), lambda b, i: (b, 0, 0))
        scratch = [pltpu.VMEM((SUBLANES, D_FF), F32)]
    return pl.pallas_call(
        functools.partial(_mlp_kernel, tmajor=tmajor, nb=nb),
        grid=grid, in_specs=in_specs,
        out_specs=[tok(D_MODEL), st_spec],
        out_shape=[jax.ShapeDtypeStruct((g, r, D_MODEL), F32), st_shape],
        scratch_shapes=scratch,
        compiler_params=pltpu.CompilerParams(dimension_semantics=("arbitrary", "arbitrary"),
                                             vmem_limit_bytes=VMEM_LIMIT),
        name="outproj_mlp",
    )(*args)


def _win_permutation():
    idx = []
    for base in (0, 256):
        for half in range(2):
            for hd in range(RET_HEADS):
                idx += [base + 64 * hd + 32 * half + i for i in range(32)]
    idx += list(range(512, 1024))
    idx += list(range(1024, 2560))
    idx += list(range(2568, 3336))
    idx += list(range(2560, 2568))
    return np.asarray(idx, np.int32)


def _rope_tables(pos):
    half = 32
    inv_freq = ROPE_BASE ** (-jnp.arange(half, dtype=F32) / half)
    ang = pos.astype(F32)[:, None] * inv_freq[None, :]
    return jnp.tile(jnp.cos(ang), (1, RET_HEADS)), jnp.tile(jnp.sin(ang), (1, RET_HEADS))


def _to_bmajor(a, n, nb):
    return a.reshape(n, nb, a.shape[-1]).transpose(1, 0, 2)


def _to_tmajor(a):
    nb, n, c = a.shape
    return a.transpose(1, 0, 2).reshape(1, n * nb, c)


def kernel(x_prompt, x_sample, cache_fox_k, cache_fox_v, cache_fox_logf, state_ret, state_conv, state_ffn_conv,
           norm_mix, w_in, ret_gn_gain, fox_q_gain, fox_k_gain, fox_f_bias, conv_w, w_out, norm_ffn, w_gate,
           w_up, ffn_conv_w, w_down):
    depth = w_in.shape[0]
    b, s, _ = x_prompt.shape
    nb, n, _ = x_sample.shape
    past = cache_fox_k.shape[2]

    w_perm = jnp.take(w_in, _win_permutation(), axis=2)
    w_perm = jnp.pad(w_perm, ((0, 0), (0, 0), (0, PROJ_PAD - w_perm.shape[2]))).astype(BF16)
    prm = {
        "w_in": w_perm,
        "norm_mix": norm_mix.reshape(depth, 1, D_MODEL),
        "gq": fox_q_gain.reshape(depth, 1, FOX_WIDTH),
        "gk": fox_k_gain.reshape(depth, 1, FOX_WIDTH),
        "fb": jnp.pad(fox_f_bias, ((0, 0), (0, LANES - FOX_HEADS))).reshape(depth, 1, LANES),
        "conv_w": conv_w,
        "w_out": w_out.astype(BF16),
        "norm_ffn": norm_ffn.reshape(depth, 1, D_MODEL),
        "w_gate": w_gate.astype(BF16),
        "w_up": w_up.astype(BF16),
        "w_down": w_down.astype(BF16),
        "ffn_conv_w": ffn_conv_w,
    }
    gain = ret_gn_gain.reshape(depth, 1, RET_WIDTH)
    lane = np.arange(FOX_WIDTH)
    bd = jnp.asarray(lane[:, None] // FOX_HEAD_DIM == lane[None, :] // FOX_HEAD_DIM, BF16)

    cos_p, sin_p = _rope_tables(jnp.arange(s, dtype=jnp.int32))
    pos_s = past + jnp.arange(n, dtype=jnp.int32)
    cos_s, sin_s = _rope_tables(jnp.repeat(pos_s, nb))

    ret_blk = min(RET_BLOCK, s)
    ret_rt = min(TOKEN_TILE, s)
    consts_p = _ret_constants(ret_blk, float(ret_blk))
    samp_rows = LANES
    consts_s = _ret_constants(samp_rows, float(n))
    fox_tile = min(FOX_TILE, s)

    cache_k = jnp.transpose(cache_fox_k, (0, 1, 3, 4, 2)).reshape(depth, nb, FOX_WIDTH, past)
    cache_v = jnp.transpose(cache_fox_v, (0, 1, 3, 4, 2)).reshape(depth, nb, FOX_WIDTH, past)
    cache_lf_t = jnp.transpose(cache_fox_logf, (0, 1, 3, 2))
    c_width = -(-(past + n) // LANES) * LANES

    kbuf = jnp.zeros((depth, b, FOX_WIDTH, s), F32)
    vbuf = jnp.zeros((depth, b, FOX_WIDTH, s), F32)
    zero_state = jnp.zeros((b, RET_WIDTH, RET_WIDTH), F32)

    xp = x_prompt
    xs = jnp.transpose(x_sample, (1, 0, 2)).reshape(1, n * nb, D_MODEL)
    p_lf, p_ret, p_conv, p_ffn = [], [], [], []
    s_k, s_v, s_lf, s_ret, s_conv, s_ffn = [], [], [], [], [], []
    for l in range(depth):
        ret_in, fq, kbuf, vbuf, fkb, fvb, lf, oconv, conv_st = _inproj(
            l, xp, prm, cos_p, sin_p, bd, kbuf=kbuf, vbuf=vbuf)
        oret, sbd = _retention(l, ret_in, zero_state, gain, consts_p, ret_blk, ret_rt)
        c_rows = _cumsum_lanes(jnp.transpose(lf, (0, 2, 1)))
        c_tiles = c_rows.reshape(b, FOX_HEADS // 2, 2, s // fox_tile, fox_tile).transpose(0, 1, 3, 2, 4)
        ofox = _fox_prompt(fq, fkb, fvb, c_tiles, fox_tile)
        xp, ffn_st = _mlp(l, xp, oret, ofox, oconv, prm)
        p_lf.append(lf)
        p_ret.append(_bd_to_state(sbd))
        p_conv.append(conv_st)
        p_ffn.append(ffn_st)

        hist_c = jnp.transpose(state_conv[l], (1, 0, 2)).reshape(2 * nb, CONV_DIM)
        ret_in, fq, fk32, fv32, fkb, fvb, lf, oconv, conv_st = _inproj(
            l, xs, prm, cos_s, sin_s, bd, hist=hist_c, nb=nb)
        ret_b = jnp.pad(_to_bmajor(ret_in, n, nb), ((0, 0), (0, samp_rows - n), (0, 0)))
        oret, sbd = _retention(l, ret_b, _state_to_bd(state_ret[l]), gain, consts_s, samp_rows, samp_rows)
        lf_b = _to_bmajor(lf, n, nb)
        lf_all = jnp.concatenate(
            [cache_lf_t[l], jnp.transpose(lf_b, (0, 2, 1)),
             jnp.zeros((nb, FOX_HEADS, c_width - past - n), F32)], axis=2)
        nc = _cumsum_lanes(lf_all)
        k_new = _to_bmajor(fk32, n, nb)
        v_new = _to_bmajor(fv32, n, nb)
        s_k.append(k_new.reshape(nb, n, FOX_HEADS, FOX_HEAD_DIM))
        s_v.append(v_new.reshape(nb, n, FOX_HEADS, FOX_HEAD_DIM))
        ofox = _fox_sample(l, _to_bmajor(fq, n, nb), jnp.transpose(k_new, (0, 2, 1)),
                           jnp.transpose(v_new, (0, 2, 1)), cache_k, cache_v, nc)
        hist_f = jnp.transpose(state_ffn_conv[l], (1, 0, 2)).reshape(2 * nb, D_FF)
        xs, ffn_st = _mlp(l, xs, _to_tmajor(oret[:, :n]), _to_tmajor(ofox), oconv, prm, hist=hist_f, nb=nb)
        s_lf.append(lf_b)
        s_ret.append(_bd_to_state(sbd))
        s_conv.append(conv_st.reshape(2, nb, CONV_DIM).transpose(1, 0, 2))
        s_ffn.append(ffn_st.reshape(2, nb, D_FF).transpose(1, 0, 2))

    y_sample = xs.reshape(n, nb, D_MODEL).transpose(1, 0, 2)
    stk = lambda ts: jnp.stack(ts, axis=0)
    from_fm = lambda a: jnp.transpose(a.reshape(depth, b, FOX_HEADS, FOX_HEAD_DIM, s), (0, 1, 4, 2, 3))
    return (xp, y_sample,
            from_fm(kbuf), from_fm(vbuf),
            stk(p_lf), stk(p_ret), stk(p_conv), stk(p_ffn),
            stk(s_k), stk(s_v), stk(s_lf), stk(s_ret), stk(s_conv), stk(s_ffn))
```

```python
import functools

import numpy as np
import jax
import jax.numpy as jnp
from jax import lax
from jax.experimental import pallas as pl
from jax.experimental.pallas import tpu as pltpu

F32 = jnp.float32
BF16 = jnp.bfloat16

D_MODEL = 1024
RET_HEADS = 4
RET_WIDTH = 256
FOX_HEADS = 8
FOX_HEAD_DIM = 64
FOX_WIDTH = 512
CONV_DIM = 256
D_FF = 2816
PROJ_PAD = 3456
ROPE_BASE = 10000.0
NORM_EPS = 1e-6
LANES = 128
SUBLANES = 8
VMEM_LIMIT = 56 * 1024 * 1024

TOKEN_TILE = 512
RET_BLOCK = 256
FOX_TILE = 512
FOX_SAMPLE_CHUNK = 1024
FOX_VROWS = 80
LOG2E = 1.4426950408889634
FOX_Q_SCALE = FOX_HEAD_DIM ** -0.5 * LOG2E
MXU_DEPTH = 256
FF_CHUNKS = ((0, 5 * MXU_DEPTH), (5 * MXU_DEPTH, D_FF))


def _dot(a, b):
    return jnp.dot(a, b, preferred_element_type=F32)


def _dot_nt(a, b):
    return lax.dot_general(a, b, (((1,), (1,)), ((), ())), preferred_element_type=F32)


def _split2(x):
    hi = x.astype(BF16)
    lo = (x - hi.astype(F32)).astype(BF16)
    return hi, lo


def _group_sum(x, ones_bd):
    hi, lo = _split2(x)
    return _dot(hi, ones_bd) + _dot(lo, ones_bd)


def _rms(x, g):
    ms = jnp.mean(x * x, axis=-1, keepdims=True)
    return x * lax.rsqrt(ms + NORM_EPS) * g


def _dwconv_carry(u, w_ref, carry_ref, first):
    tm = u.shape[0]

    @pl.when(first)
    def _():
        carry_ref[...] = jnp.zeros(carry_ref.shape, F32)

    row = lax.broadcasted_iota(jnp.int32, u.shape, 0)
    c6 = carry_ref[6:7, :]
    c7 = carry_ref[7:8, :]
    um1 = jnp.where(row == 0, c7, pltpu.roll(u, 1, axis=0))
    um2 = jnp.where(row == 0, c6, jnp.where(row == 1, c7, pltpu.roll(u, 2, axis=0)))
    y = w_ref[0:1, :] * um2 + w_ref[1:2, :] * um1 + w_ref[2:3, :] * u
    carry_ref[...] = u[tm - SUBLANES:tm, :]
    return y


def _dwconv_tmajor(u, w_ref, hist, nb):
    tm = u.shape[0]
    full = jnp.concatenate([hist, u], axis=0)
    y = (w_ref[0:1, :] * full[0:tm] + w_ref[1:2, :] * full[nb:nb + tm]
         + w_ref[2:3, :] * full[2 * nb:2 * nb + tm])
    return y, full[tm:tm + 2 * nb]


def _inproj_kernel(*refs, tmajor, nb):
    if tmajor:
        (x_ref, nw_ref, w_ref, cos_ref, sin_ref, gq_ref, gk_ref, fb_ref, cw_ref, bd_ref, hist_ref,
         ret_ref, fq_ref, fk32_ref, fv32_ref, fkb_ref, fvb_ref, lf_ref, oc_ref, st_ref) = refs
    else:
        (x_ref, nw_ref, w_ref, cos_ref, sin_ref, gq_ref, gk_ref, fb_ref, cw_ref, bd_ref,
         _, _, ret_ref, fq_ref, fk32_ref, fv32_ref, fkb_ref, fvb_ref, lf_ref, oc_ref, st_ref,
         carry_ref) = refs
    h = _rms(x_ref[...], nw_ref[...]).astype(BF16)

    a = _dot(h, w_ref[:, 0:1024])
    cos = cos_ref[...]
    sin = sin_ref[...]
    q1, q2, k1, k2 = a[:, 0:128], a[:, 128:256], a[:, 256:384], a[:, 384:512]
    ret_ref[:, 0:128] = (q1 * cos - q2 * sin).astype(BF16)
    ret_ref[:, 128:256] = (q1 * sin + q2 * cos).astype(BF16)
    ret_ref[:, 256:384] = ((k1 * cos - k2 * sin) * 0.125).astype(BF16)
    ret_ref[:, 384:512] = ((k1 * sin + k2 * cos) * 0.125).astype(BF16)
    ret_ref[:, 512:1024] = a[:, 512:1024].astype(BF16)

    f = _dot(h, w_ref[:, 1024:2560])
    fq, fk, fv = f[:, 0:512], f[:, 512:1024], f[:, 1024:1536]
    bd = bd_ref[...]
    inv_d = 1.0 / FOX_HEAD_DIM
    ssq = _group_sum(fq * fq, bd)
    fq_ref[...] = (fq * lax.rsqrt(ssq * inv_d + NORM_EPS) * gq_ref[...] * FOX_Q_SCALE).astype(BF16)
    ssk = _group_sum(fk * fk, bd)
    fkn = fk * lax.rsqrt(ssk * inv_d + NORM_EPS) * gk_ref[...]
    fkb_ref[...] = fkn.astype(BF16)
    if tmajor:
        fk32_ref[...] = fkn
        fv32_ref[...] = fv
        fvb_ref[...] = fv.astype(BF16)
    else:
        fk32_ref[...] = fkn.T
        fvt = fv.T
        fv32_ref[...] = fvt
        fvb_ref[...] = fvt.astype(BF16)

    c = _dot(h, w_ref[:, 2560:3328])
    cb, cc, ch = c[:, 0:256], c[:, 256:512], c[:, 512:768]
    u = cc * ch
    if tmajor:
        y, new_hist = _dwconv_tmajor(u, cw_ref, hist_ref[...], nb)
        st_ref[...] = new_hist
    else:
        y = _dwconv_carry(u, cw_ref, carry_ref, pl.program_id(1) == 0)
        st_ref[...] = carry_ref[6:8, :]
    oc_ref[...] = (cb * y).astype(BF16)

    z = _dot(h, w_ref[:, 3328:3456]) + fb_ref[...]
    ls = jnp.minimum(z, 0.0) - jnp.log1p(jnp.exp(-jnp.abs(z)))
    lf_ref[...] = ls[:, 0:FOX_HEADS]


def _inproj(l, x, prm, cos, sin, bd, *, kbuf=None, vbuf=None, hist=None, nb=0):
    tmajor = hist is not None
    g, r, _ = x.shape
    tm = min(TOKEN_TILE, r)
    grid = (g, r // tm)
    depth = prm["w_in"].shape[0]
    lsel = lambda b, i: (l, 0, 0)
    in_specs = [
        pl.BlockSpec((None, tm, D_MODEL), lambda b, i: (b, i, 0)),
        pl.BlockSpec((None, 1, D_MODEL), lsel),
        pl.BlockSpec((None, D_MODEL, PROJ_PAD), lsel, pipeline_mode=pl.Buffered(1)),
        pl.BlockSpec((tm, LANES), lambda b, i: (i, 0)),
        pl.BlockSpec((tm, LANES), lambda b, i: (i, 0)),
        pl.BlockSpec((None, 1, FOX_WIDTH), lsel),
        pl.BlockSpec((None, 1, FOX_WIDTH), lsel),
        pl.BlockSpec((None, 1, LANES), lsel),
        pl.BlockSpec((None, 3, CONV_DIM), lsel),
        pl.BlockSpec((FOX_WIDTH, FOX_WIDTH), lambda b, i: (0, 0)),
    ]
    args = [x, prm["norm_mix"], prm["w_in"], cos, sin, prm["gq"], prm["gk"], prm["fb"], prm["conv_w"], bd]
    tok = lambda width: pl.BlockSpec((None, tm, width), lambda b, i: (b, i, 0))
    if tmajor:
        in_specs.append(pl.BlockSpec((2 * nb, CONV_DIM), lambda b, i: (0, 0)))
        args.append(hist)
        kv_shape = jax.ShapeDtypeStruct((g, r, FOX_WIDTH), F32)
        kv_spec = tok(FOX_WIDTH)
        vb_shape = jax.ShapeDtypeStruct((g, r, FOX_WIDTH), BF16)
        vb_spec = tok(FOX_WIDTH)
        st_shape = jax.ShapeDtypeStruct((2 * nb, CONV_DIM), F32)
        st_spec = pl.BlockSpec((2 * nb, CONV_DIM), lambda b, i: (0, 0))
        aliases = {}
        scratch = []
    else:
        in_specs += [pl.BlockSpec(memory_space=pl.ANY), pl.BlockSpec(memory_space=pl.ANY)]
        args += [kbuf, vbuf]
        kv_shape = jax.ShapeDtypeStruct((depth, g, FOX_WIDTH, r), F32)
        kv_spec = pl.BlockSpec((None, None, FOX_WIDTH, tm), lambda b, i: (l, b, 0, i))
        vb_shape = jax.ShapeDtypeStruct((g, FOX_WIDTH, r), BF16)
        vb_spec = pl.BlockSpec((None, FOX_WIDTH, tm), lambda b, i: (b, 0, i))
        st_shape = jax.ShapeDtypeStruct((g, 2, CONV_DIM), F32)
        st_spec = pl.BlockSpec((None, 2, CONV_DIM), lambda b, i: (b, 0, 0))
        aliases = {10: 2, 11: 3}
        scratch = [pltpu.VMEM((SUBLANES, CONV_DIM), F32)]
    out_shape = [
        jax.ShapeDtypeStruct((g, r, 1024), BF16),
        jax.ShapeDtypeStruct((g, r, FOX_WIDTH), BF16),
        kv_shape, kv_shape,
        jax.ShapeDtypeStruct((g, r, FOX_WIDTH), BF16),
        vb_shape,
        jax.ShapeDtypeStruct((g, r, FOX_HEADS), F32),
        jax.ShapeDtypeStruct((g, r, CONV_DIM), BF16),
        st_shape,
    ]
    out_specs = [tok(1024), tok(FOX_WIDTH), kv_spec, kv_spec, tok(FOX_WIDTH), vb_spec,
                 tok(FOX_HEADS), tok(CONV_DIM), st_spec]
    return pl.pallas_call(
        functools.partial(_inproj_kernel, tmajor=tmajor, nb=nb),
        grid=grid, in_specs=in_specs, out_specs=out_specs, out_shape=out_shape,
        scratch_shapes=scratch, input_output_aliases=aliases,
        compiler_params=pltpu.CompilerParams(dimension_semantics=("arbitrary", "arbitrary"),
                                             vmem_limit_bytes=VMEM_LIMIT),
        name="inproj",
    )(*args)


def _ret_kernel(x_ref, s0_ref, gain_ref, dec_ref, qs_ref, ks_ref, gbm_ref, bm_ref, bdv_ref,
                o_ref, sout_ref, s_ref, *, blk):
    @pl.when(pl.program_id(1) == 0)
    def _():
        s_ref[...] = s0_ref[...]

    inv_d = 1.0 / 64.0
    bdv = bdv_ref[...]
    lane = lax.broadcasted_iota(jnp.int32, (1, RET_WIDTH), 1)
    head_k = (lane % 128) // 32
    head_v = lane // 64
    for sb in range(x_ref.shape[0] // blk):
        rows = slice(sb * blk, (sb + 1) * blk)
        q = x_ref[rows, 0:256]
        k = x_ref[rows, 256:512]
        v = x_ref[rows, 512:768]
        gate = x_ref[rows, 768:1024].astype(F32)
        state = s_ref[...]
        qw = (q.astype(F32) * qs_ref[...]).astype(BF16)
        o = _dot(qw, state.astype(BF16))
        for hd in range(RET_HEADS):
            qh = jnp.where(head_k == hd, q, jnp.zeros_like(q))
            a = _dot_nt(qh, k) * dec_ref[hd]
            o = o + jnp.where(head_v == hd, _dot(a.astype(BF16), v), 0.0)
        kw_t = (k.astype(F32) * ks_ref[...]).T.astype(BF16)
        s_ref[...] = gbm_ref[...] * state + bm_ref[...] * _dot(kw_t, v)
        mu = _group_sum(o, bdv) * inv_d
        d = o - mu
        var = _group_sum(d * d, bdv) * inv_d
        on = d * lax.rsqrt(var + NORM_EPS) * gain_ref[...]
        o_ref[rows, :] = (on * (gate / (1.0 + jnp.exp(-gate)))).astype(BF16)
    sout_ref[...] = s_ref[...]


def _ret_constants(blk, chunk_len):
    log_g = jnp.log1p(-jnp.exp2(-5.0 - jnp.arange(RET_HEADS, dtype=F32)))
    lane = np.arange(RET_WIDTH)
    head_k = (lane % 128) // 32
    head_v = lane // 64
    i = jnp.arange(blk, dtype=F32)
    diff = i[:, None] - i[None, :]
    dec = jnp.where(diff[None] >= 0.0, jnp.exp(jnp.maximum(diff, 0.0)[None] * log_g[:, None, None]), 0.0)
    lg_k = log_g[head_k]
    qs = jnp.exp((i + 1.0)[:, None] * lg_k[None, :])
    ks = jnp.exp((chunk_len - 1.0 - i)[:, None] * lg_k[None, :])
    gbm = jnp.broadcast_to(jnp.exp(chunk_len * lg_k)[:, None], (RET_WIDTH, RET_WIDTH))
    bm = jnp.asarray(head_k[:, None] == head_v[None, :], F32)
    bdv = jnp.asarray(head_v[:, None] == head_v[None, :], BF16)
    return dec, qs, ks, gbm, bm, bdv


def _retention(l, ret_in, s0, gain, consts, blk, rt):
    g, r, _ = ret_in.shape
    grid = (g, r // rt)
    const2 = lambda shape: pl.BlockSpec(shape, lambda b, i: (0,) * len(shape))
    dec, qs, ks, gbm, bm, bdv = consts
    in_specs = [
        pl.BlockSpec((None, rt, 1024), lambda b, i: (b, i, 0)),
        pl.BlockSpec((None, RET_WIDTH, RET_WIDTH), lambda b, i: (b, 0, 0)),
        pl.BlockSpec((None, 1, RET_WIDTH), lambda b, i: (l, 0, 0)),
        const2(dec.shape), const2(qs.shape), const2(ks.shape), const2(gbm.shape), const2(bm.shape),
        const2(bdv.shape),
    ]
    return pl.pallas_call(
        functools.partial(_ret_kernel, blk=blk),
        grid=grid, in_specs=in_specs,
        out_specs=[pl.BlockSpec((None, rt, RET_WIDTH), lambda b, i: (b, i, 0)),
                   pl.BlockSpec((None, RET_WIDTH, RET_WIDTH), lambda b, i: (b, 0, 0))],
        out_shape=[jax.ShapeDtypeStruct((g, r, RET_WIDTH), BF16),
                   jax.ShapeDtypeStruct((g, RET_WIDTH, RET_WIDTH), F32)],
        scratch_shapes=[pltpu.VMEM((RET_WIDTH, RET_WIDTH), F32)],
        compiler_params=pltpu.CompilerParams(dimension_semantics=("arbitrary", "arbitrary"),
                                             vmem_limit_bytes=VMEM_LIMIT),
        name="retention",
    )(ret_in, s0, gain, dec, qs, ks, gbm, bm, bdv)


def _state_to_bd(st):
    g = st.shape[0]
    t = jnp.einsum("bhpie,hg->bphige", st.reshape(g, RET_HEADS, 2, 32, 64), jnp.eye(RET_HEADS, dtype=st.dtype))
    return t.reshape(g, RET_WIDTH, RET_WIDTH)


def _bd_to_state(sbd):
    g = sbd.shape[0]
    t = jnp.einsum("bphihe->bhpie", sbd.reshape(g, 2, RET_HEADS, 32, RET_HEADS, 64))
    return t.reshape(g, RET_HEADS, 64, 64)


def _cumsum_kernel(x_ref, u_ref, o_ref):
    nc = x_ref.shape[1] // LANES
    xs = jnp.concatenate([x_ref[:, LANES * j:LANES * (j + 1)] for j in range(nc)], axis=0)
    x1 = xs.astype(BF16)
    r1 = xs - x1.astype(F32)
    x2 = r1.astype(BF16)
    x3 = (r1 - x2.astype(F32)).astype(BF16)
    tri = u_ref[...]
    loc = _dot(x1, tri) + _dot(x2, tri) + _dot(x3, tri)
    off = jnp.zeros((SUBLANES, 1), F32)
    for j in range(nc):
        lj = loc[SUBLANES * j:SUBLANES * (j + 1), :]
        o_ref[:, LANES * j:LANES * (j + 1)] = (lj + off) * (-LOG2E)
        off = off + lj[:, LANES - 1:LANES]


def _cumsum_lanes(x):
    n, h, w = x.shape
    tri = jnp.asarray(np.arange(LANES)[:, None] <= np.arange(LANES)[None, :], BF16)
    return pl.pallas_call(
        _cumsum_kernel,
        grid=(n,),
        in_specs=[pl.BlockSpec((None, h, w), lambda b: (b, 0, 0)),
                  pl.BlockSpec((LANES, LANES), lambda b: (0, 0))],
        out_specs=pl.BlockSpec((None, h, w), lambda b: (b, 0, 0)),
        out_shape=jax.ShapeDtypeStruct((n, h, w), F32),
        compiler_params=pltpu.CompilerParams(dimension_semantics=("arbitrary",)),
        name="forget_cumsum",
    )(x, tri)


def _fox_prompt_kernel(q_ref, k_ref, vt_ref, nc_ref, o_ref, va_ref, qh_ref, sd_ref, sa_ref, sb_ref, m_ref,
                       acc_ref, *, tile):
    nq = q_ref.shape[0] // tile
    lane = lax.broadcasted_iota(jnp.int32, (1, LANES), 1)
    krow = lax.broadcasted_iota(jnp.int32, (tile, tile), 0)
    qcol = lax.broadcasted_iota(jnp.int32, (tile, tile), 1)
    for j in range(nq):
        for hd in range(2):
            va_ref[hd, j, 0:FOX_HEAD_DIM, :] = vt_ref[FOX_HEAD_DIM * hd:FOX_HEAD_DIM * (hd + 1),
                                                       tile * j:tile * (j + 1)]
            va_ref[hd, j, FOX_HEAD_DIM:FOX_VROWS, :] = jnp.ones((FOX_VROWS - FOX_HEAD_DIM, tile), BF16)

    def set_q(slot, qi):
        q = q_ref[pl.ds(pl.multiple_of(qi * tile, tile), tile), :]
        for hd in range(2):
            qh_ref[slot, hd] = jnp.where(lane // FOX_HEAD_DIM == hd, q, jnp.zeros_like(q))

    def park(dst_ref, qslot, kj, causal):
        kt = k_ref[pl.ds(pl.multiple_of(kj * tile, tile), tile), :]
        ncol = nc_ref[kj]
        for hd in range(2):
            s = _dot_nt(kt, qh_ref[qslot, hd]) + ncol[:, hd:hd + 1]
            dst_ref[hd] = jnp.where(krow <= qcol, s, -jnp.inf) if causal else s

    def consume(src_ref, kj):
        for hd in range(2):
            s = src_ref[hd]
            m_prev = m_ref[hd]
            m_new = jnp.maximum(m_prev, jnp.max(s, axis=0, keepdims=True))
            alpha = jnp.exp2(m_prev - m_new)
            p = jnp.exp2(s - m_new).astype(BF16)
            acc_ref[hd] = alpha * acc_ref[hd] + _dot(va_ref[hd, kj], p)
            m_ref[hd] = m_new

    def park_next_diag(qi):
        nxt = jnp.minimum(qi + 1, nq - 1)
        set_q(1 - qi % 2, nxt)
        park(sd_ref, 1 - qi % 2, nxt, True)

    set_q(0, 0)
    park(sd_ref, 0, 0, True)

    def q_body(qi, carry):
        qs = qi % 2
        m_ref[...] = jnp.full(m_ref.shape, -jnp.inf, F32)
        acc_ref[...] = jnp.zeros(acc_ref.shape, F32)

        @pl.when(qi == 0)
        def _():
            consume(sd_ref, 0)
            park_next_diag(qi)

        @pl.when(qi > 0)
        def _():
            park(sa_ref, qs, 0, False)
            consume(sd_ref, qi)

            def kv_pair(jp, c2):
                j = 2 * jp + 1
                park(sb_ref, qs, j, False)
                consume(sa_ref, j - 1)
                park(sa_ref, qs, j + 1, False)
                consume(sb_ref, j)
                return c2

            lax.fori_loop(0, (qi - 1) // 2, kv_pair, 0)

            @pl.when(qi % 2 == 0)
            def _():
                park(sb_ref, qs, qi - 1, False)
                consume(sa_ref, qi - 2)
                park_next_diag(qi)
                consume(sb_ref, qi - 1)

            @pl.when(qi % 2 == 1)
            def _():
                park_next_diag(qi)
                consume(sa_ref, qi - 1)

        ot = jnp.concatenate(
            [acc_ref[hd, 0:FOX_HEAD_DIM, :] / acc_ref[hd, FOX_HEAD_DIM:FOX_HEAD_DIM + 1, :] for hd in range(2)],
            axis=0)
        q0 = pl.multiple_of(qi * tile, tile)
        o_ref[pl.ds(q0, tile), :] = ot.T.astype(BF16)
        return carry

    lax.fori_loop(0, nq, q_body, 0)


def _fox_prompt(q, k, vt, nc_cols, tile):
    b, s, _ = q.shape
    nt = s // tile
    qk_spec = pl.BlockSpec((None, s, LANES), lambda bi, hp: (bi, 0, hp))
    return pl.pallas_call(
        functools.partial(_fox_prompt_kernel, tile=tile),
        grid=(b, FOX_HEADS // 2),
        in_specs=[qk_spec, qk_spec, pl.BlockSpec((None, LANES, s), lambda bi, hp: (bi, hp, 0)),
                  pl.BlockSpec((None, None, nt, tile, 2), lambda bi, hp: (bi, hp, 0, 0, 0))],
        out_specs=qk_spec,
        out_shape=jax.ShapeDtypeStruct((b, s, FOX_WIDTH), BF16),
        scratch_shapes=[pltpu.VMEM((2, nt, FOX_VROWS, tile), BF16), pltpu.VMEM((2, 2, tile, LANES), BF16),
                        pltpu.VMEM((2, tile, tile), F32), pltpu.VMEM((2, tile, tile), F32),
                        pltpu.VMEM((2, tile, tile), F32),
                        pltpu.VMEM((2, 1, tile), F32), pltpu.VMEM((2, FOX_VROWS, tile), F32)],
        compiler_params=pltpu.CompilerParams(dimension_semantics=("arbitrary", "arbitrary"),
                                             vmem_limit_bytes=VMEM_LIMIT),
        name="fox_prompt",
    )(q, k, vt, nc_cols)


def _fox_sample_kernel(q_ref, kt_ref, vt_ref, knt_ref, vnt_ref, ncc_ref, ncn_ref, o_ref,
                       m_ref, l_ref, acc_ref):
    chunk = pl.program_id(1)
    n = q_ref.shape[0]
    lane = lax.broadcasted_iota(jnp.int32, (1, FOX_WIDTH), 1)

    @pl.when(chunk == 0)
    def _():
        m_ref[...] = jnp.full(m_ref.shape, -jnp.inf, F32)
        l_ref[...] = jnp.zeros(l_ref.shape, F32)
        acc_ref[...] = jnp.zeros(acc_ref.shape, F32)

    q = q_ref[...]
    qbd = jnp.concatenate(
        [jnp.where(lane // FOX_HEAD_DIM == hd, q, jnp.zeros_like(q)) for hd in range(FOX_HEADS)], axis=0)

    def head_rows(nc):
        return jnp.concatenate(
            [jnp.broadcast_to(nc[hd:hd + 1, :], (n, nc.shape[1])) for hd in range(FOX_HEADS)], axis=0)

    def update(s, vt):
        m_prev = m_ref[...]
        m_new = jnp.maximum(m_prev, jnp.max(s, axis=-1, keepdims=True))
        alpha = jnp.exp2(m_prev - m_new)
        p = jnp.exp2(s - m_new[:, 0:1])
        l_ref[...] = alpha * l_ref[...] + jnp.sum(p, axis=-1, keepdims=True)
        acc_ref[...] = alpha[:, 0:1] * acc_ref[...] + _dot_nt(p.astype(BF16), vt)
        m_ref[...] = m_new

    update(_dot(qbd, kt_ref[...].astype(BF16)) + head_rows(ncc_ref[...]), vt_ref[...].astype(BF16))

    @pl.when(chunk == pl.num_programs(1) - 1)
    def _():
        r = lax.broadcasted_iota(jnp.int32, (FOX_HEADS * n, n), 0)
        c = lax.broadcasted_iota(jnp.int32, (FOX_HEADS * n, n), 1)
        sn = _dot(qbd, knt_ref[...].astype(BF16)) + head_rows(ncn_ref[:, 0:n])
        update(jnp.where(c <= r % n, sn, -jnp.inf), vnt_ref[...].astype(BF16))
        o_all = acc_ref[...] / l_ref[:, 0:1]
        o = jnp.zeros((n, FOX_WIDTH), F32)
        for hd in range(FOX_HEADS):
            o = o + jnp.where(lane // FOX_HEAD_DIM == hd, o_all[n * hd:n * (hd + 1), :], 0.0)
        o_ref[...] = o.astype(BF16)


def _fox_sample(l, q, knt, vnt, cache_kt, cache_vt, nc):
    nb, n, _ = q.shape
    past = cache_kt.shape[3]
    ck = min(FOX_SAMPLE_CHUNK, past)
    q_spec = pl.BlockSpec((None, n, FOX_WIDTH), lambda b, c: (b, 0, 0))
    new_spec = pl.BlockSpec((None, FOX_WIDTH, n), lambda b, c: (b, 0, 0))
    cache_spec = pl.BlockSpec((None, None, FOX_WIDTH, ck), lambda b, c: (l, b, 0, c))
    rows = FOX_HEADS * n
    return pl.pallas_call(
        _fox_sample_kernel,
        grid=(nb, past // ck),
        in_specs=[q_spec, cache_spec, cache_spec, new_spec, new_spec,
                  pl.BlockSpec((None, FOX_HEADS, ck), lambda b, c: (b, 0, c)),
                  pl.BlockSpec((None, FOX_HEADS, LANES), lambda b, c: (b, 0, past // LANES))],
        out_specs=q_spec,
        out_shape=jax.ShapeDtypeStruct((nb, n, FOX_WIDTH), BF16),
        scratch_shapes=[pltpu.VMEM((rows, LANES), F32), pltpu.VMEM((rows, LANES), F32),
                        pltpu.VMEM((rows, FOX_WIDTH), F32)],
        compiler_params=pltpu.CompilerParams(dimension_semantics=("arbitrary", "arbitrary"),
                                             vmem_limit_bytes=VMEM_LIMIT),
        name="fox_sample",
    )(q, cache_kt, cache_vt, knt, vnt, nc, nc)


def _mlp_kernel(*refs, tmajor, nb):
    if tmajor:
        (x_ref, oret_ref, ofox_ref, oconv_ref, wout_ref, nw_ref, wg_ref, wu_ref, wd_ref, fcw_ref, hist_ref,
         y_ref, st_ref) = refs
    else:
        (x_ref, oret_ref, ofox_ref, oconv_ref, wout_ref, nw_ref, wg_ref, wu_ref, wd_ref, fcw_ref,
         y_ref, st_ref, carry_ref) = refs
    x1 = (x_ref[...] + _dot(oret_ref[...], wout_ref[0:256, :]) + _dot(ofox_ref[...], wout_ref[256:768, :])
          + _dot(oconv_ref[...], wout_ref[768:1024, :]))
    h2 = _rms(x1, nw_ref[...]).astype(BF16)
    down = None
    for c0, c1 in FF_CHUNKS:
        cols = slice(c0, c1)
        gate_pre = _dot(h2, wg_ref[:, cols])
        if tmajor:
            gate_c, new_hist = _dwconv_tmajor(gate_pre, fcw_ref.at[:, cols], hist_ref[:, cols], nb)
            st_ref[:, cols] = new_hist
        else:
            gate_c = _dwconv_carry(gate_pre, fcw_ref.at[:, cols], carry_ref.at[:, cols], pl.program_id(1) == 0)
            st_ref[:, cols] = carry_ref[6:8, cols]
        up = _dot(h2, wu_ref[:, cols])
        act = (gate_c / (1.0 + jnp.exp(-gate_c)) * up).astype(BF16)
        part = _dot(act, wd_ref[cols, :])
        down = part if down is None else down + part
    y_ref[...] = x1 + down


def _mlp(l, x, oret, ofox, oconv, prm, *, hist=None, nb=0):
    tmajor = hist is not None
    g, r, _ = x.shape
    tm = min(TOKEN_TILE, r)
    grid = (g, r // tm)
    lsel = lambda b, i: (l, 0, 0)
    tok = lambda width: pl.BlockSpec((None, tm, width), lambda b, i: (b, i, 0))
    wspec = lambda rows, cols: pl.BlockSpec((None, rows, cols), lsel, pipeline_mode=pl.Buffered(1))
    in_specs = [tok(D_MODEL), tok(RET_WIDTH), tok(FOX_WIDTH), tok(CONV_DIM),
                wspec(D_MODEL, D_MODEL), pl.BlockSpec((None, 1, D_MODEL), lsel),
                wspec(D_MODEL, D_FF), wspec(D_MODEL, D_FF), wspec(D_FF, D_MODEL),
                pl.BlockSpec((None, 3, D_FF), lsel)]
    args = [x, oret, ofox, oconv, prm["w_out"], prm["norm_ffn"], prm["w_gate"], prm["w_up"], prm["w_down"],
            prm["ffn_conv_w"]]
    if tmajor:
        in_specs.append(pl.BlockSpec((2 * nb, D_FF), lambda b, i: (0, 0)))
        args.append(hist)
        st_shape = jax.ShapeDtypeStruct((2 * nb, D_FF), F32)
        st_spec = pl.BlockSpec((2 * nb, D_FF), lambda b, i: (0, 0))
        scratch = []
    else:
        st_shape = jax.ShapeDtypeStruct((g, 2, D_FF), F32)
        st_spec = pl.BlockSpec((None, 2, D_FF), lambda b, i: (b, 0, 0))
        scratch = [pltpu.VMEM((SUBLANES, D_FF), F32)]
    return pl.pallas_call(
        functools.partial(_mlp_kernel, tmajor=tmajor, nb=nb),
        grid=grid, in_specs=in_specs,
        out_specs=[tok(D_MODEL), st_spec],
        out_shape=[jax.ShapeDtypeStruct((g, r, D_MODEL), F32), st_shape],
        scratch_shapes=scratch,
        compiler_params=pltpu.CompilerParams(dimension_semantics=("arbitrary", "arbitrary"),
                                             vmem_limit_bytes=VMEM_LIMIT),
        name="outproj_mlp",
    )(*args)


def _win_permutation():
    idx = []
    for base in (0, 256):
        for half in range(2):
            for hd in range(RET_HEADS):
                idx += [base + 64 * hd + 32 * half + i for i in range(32)]
    idx += list(range(512, 1024))
    idx += list(range(1024, 2560))
    idx += list(range(2568, 3336))
    idx += list(range(2560, 2568))
    return np.asarray(idx, np.int32)


def _rope_tables(pos):
    half = 32
    inv_freq = ROPE_BASE ** (-jnp.arange(half, dtype=F32) / half)
    ang = pos.astype(F32)[:, None] * inv_freq[None, :]
    return jnp.tile(jnp.cos(ang), (1, RET_HEADS)), jnp.tile(jnp.sin(ang), (1, RET_HEADS))


def _to_bmajor(a, n, nb):
    return a.reshape(n, nb, a.shape[-1]).transpose(1, 0, 2)


def _to_tmajor(a):
    nb, n, c = a.shape
    return a.transpose(1, 0, 2).reshape(1, n * nb, c)


def kernel(x_prompt, x_sample, cache_fox_k, cache_fox_v, cache_fox_logf, state_ret, state_conv, state_ffn_conv,
           norm_mix, w_in, ret_gn_gain, fox_q_gain, fox_k_gain, fox_f_bias, conv_w, w_out, norm_ffn, w_gate,
           w_up, ffn_conv_w, w_down):
    depth = w_in.shape[0]
    b, s, _ = x_prompt.shape
    nb, n, _ = x_sample.shape
    past = cache_fox_k.shape[2]

    w_perm = jnp.take(w_in, _win_permutation(), axis=2)
    w_perm = jnp.pad(w_perm, ((0, 0), (0, 0), (0, PROJ_PAD - w_perm.shape[2]))).astype(BF16)
    prm = {
        "w_in": w_perm,
        "norm_mix": norm_mix.reshape(depth, 1, D_MODEL),
        "gq": fox_q_gain.reshape(depth, 1, FOX_WIDTH),
        "gk": fox_k_gain.reshape(depth, 1, FOX_WIDTH),
        "fb": jnp.pad(fox_f_bias, ((0, 0), (0, LANES - FOX_HEADS))).reshape(depth, 1, LANES),
        "conv_w": conv_w,
        "w_out": w_out.astype(BF16),
        "norm_ffn": norm_ffn.reshape(depth, 1, D_MODEL),
        "w_gate": w_gate.astype(BF16),
        "w_up": w_up.astype(BF16),
        "w_down": w_down.astype(BF16),
        "ffn_conv_w": ffn_conv_w,
    }
    gain = ret_gn_gain.reshape(depth, 1, RET_WIDTH)
    lane = np.arange(FOX_WIDTH)
    bd = jnp.asarray(lane[:, None] // FOX_HEAD_DIM == lane[None, :] // FOX_HEAD_DIM, BF16)

    cos_p, sin_p = _rope_tables(jnp.arange(s, dtype=jnp.int32))
    pos_s = past + jnp.arange(n, dtype=jnp.int32)
    cos_s, sin_s = _rope_tables(jnp.repeat(pos_s, nb))

    ret_blk = min(RET_BLOCK, s)
    ret_rt = min(TOKEN_TILE, s)
    consts_p = _ret_constants(ret_blk, float(ret_blk))
    samp_rows = LANES
    consts_s = _ret_constants(samp_rows, float(n))
    fox_tile = min(FOX_TILE, s)

    cache_k = jnp.transpose(cache_fox_k, (0, 1, 3, 4, 2)).reshape(depth, nb, FOX_WIDTH, past)
    cache_v = jnp.transpose(cache_fox_v, (0, 1, 3, 4, 2)).reshape(depth, nb, FOX_WIDTH, past)
    cache_lf_t = jnp.transpose(cache_fox_logf, (0, 1, 3, 2))
    c_width = -(-(past + n) // LANES) * LANES

    kbuf = jnp.zeros((depth, b, FOX_WIDTH, s), F32)
    vbuf = jnp.zeros((depth, b, FOX_WIDTH, s), F32)
    zero_state = jnp.zeros((b, RET_WIDTH, RET_WIDTH), F32)

    xp = x_prompt
    xs = jnp.transpose(x_sample, (1, 0, 2)).reshape(1, n * nb, D_MODEL)
    p_lf, p_ret, p_conv, p_ffn = [], [], [], []
    s_k, s_v, s_lf, s_ret, s_conv, s_ffn = [], [], [], [], [], []
    for l in range(depth):
        ret_in, fq, kbuf, vbuf, fkb, fvb, lf, oconv, conv_st = _inproj(
            l, xp, prm, cos_p, sin_p, bd, kbuf=kbuf, vbuf=vbuf)
        oret, sbd = _retention(l, ret_in, zero_state, gain, consts_p, ret_blk, ret_rt)
        c_rows = _cumsum_lanes(jnp.transpose(lf, (0, 2, 1)))
        nc_cols = c_rows.reshape(b, FOX_HEADS // 2, 2, s // fox_tile, fox_tile).transpose(0, 1, 3, 4, 2)
        ofox = _fox_prompt(fq, fkb, fvb, nc_cols, fox_tile)
        xp, ffn_st = _mlp(l, xp, oret, ofox, oconv, prm)
        p_lf.append(lf)
        p_ret.append(_bd_to_state(sbd))
        p_conv.append(conv_st)
        p_ffn.append(ffn_st)

        hist_c = jnp.transpose(state_conv[l], (1, 0, 2)).reshape(2 * nb, CONV_DIM)
        ret_in, fq, fk32, fv32, fkb, fvb, lf, oconv, conv_st = _inproj(
            l, xs, prm, cos_s, sin_s, bd, hist=hist_c, nb=nb)
        ret_b = jnp.pad(_to_bmajor(ret_in, n, nb), ((0, 0), (0, samp_rows - n), (0, 0)))
        oret, sbd = _retention(l, ret_b, _state_to_bd(state_ret[l]), gain, consts_s, samp_rows, samp_rows)
        lf_b = _to_bmajor(lf, n, nb)
        lf_all = jnp.concatenate(
            [cache_lf_t[l], jnp.transpose(lf_b, (0, 2, 1)),
             jnp.zeros((nb, FOX_HEADS, c_width - past - n), F32)], axis=2)
        nc = _cumsum_lanes(lf_all)
        k_new = _to_bmajor(fk32, n, nb)
        v_new = _to_bmajor(fv32, n, nb)
        s_k.append(k_new.reshape(nb, n, FOX_HEADS, FOX_HEAD_DIM))
        s_v.append(v_new.reshape(nb, n, FOX_HEADS, FOX_HEAD_DIM))
        ofox = _fox_sample(l, _to_bmajor(fq, n, nb), jnp.transpose(k_new, (0, 2, 1)),
                           jnp.transpose(v_new, (0, 2, 1)), cache_k, cache_v, nc)
        hist_f = jnp.transpose(state_ffn_conv[l], (1, 0, 2)).reshape(2 * nb, D_FF)
        xs, ffn_st = _mlp(l, xs, _to_tmajor(oret[:, :n]), _to_tmajor(ofox), oconv, prm, hist=hist_f, nb=nb)
        s_lf.append(lf_b)
        s_ret.append(_bd_to_state(sbd))
        s_conv.append(conv_st.reshape(2, nb, CONV_DIM).transpose(1, 0, 2))
        s_ffn.append(ffn_st.reshape(2, nb, D_FF).transpose(1, 0, 2))

    y_sample = xs.reshape(n, nb, D_MODEL).transpose(1, 0, 2)
    stk = lambda ts: jnp.stack(ts, axis=0)
    from_fm = lambda a: jnp.transpose(a.reshape(depth, b, FOX_HEADS, FOX_HEAD_DIM, s), (0, 1, 4, 2, 3))
    return (xp, y_sample,
            from_fm(kbuf), from_fm(vbuf),
            stk(p_lf), stk(p_ret), stk(p_conv), stk(p_ffn),
            stk(s_k), stk(s_v), stk(s_lf), stk(s_ret), stk(s_conv), stk(s_ffn))
```

```python
import functools

import numpy as np
import jax
import jax.numpy as jnp
from jax import lax
from jax.experimental import pallas as pl
from jax.experimental.pallas import tpu as pltpu

F32 = jnp.float32
BF16 = jnp.bfloat16

D_MODEL = 1024
RET_HEADS = 4
RET_WIDTH = 256
FOX_HEADS = 8
FOX_HEAD_DIM = 64
FOX_WIDTH = 512
CONV_DIM = 256
D_FF = 2816
PROJ_PAD = 3456
ROPE_BASE = 10000.0
NORM_EPS = 1e-6
LANES = 128
SUBLANES = 8
VMEM_LIMIT = 56 * 1024 * 1024

TOKEN_TILE = 512
RET_BLOCK = 256
FOX_TILE = 512
FOX_SAMPLE_CHUNK = 1024
FOX_VROWS = 80
LOG2E = 1.4426950408889634
FOX_Q_SCALE = FOX_HEAD_DIM ** -0.5 * LOG2E
MXU_DEPTH = 256
FF_CHUNKS = ((0, D_FF),)


def _dot(a, b):
    return jnp.dot(a, b, preferred_element_type=F32)


def _dot_nt(a, b):
    return lax.dot_general(a, b, (((1,), (1,)), ((), ())), preferred_element_type=F32)


def _split2(x):
    hi = x.astype(BF16)
    lo = (x - hi.astype(F32)).astype(BF16)
    return hi, lo


def _group_sum(x, ones_bd):
    hi, lo = _split2(x)
    return _dot(hi, ones_bd) + _dot(lo, ones_bd)


def _rms(x, g):
    ms = jnp.mean(x * x, axis=-1, keepdims=True)
    return x * lax.rsqrt(ms + NORM_EPS) * g


def _dwconv_carry(u, w_ref, carry_ref, first):
    tm = u.shape[0]

    @pl.when(first)
    def _():
        carry_ref[...] = jnp.zeros(carry_ref.shape, F32)

    row = lax.broadcasted_iota(jnp.int32, u.shape, 0)
    c6 = carry_ref[6:7, :]
    c7 = carry_ref[7:8, :]
    um1 = jnp.where(row == 0, c7, pltpu.roll(u, 1, axis=0))
    um2 = jnp.where(row == 0, c6, jnp.where(row == 1, c7, pltpu.roll(u, 2, axis=0)))
    y = w_ref[0:1, :] * um2 + w_ref[1:2, :] * um1 + w_ref[2:3, :] * u
    carry_ref[...] = u[tm - SUBLANES:tm, :]
    return y


def _dwconv_tmajor(u, w_ref, hist, nb):
    tm = u.shape[0]
    full = jnp.concatenate([hist, u], axis=0)
    y = (w_ref[0:1, :] * full[0:tm] + w_ref[1:2, :] * full[nb:nb + tm]
         + w_ref[2:3, :] * full[2 * nb:2 * nb + tm])
    return y, full[tm:tm + 2 * nb]


def _inproj_kernel(*refs, tmajor, nb):
    if tmajor:
        (x_ref, nw_ref, w_ref, cos_ref, sin_ref, gq_ref, gk_ref, fb_ref, cw_ref, bd_ref, hist_ref,
         ret_ref, fq_ref, fk32_ref, fv32_ref, fkb_ref, fvb_ref, lf_ref, oc_ref, st_ref) = refs
    else:
        (x_ref, nw_ref, w_ref, cos_ref, sin_ref, gq_ref, gk_ref, fb_ref, cw_ref, bd_ref,
         _, _, ret_ref, fq_ref, fk32_ref, fv32_ref, fkb_ref, fvb_ref, lf_ref, oc_ref, st_ref,
         carry_ref) = refs
    h = _rms(x_ref[...], nw_ref[...]).astype(BF16)

    a = _dot(h, w_ref[:, 0:1024])
    cos = cos_ref[...]
    sin = sin_ref[...]
    q1, q2, k1, k2 = a[:, 0:128], a[:, 128:256], a[:, 256:384], a[:, 384:512]
    ret_ref[:, 0:128] = (q1 * cos - q2 * sin).astype(BF16)
    ret_ref[:, 128:256] = (q1 * sin + q2 * cos).astype(BF16)
    ret_ref[:, 256:384] = ((k1 * cos - k2 * sin) * 0.125).astype(BF16)
    ret_ref[:, 384:512] = ((k1 * sin + k2 * cos) * 0.125).astype(BF16)
    ret_ref[:, 512:1024] = a[:, 512:1024].astype(BF16)

    f = _dot(h, w_ref[:, 1024:2560])
    fq, fk, fv = f[:, 0:512], f[:, 512:1024], f[:, 1024:1536]
    bd = bd_ref[...]
    inv_d = 1.0 / FOX_HEAD_DIM
    ssq = _dot((fq * fq).astype(BF16), bd)
    fq_ref[...] = (fq * lax.rsqrt(ssq * inv_d + NORM_EPS) * gq_ref[...] * FOX_Q_SCALE).astype(BF16)
    ssk = _group_sum(fk * fk, bd)
    fkn = fk * lax.rsqrt(ssk * inv_d + NORM_EPS) * gk_ref[...]
    fkb_ref[...] = fkn.astype(BF16)
    if tmajor:
        fk32_ref[...] = fkn
        fv32_ref[...] = fv
        fvb_ref[...] = fv.astype(BF16)
    else:
        fk32_ref[...] = fkn.T
        fvt = fv.T
        fv32_ref[...] = fvt
        fvb_ref[...] = fvt.astype(BF16)

    c = _dot(h, w_ref[:, 2560:3328])
    cb, cc, ch = c[:, 0:256], c[:, 256:512], c[:, 512:768]
    u = cc * ch
    if tmajor:
        y, new_hist = _dwconv_tmajor(u, cw_ref, hist_ref[...], nb)
        st_ref[...] = new_hist
    else:
        y = _dwconv_carry(u, cw_ref, carry_ref, pl.program_id(1) == 0)
        st_ref[...] = carry_ref[6:8, :]
    oc_ref[...] = (cb * y).astype(BF16)

    z = _dot(h, w_ref[:, 3328:3456]) + fb_ref[...]
    ls = jnp.minimum(z, 0.0) - jnp.log1p(jnp.exp(-jnp.abs(z)))
    lf_ref[...] = ls[:, 0:FOX_HEADS]


def _inproj(l, x, prm, cos, sin, bd, *, kbuf=None, vbuf=None, hist=None, nb=0):
    tmajor = hist is not None
    g, r, _ = x.shape
    tm = min(TOKEN_TILE, r)
    grid = (g, r // tm)
    depth = prm["w_in"].shape[0]
    lsel = lambda b, i: (l, 0, 0)
    in_specs = [
        pl.BlockSpec((None, tm, D_MODEL), lambda b, i: (b, i, 0)),
        pl.BlockSpec((None, 1, D_MODEL), lsel),
        pl.BlockSpec((None, D_MODEL, PROJ_PAD), lsel, pipeline_mode=pl.Buffered(1)),
        pl.BlockSpec((tm, LANES), lambda b, i: (i, 0)),
        pl.BlockSpec((tm, LANES), lambda b, i: (i, 0)),
        pl.BlockSpec((None, 1, FOX_WIDTH), lsel),
        pl.BlockSpec((None, 1, FOX_WIDTH), lsel),
        pl.BlockSpec((None, 1, LANES), lsel),
        pl.BlockSpec((None, 3, CONV_DIM), lsel),
        pl.BlockSpec((FOX_WIDTH, FOX_WIDTH), lambda b, i: (0, 0)),
    ]
    args = [x, prm["norm_mix"], prm["w_in"], cos, sin, prm["gq"], prm["gk"], prm["fb"], prm["conv_w"], bd]
    tok = lambda width: pl.BlockSpec((None, tm, width), lambda b, i: (b, i, 0))
    if tmajor:
        in_specs.append(pl.BlockSpec((2 * nb, CONV_DIM), lambda b, i: (0, 0)))
        args.append(hist)
        kv_shape = jax.ShapeDtypeStruct((g, r, FOX_WIDTH), F32)
        kv_spec = tok(FOX_WIDTH)
        vb_shape = jax.ShapeDtypeStruct((g, r, FOX_WIDTH), BF16)
        vb_spec = tok(FOX_WIDTH)
        st_shape = jax.ShapeDtypeStruct((2 * nb, CONV_DIM), F32)
        st_spec = pl.BlockSpec((2 * nb, CONV_DIM), lambda b, i: (0, 0))
        aliases = {}
        scratch = []
    else:
        in_specs += [pl.BlockSpec(memory_space=pl.ANY), pl.BlockSpec(memory_space=pl.ANY)]
        args += [kbuf, vbuf]
        kv_shape = jax.ShapeDtypeStruct((depth, g, FOX_WIDTH, r), F32)
        kv_spec = pl.BlockSpec((None, None, FOX_WIDTH, tm), lambda b, i: (l, b, 0, i))
        vb_shape = jax.ShapeDtypeStruct((g, FOX_WIDTH, r), BF16)
        vb_spec = pl.BlockSpec((None, FOX_WIDTH, tm), lambda b, i: (b, 0, i))
        st_shape = jax.ShapeDtypeStruct((g, 2, CONV_DIM), F32)
        st_spec = pl.BlockSpec((None, 2, CONV_DIM), lambda b, i: (b, 0, 0))
        aliases = {10: 2, 11: 3}
        scratch = [pltpu.VMEM((SUBLANES, CONV_DIM), F32)]
    out_shape = [
        jax.ShapeDtypeStruct((g, r, 1024), BF16),
        jax.ShapeDtypeStruct((g, r, FOX_WIDTH), BF16),
        kv_shape, kv_shape,
        jax.ShapeDtypeStruct((g, r, FOX_WIDTH), BF16),
        vb_shape,
        jax.ShapeDtypeStruct((g, r, FOX_HEADS), F32),
        jax.ShapeDtypeStruct((g, r, CONV_DIM), BF16),
        st_shape,
    ]
    out_specs = [tok(1024), tok(FOX_WIDTH), kv_spec, kv_spec, tok(FOX_WIDTH), vb_spec,
                 tok(FOX_HEADS), tok(CONV_DIM), st_spec]
    return pl.pallas_call(
        functools.partial(_inproj_kernel, tmajor=tmajor, nb=nb),
        grid=grid, in_specs=in_specs, out_specs=out_specs, out_shape=out_shape,
        scratch_shapes=scratch, input_output_aliases=aliases,
        compiler_params=pltpu.CompilerParams(dimension_semantics=("arbitrary", "arbitrary"),
                                             vmem_limit_bytes=VMEM_LIMIT),
        name="inproj",
    )(*args)


def _ret_kernel(x_ref, s0_ref, gain_ref, dec_ref, qs_ref, ks_ref, gbm_ref, bm_ref, bdv_ref,
                o_ref, sout_ref, s_ref, *, blk):
    @pl.when(pl.program_id(1) == 0)
    def _():
        s_ref[...] = s0_ref[...]

    inv_d = 1.0 / 64.0
    bdv = bdv_ref[...]
    lane = lax.broadcasted_iota(jnp.int32, (1, RET_WIDTH), 1)
    head_k = (lane % 128) // 32
    head_v = lane // 64
    for sb in range(x_ref.shape[0] // blk):
        rows = slice(sb * blk, (sb + 1) * blk)
        q = x_ref[rows, 0:256]
        k = x_ref[rows, 256:512]
        v = x_ref[rows, 512:768]
        gate = x_ref[rows, 768:1024].astype(F32)
        state = s_ref[...]
        qw = (q.astype(F32) * qs_ref[...]).astype(BF16)
        o = _dot(qw, state.astype(BF16))
        qst = jnp.concatenate([jnp.where(head_k == hd, q, jnp.zeros_like(q)) for hd in range(RET_HEADS)], axis=0)
        a = (_dot_nt(qst, k) * dec_ref[...]).astype(BF16)
        oa = _dot(a, v)
        for hd in range(RET_HEADS):
            o = o + jnp.where(head_v == hd, oa[hd * blk:(hd + 1) * blk, :], 0.0)
        kw_t = (k.astype(F32) * ks_ref[...]).T.astype(BF16)
        s_ref[...] = gbm_ref[...] * state + bm_ref[...] * _dot(kw_t, v)
        hi, lo = _split2(o)
        mu2 = _dot(jnp.concatenate([hi, lo], axis=0), bdv)
        d = o - (mu2[0:blk] + mu2[blk:2 * blk]) * inv_d
        hi, lo = _split2(d * d)
        var2 = _dot(jnp.concatenate([hi, lo], axis=0), bdv)
        var = (var2[0:blk] + var2[blk:2 * blk]) * inv_d
        on = d * lax.rsqrt(var + NORM_EPS) * gain_ref[...]
        o_ref[rows, :] = (on * (gate / (1.0 + jnp.exp(-gate)))).astype(BF16)
    sout_ref[...] = s_ref[...]


def _ret_constants(blk, chunk_len):
    log_g = jnp.log1p(-jnp.exp2(-5.0 - jnp.arange(RET_HEADS, dtype=F32)))
    lane = np.arange(RET_WIDTH)
    head_k = (lane % 128) // 32
    head_v = lane // 64
    i = jnp.arange(blk, dtype=F32)
    diff = i[:, None] - i[None, :]
    dec = jnp.where(diff[None] >= 0.0, jnp.exp(jnp.maximum(diff, 0.0)[None] * log_g[:, None, None]), 0.0)
    dec = dec.reshape(RET_HEADS * blk, blk)
    lg_k = log_g[head_k]
    qs = jnp.exp((i + 1.0)[:, None] * lg_k[None, :])
    ks = jnp.exp((chunk_len - 1.0 - i)[:, None] * lg_k[None, :])
    gbm = jnp.broadcast_to(jnp.exp(chunk_len * lg_k)[:, None], (RET_WIDTH, RET_WIDTH))
    bm = jnp.asarray(head_k[:, None] == head_v[None, :], F32)
    bdv = jnp.asarray(head_v[:, None] == head_v[None, :], BF16)
    return dec, qs, ks, gbm, bm, bdv


def _retention(l, ret_in, s0, gain, consts, blk, rt):
    g, r, _ = ret_in.shape
    grid = (g, r // rt)
    const2 = lambda shape: pl.BlockSpec(shape, lambda b, i: (0,) * len(shape))
    dec, qs, ks, gbm, bm, bdv = consts
    in_specs = [
        pl.BlockSpec((None, rt, 1024), lambda b, i: (b, i, 0)),
        pl.BlockSpec((None, RET_WIDTH, RET_WIDTH), lambda b, i: (b, 0, 0)),
        pl.BlockSpec((None, 1, RET_WIDTH), lambda b, i: (l, 0, 0)),
        const2(dec.shape), const2(qs.shape), const2(ks.shape), const2(gbm.shape), const2(bm.shape),
        const2(bdv.shape),
    ]
    return pl.pallas_call(
        functools.partial(_ret_kernel, blk=blk),
        grid=grid, in_specs=in_specs,
        out_specs=[pl.BlockSpec((None, rt, RET_WIDTH), lambda b, i: (b, i, 0)),
                   pl.BlockSpec((None, RET_WIDTH, RET_WIDTH), lambda b, i: (b, 0, 0))],
        out_shape=[jax.ShapeDtypeStruct((g, r, RET_WIDTH), BF16),
                   jax.ShapeDtypeStruct((g, RET_WIDTH, RET_WIDTH), F32)],
        scratch_shapes=[pltpu.VMEM((RET_WIDTH, RET_WIDTH), F32)],
        compiler_params=pltpu.CompilerParams(dimension_semantics=("arbitrary", "arbitrary"),
                                             vmem_limit_bytes=VMEM_LIMIT),
        name="retention",
    )(ret_in, s0, gain, dec, qs, ks, gbm, bm, bdv)


def _state_to_bd(st):
    g = st.shape[0]
    t = jnp.einsum("bhpie,hg->bphige", st.reshape(g, RET_HEADS, 2, 32, 64), jnp.eye(RET_HEADS, dtype=st.dtype))
    return t.reshape(g, RET_WIDTH, RET_WIDTH)


def _bd_to_state(sbd):
    g = sbd.shape[0]
    t = jnp.einsum("bphihe->bhpie", sbd.reshape(g, 2, RET_HEADS, 32, RET_HEADS, 64))
    return t.reshape(g, RET_HEADS, 64, 64)


def _cumsum_kernel(x_ref, u_ref, o_ref):
    nc = x_ref.shape[1] // LANES
    xs = jnp.concatenate([x_ref[:, LANES * j:LANES * (j + 1)] for j in range(nc)], axis=0)
    x1 = xs.astype(BF16)
    r1 = xs - x1.astype(F32)
    x2 = r1.astype(BF16)
    x3 = (r1 - x2.astype(F32)).astype(BF16)
    tri = u_ref[...]
    loc = _dot(x1, tri) + _dot(x2, tri) + _dot(x3, tri)
    off = jnp.zeros((SUBLANES, 1), F32)
    for j in range(nc):
        lj = loc[SUBLANES * j:SUBLANES * (j + 1), :]
        o_ref[:, LANES * j:LANES * (j + 1)] = (lj + off) * (-LOG2E)
        off = off + lj[:, LANES - 1:LANES]


def _cumsum_lanes(x):
    n, h, w = x.shape
    tri = jnp.asarray(np.arange(LANES)[:, None] <= np.arange(LANES)[None, :], BF16)
    return pl.pallas_call(
        _cumsum_kernel,
        grid=(n,),
        in_specs=[pl.BlockSpec((None, h, w), lambda b: (b, 0, 0)),
                  pl.BlockSpec((LANES, LANES), lambda b: (0, 0))],
        out_specs=pl.BlockSpec((None, h, w), lambda b: (b, 0, 0)),
        out_shape=jax.ShapeDtypeStruct((n, h, w), F32),
        compiler_params=pltpu.CompilerParams(dimension_semantics=("arbitrary",)),
        name="forget_cumsum",
    )(x, tri)


def _fox_prompt_kernel(q_ref, k_ref, vt_ref, nc_ref, o_ref, va_ref, qh_ref, sd_ref, sa_ref, sb_ref, m_ref,
                       acc_ref, *, tile):
    nq = q_ref.shape[0] // tile
    lane = lax.broadcasted_iota(jnp.int32, (1, LANES), 1)
    krow = lax.broadcasted_iota(jnp.int32, (tile, tile), 0)
    qcol = lax.broadcasted_iota(jnp.int32, (tile, tile), 1)
    for j in range(nq):
        for hd in range(2):
            va_ref[hd, j, 0:FOX_HEAD_DIM, :] = vt_ref[FOX_HEAD_DIM * hd:FOX_HEAD_DIM * (hd + 1),
                                                       tile * j:tile * (j + 1)]
            va_ref[hd, j, FOX_HEAD_DIM:FOX_VROWS, :] = jnp.ones((FOX_VROWS - FOX_HEAD_DIM, tile), BF16)

    def set_q(slot, qi):
        q = q_ref[pl.ds(pl.multiple_of(qi * tile, tile), tile), :]
        for hd in range(2):
            qh_ref[slot, hd] = jnp.where(lane // FOX_HEAD_DIM == hd, q, jnp.zeros_like(q))

    def park(dst_ref, qslot, kj, causal):
        kt = k_ref[pl.ds(pl.multiple_of(kj * tile, tile), tile), :]
        ncol = nc_ref[kj]
        for hd in range(2):
            s = _dot_nt(kt, qh_ref[qslot, hd]) + ncol[:, hd:hd + 1]
            dst_ref[hd] = jnp.where(krow <= qcol, s, -jnp.inf) if causal else s

    def consume(src_ref, kj):
        for hd in range(2):
            s = src_ref[hd]
            m_prev = m_ref[hd]
            m_new = jnp.maximum(m_prev, jnp.max(s, axis=0, keepdims=True))
            alpha = jnp.exp2(m_prev - m_new)
            p = jnp.exp2(s - m_new).astype(BF16)
            acc_ref[hd] = alpha * acc_ref[hd] + _dot(va_ref[hd, kj], p)
            m_ref[hd] = m_new

    def park_next_diag(qi):
        nxt = jnp.minimum(qi + 1, nq - 1)
        set_q(1 - qi % 2, nxt)
        park(sd_ref, 1 - qi % 2, nxt, True)

    set_q(0, 0)
    park(sd_ref, 0, 0, True)

    def q_body(qi, carry):
        qs = qi % 2
        m_ref[...] = jnp.full(m_ref.shape, -jnp.inf, F32)
        acc_ref[...] = jnp.zeros(acc_ref.shape, F32)

        @pl.when(qi == 0)
        def _():
            consume(sd_ref, 0)
            park_next_diag(qi)

        @pl.when(qi > 0)
        def _():
            park(sa_ref, qs, 0, False)
            consume(sd_ref, qi)

            def kv_pair(jp, c2):
                j = 2 * jp + 1
                park(sb_ref, qs, j, False)
                consume(sa_ref, j - 1)
                park(sa_ref, qs, j + 1, False)
                consume(sb_ref, j)
                return c2

            lax.fori_loop(0, (qi - 1) // 2, kv_pair, 0)

            @pl.when(qi % 2 == 0)
            def _():
                park(sb_ref, qs, qi - 1, False)
                consume(sa_ref, qi - 2)
                park_next_diag(qi)
                consume(sb_ref, qi - 1)

            @pl.when(qi % 2 == 1)
            def _():
                park_next_diag(qi)
                consume(sa_ref, qi - 1)

        ot = jnp.concatenate(
            [acc_ref[hd, 0:FOX_HEAD_DIM, :] / acc_ref[hd, FOX_HEAD_DIM:FOX_HEAD_DIM + 1, :] for hd in range(2)],
            axis=0)
        q0 = pl.multiple_of(qi * tile, tile)
        o_ref[pl.ds(q0, tile), :] = ot.T.astype(BF16)
        return carry

    lax.fori_loop(0, nq, q_body, 0)


def _fox_prompt(q, k, vt, nc_cols, tile):
    b, s, _ = q.shape
    nt = s // tile
    qk_spec = pl.BlockSpec((None, s, LANES), lambda bi, hp: (bi, 0, hp))
    return pl.pallas_call(
        functools.partial(_fox_prompt_kernel, tile=tile),
        grid=(b, FOX_HEADS // 2),
        in_specs=[qk_spec, qk_spec, pl.BlockSpec((None, LANES, s), lambda bi, hp: (bi, hp, 0)),
                  pl.BlockSpec((None, None, nt, tile, 2), lambda bi, hp: (bi, hp, 0, 0, 0))],
        out_specs=qk_spec,
        out_shape=jax.ShapeDtypeStruct((b, s, FOX_WIDTH), BF16),
        scratch_shapes=[pltpu.VMEM((2, nt, FOX_VROWS, tile), BF16), pltpu.VMEM((2, 2, tile, LANES), BF16),
                        pltpu.VMEM((2, tile, tile), F32), pltpu.VMEM((2, tile, tile), F32),
                        pltpu.VMEM((2, tile, tile), F32),
                        pltpu.VMEM((2, 1, tile), F32), pltpu.VMEM((2, FOX_VROWS, tile), F32)],
        compiler_params=pltpu.CompilerParams(dimension_semantics=("arbitrary", "arbitrary"),
                                             vmem_limit_bytes=VMEM_LIMIT),
        name="fox_prompt",
    )(q, k, vt, nc_cols)


def _fox_sample_kernel(q_ref, kt_ref, vt_ref, knt_ref, vnt_ref, ncc_ref, ncn_ref, o_ref,
                       m_ref, l_ref, acc_ref):
    chunk = pl.program_id(1)
    n = q_ref.shape[0]
    lane = lax.broadcasted_iota(jnp.int32, (1, FOX_WIDTH), 1)

    @pl.when(chunk == 0)
    def _():
        m_ref[...] = jnp.full(m_ref.shape, -jnp.inf, F32)
        l_ref[...] = jnp.zeros(l_ref.shape, F32)
        acc_ref[...] = jnp.zeros(acc_ref.shape, F32)

    q = q_ref[...]
    qbd = jnp.concatenate(
        [jnp.where(lane // FOX_HEAD_DIM == hd, q, jnp.zeros_like(q)) for hd in range(FOX_HEADS)], axis=0)

    def head_rows(nc):
        return jnp.concatenate(
            [jnp.broadcast_to(nc[hd:hd + 1, :], (n, nc.shape[1])) for hd in range(FOX_HEADS)], axis=0)

    def update(s, vt):
        m_prev = m_ref[...]
        m_new = jnp.maximum(m_prev, jnp.max(s, axis=-1, keepdims=True))
        alpha = jnp.exp2(m_prev - m_new)
        p = jnp.exp2(s - m_new[:, 0:1])
        l_ref[...] = alpha * l_ref[...] + jnp.sum(p, axis=-1, keepdims=True)
        acc_ref[...] = alpha[:, 0:1] * acc_ref[...] + _dot_nt(p.astype(BF16), vt)
        m_ref[...] = m_new

    update(_dot(qbd, kt_ref[...].astype(BF16)) + head_rows(ncc_ref[...]), vt_ref[...].astype(BF16))

    @pl.when(chunk == pl.num_programs(1) - 1)
    def _():
        r = lax.broadcasted_iota(jnp.int32, (FOX_HEADS * n, n), 0)
        c = lax.broadcasted_iota(jnp.int32, (FOX_HEADS * n, n), 1)
        sn = _dot(qbd, knt_ref[...].astype(BF16)) + head_rows(ncn_ref[:, 0:n])
        update(jnp.where(c <= r % n, sn, -jnp.inf), vnt_ref[...].astype(BF16))
        o_all = acc_ref[...] / l_ref[:, 0:1]
        o = jnp.zeros((n, FOX_WIDTH), F32)
        for hd in range(FOX_HEADS):
            o = o + jnp.where(lane // FOX_HEAD_DIM == hd, o_all[n * hd:n * (hd + 1), :], 0.0)
        o_ref[...] = o.astype(BF16)


def _fox_sample(l, q, knt, vnt, cache_kt, cache_vt, nc):
    nb, n, _ = q.shape
    past = cache_kt.shape[3]
    ck = min(FOX_SAMPLE_CHUNK, past)
    q_spec = pl.BlockSpec((None, n, FOX_WIDTH), lambda b, c: (b, 0, 0))
    new_spec = pl.BlockSpec((None, FOX_WIDTH, n), lambda b, c: (b, 0, 0))
    cache_spec = pl.BlockSpec((None, None, FOX_WIDTH, ck), lambda b, c: (l, b, 0, c))
    rows = FOX_HEADS * n
    return pl.pallas_call(
        _fox_sample_kernel,
        grid=(nb, past // ck),
        in_specs=[q_spec, cache_spec, cache_spec, new_spec, new_spec,
                  pl.BlockSpec((None, FOX_HEADS, ck), lambda b, c: (b, 0, c)),
                  pl.BlockSpec((None, FOX_HEADS, LANES), lambda b, c: (b, 0, past // LANES))],
        out_specs=q_spec,
        out_shape=jax.ShapeDtypeStruct((nb, n, FOX_WIDTH), BF16),
        scratch_shapes=[pltpu.VMEM((rows, LANES), F32), pltpu.VMEM((rows, LANES), F32),
                        pltpu.VMEM((rows, FOX_WIDTH), F32)],
        compiler_params=pltpu.CompilerParams(dimension_semantics=("arbitrary", "arbitrary"),
                                             vmem_limit_bytes=VMEM_LIMIT),
        name="fox_sample",
    )(q, cache_kt, cache_vt, knt, vnt, nc, nc)


def _mlp_kernel(*refs, tmajor, nb):
    if tmajor:
        (x_ref, oret_ref, ofox_ref, oconv_ref, wout_ref, nw_ref, wg_ref, wu_ref, wd_ref, fcw_ref, hist_ref,
         y_ref, st_ref) = refs
    else:
        (x_ref, oret_ref, ofox_ref, oconv_ref, wout_ref, nw_ref, wg_ref, wu_ref, wd_ref, fcw_ref,
         y_ref, st_ref, carry_ref) = refs
    x1 = (x_ref[...] + _dot(oret_ref[...], wout_ref[0:256, :]) + _dot(ofox_ref[...], wout_ref[256:768, :])
          + _dot(oconv_ref[...], wout_ref[768:1024, :]))
    h2 = _rms(x1, nw_ref[...]).astype(BF16)
    down = None
    for c0, c1 in FF_CHUNKS:
        cols = slice(c0, c1)
        gate_pre = _dot(h2, wg_ref[:, cols])
        if tmajor:
            gate_c, new_hist = _dwconv_tmajor(gate_pre, fcw_ref.at[:, cols], hist_ref[:, cols], nb)
            st_ref[:, cols] = new_hist
        else:
            gate_c = _dwconv_carry(gate_pre, fcw_ref.at[:, cols], carry_ref.at[:, cols], pl.program_id(1) == 0)
            st_ref[:, cols] = carry_ref[6:8, cols]
        up = _dot(h2, wu_ref[:, cols])
        act = (gate_c / (1.0 + jnp.exp(-gate_c)) * up).astype(BF16)
        part = _dot(act, wd_ref[cols, :])
        down = part if down is None else down + part
    y_ref[...] = x1 + down


def _mlp(l, x, oret, ofox, oconv, prm, *, hist=None, nb=0):
    tmajor = hist is not None
    g, r, _ = x.shape
    tm = min(TOKEN_TILE, r)
    grid = (g, r // tm)
    lsel = lambda b, i: (l, 0, 0)
    tok = lambda width: pl.BlockSpec((None, tm, width), lambda b, i: (b, i, 0))
    wspec = lambda rows, cols: pl.BlockSpec((None, rows, cols), lsel, pipeline_mode=pl.Buffered(1))
    in_specs = [tok(D_MODEL), tok(RET_WIDTH), tok(FOX_WIDTH), tok(CONV_DIM),
                wspec(D_MODEL, D_MODEL), pl.BlockSpec((None, 1, D_MODEL), lsel),
                wspec(D_MODEL, D_FF), wspec(D_MODEL, D_FF), wspec(D_FF, D_MODEL),
                pl.BlockSpec((None, 3, D_FF), lsel)]
    args = [x, oret, ofox, oconv, prm["w_out"], prm["norm_ffn"], prm["w_gate"], prm["w_up"], prm["w_down"],
            prm["ffn_conv_w"]]
    if tmajor:
        in_specs.append(pl.BlockSpec((2 * nb, D_FF), lambda b, i: (0, 0)))
        args.append(hist)
        st_shape = jax.ShapeDtypeStruct((2 * nb, D_FF), F32)
        st_spec = pl.BlockSpec((2 * nb, D_FF), lambda b, i: (0, 0))
        scratch = []
    else:
        st_shape = jax.ShapeDtypeStruct((g, 2, D_FF), F32)
        st_spec = pl.BlockSpec((None, 2, D_FF), lambda b, i: (b, 0, 0))
        scratch = [pltpu.VMEM((SUBLANES, D_FF), F32)]
    return pl.pallas_call(
        functools.partial(_mlp_kernel, tmajor=tmajor, nb=nb),
        grid=grid, in_specs=in_specs,
        out_specs=[tok(D_MODEL), st_spec],
        out_shape=[jax.ShapeDtypeStruct((g, r, D_MODEL), F32), st_shape],
        scratch_shapes=scratch,
        compiler_params=pltpu.CompilerParams(dimension_semantics=("arbitrary", "arbitrary"),
                                             vmem_limit_bytes=VMEM_LIMIT),
        name="outproj_mlp",
    )(*args)


def _win_permutation():
    idx = []
    for base in (0, 256):
        for half in range(2):
            for hd in range(RET_HEADS):
                idx += [base + 64 * hd + 32 * half + i for i in range(32)]
    idx += list(range(512, 1024))
    idx += list(range(1024, 2560))
    idx += list(range(2568, 3336))
    idx += list(range(2560, 2568))
    return np.asarray(idx, np.int32)


def _rope_tables(pos):
    half = 32
    inv_freq = ROPE_BASE ** (-jnp.arange(half, dtype=F32) / half)
    ang = pos.astype(F32)[:, None] * inv_freq[None, :]
    return jnp.tile(jnp.cos(ang), (1, RET_HEADS)), jnp.tile(jnp.sin(ang), (1, RET_HEADS))


def _to_bmajor(a, n, nb):
    return a.reshape(n, nb, a.shape[-1]).transpose(1, 0, 2)


def _to_tmajor(a):
    nb, n, c = a.shape
    return a.transpose(1, 0, 2).reshape(1, n * nb, c)


def kernel(x_prompt, x_sample, cache_fox_k, cache_fox_v, cache_fox_logf, state_ret, state_conv, state_ffn_conv,
           norm_mix, w_in, ret_gn_gain, fox_q_gain, fox_k_gain, fox_f_bias, conv_w, w_out, norm_ffn, w_gate,
           w_up, ffn_conv_w, w_down):
    depth = w_in.shape[0]
    b, s, _ = x_prompt.shape
    nb, n, _ = x_sample.shape
    past = cache_fox_k.shape[2]

    w_perm = jnp.take(w_in, _win_permutation(), axis=2)
    w_perm = jnp.pad(w_perm, ((0, 0), (0, 0), (0, PROJ_PAD - w_perm.shape[2]))).astype(BF16)
    prm = {
        "w_in": w_perm,
        "norm_mix": norm_mix.reshape(depth, 1, D_MODEL),
        "gq": fox_q_gain.reshape(depth, 1, FOX_WIDTH),
        "gk": fox_k_gain.reshape(depth, 1, FOX_WIDTH),
        "fb": jnp.pad(fox_f_bias, ((0, 0), (0, LANES - FOX_HEADS))).reshape(depth, 1, LANES),
        "conv_w": conv_w,
        "w_out": w_out.astype(BF16),
        "norm_ffn": norm_ffn.reshape(depth, 1, D_MODEL),
        "w_gate": w_gate.astype(BF16),
        "w_up": w_up.astype(BF16),
        "w_down": w_down.astype(BF16),
        "ffn_conv_w": ffn_conv_w,
    }
    gain = ret_gn_gain.reshape(depth, 1, RET_WIDTH)
    lane = np.arange(FOX_WIDTH)
    bd = jnp.asarray(lane[:, None] // FOX_HEAD_DIM == lane[None, :] // FOX_HEAD_DIM, BF16)

    cos_p, sin_p = _rope_tables(jnp.arange(s, dtype=jnp.int32))
    pos_s = past + jnp.arange(n, dtype=jnp.int32)
    cos_s, sin_s = _rope_tables(jnp.repeat(pos_s, nb))

    ret_blk = min(RET_BLOCK, s)
    ret_rt = min(TOKEN_TILE, s)
    consts_p = _ret_constants(ret_blk, float(ret_blk))
    samp_rows = LANES
    consts_s = _ret_constants(samp_rows, float(n))
    fox_tile = min(FOX_TILE, s)

    cache_k = jnp.transpose(cache_fox_k, (0, 1, 3, 4, 2)).reshape(depth, nb, FOX_WIDTH, past)
    cache_v = jnp.transpose(cache_fox_v, (0, 1, 3, 4, 2)).reshape(depth, nb, FOX_WIDTH, past)
    cache_lf_t = jnp.transpose(cache_fox_logf, (0, 1, 3, 2))
    c_width = -(-(past + n) // LANES) * LANES

    kbuf = jnp.zeros((depth, b, FOX_WIDTH, s), F32)
    vbuf = jnp.zeros((depth, b, FOX_WIDTH, s), F32)
    zero_state = jnp.zeros((b, RET_WIDTH, RET_WIDTH), F32)

    xp = x_prompt
    xs = jnp.transpose(x_sample, (1, 0, 2)).reshape(1, n * nb, D_MODEL)
    p_lf, p_ret, p_conv, p_ffn = [], [], [], []
    s_k, s_v, s_lf, s_ret, s_conv, s_ffn = [], [], [], [], [], []
    for l in range(depth):
        ret_in, fq, kbuf, vbuf, fkb, fvb, lf, oconv, conv_st = _inproj(
            l, xp, prm, cos_p, sin_p, bd, kbuf=kbuf, vbuf=vbuf)
        oret, sbd = _retention(l, ret_in, zero_state, gain, consts_p, ret_blk, ret_rt)
        c_rows = _cumsum_lanes(jnp.transpose(lf, (0, 2, 1)))
        nc_cols = c_rows.reshape(b, FOX_HEADS // 2, 2, s // fox_tile, fox_tile).transpose(0, 1, 3, 4, 2)
        ofox = _fox_prompt(fq, fkb, fvb, nc_cols, fox_tile)
        xp, ffn_st = _mlp(l, xp, oret, ofox, oconv, prm)
        p_lf.append(lf)
        p_ret.append(_bd_to_state(sbd))
        p_conv.append(conv_st)
        p_ffn.append(ffn_st)

        hist_c = jnp.transpose(state_conv[l], (1, 0, 2)).reshape(2 * nb, CONV_DIM)
        ret_in, fq, fk32, fv32, fkb, fvb, lf, oconv, conv_st = _inproj(
            l, xs, prm, cos_s, sin_s, bd, hist=hist_c, nb=nb)
        ret_b = jnp.pad(_to_bmajor(ret_in, n, nb), ((0, 0), (0, samp_rows - n), (0, 0)))
        oret, sbd = _retention(l, ret_b, _state_to_bd(state_ret[l]), gain, consts_s, samp_rows, samp_rows)
        lf_b = _to_bmajor(lf, n, nb)
        lf_all = jnp.concatenate(
            [cache_lf_t[l], jnp.transpose(lf_b, (0, 2, 1)),
             jnp.zeros((nb, FOX_HEADS, c_width - past - n), F32)], axis=2)
        nc = _cumsum_lanes(lf_all)
        k_new = _to_bmajor(fk32, n, nb)
        v_new = _to_bmajor(fv32, n, nb)
        s_k.append(k_new.reshape(nb, n, FOX_HEADS, FOX_HEAD_DIM))
        s_v.append(v_new.reshape(nb, n, FOX_HEADS, FOX_HEAD_DIM))
        ofox = _fox_sample(l, _to_bmajor(fq, n, nb), jnp.transpose(k_new, (0, 2, 1)),
                           jnp.transpose(v_new, (0, 2, 1)), cache_k, cache_v, nc)
        hist_f = jnp.transpose(state_ffn_conv[l], (1, 0, 2)).reshape(2 * nb, D_FF)
        xs, ffn_st = _mlp(l, xs, _to_tmajor(oret[:, :n]), _to_tmajor(ofox), oconv, prm, hist=hist_f, nb=nb)
        s_lf.append(lf_b)
        s_ret.append(_bd_to_state(sbd))
        s_conv.append(conv_st.reshape(2, nb, CONV_DIM).transpose(1, 0, 2))
        s_ffn.append(ffn_st.reshape(2, nb, D_FF).transpose(1, 0, 2))

    y_sample = xs.reshape(n, nb, D_MODEL).transpose(1, 0, 2)
    stk = lambda ts: jnp.stack(ts, axis=0)
    from_fm = lambda a: jnp.transpose(a.reshape(depth, b, FOX_HEADS, FOX_HEAD_DIM, s), (0, 1, 4, 2, 3))
    return (xp, y_sample,
            from_fm(kbuf), from_fm(vbuf),
            stk(p_lf), stk(p_ret), stk(p_conv), stk(p_ffn),
            stk(s_k), stk(s_v), stk(s_lf), stk(s_ret), stk(s_conv), stk(s_ffn))
```

```python
import functools

import numpy as np
import jax
import jax.numpy as jnp
from jax import lax
from jax.experimental import pallas as pl
from jax.experimental.pallas import tpu as pltpu

F32 = jnp.float32
BF16 = jnp.bfloat16

D_MODEL = 1024
RET_HEADS = 4
RET_WIDTH = 256
FOX_HEADS = 8
FOX_HEAD_DIM = 64
FOX_WIDTH = 512
CONV_DIM = 256
D_FF = 2816
PROJ_PAD = 3456
ROPE_BASE = 10000.0
NORM_EPS = 1e-6
LANES = 128
SUBLANES = 8
VMEM_LIMIT = 56 * 1024 * 1024

TOKEN_TILE = 512
RET_BLOCK = 256
MLP_ROW_BLOCK = 256
FOX_TILE = 512
FOX_SAMPLE_CHUNK = 2048
FOX_VROWS = 80
LOG2E = 1.4426950408889634
FOX_Q_SCALE = FOX_HEAD_DIM ** -0.5 * LOG2E
MXU_DEPTH = 256
FF_CHUNKS = ((0, D_FF),)


def _dot(a, b):
    return jnp.dot(a, b, preferred_element_type=F32)


def _dot_nt(a, b):
    return lax.dot_general(a, b, (((1,), (1,)), ((), ())), preferred_element_type=F32)


def _split2(x):
    hi = x.astype(BF16)
    lo = (x - hi.astype(F32)).astype(BF16)
    return hi, lo


def _group_sum(x, ones_bd):
    hi, lo = _split2(x)
    return _dot(hi, ones_bd) + _dot(lo, ones_bd)


def _rms(x, g):
    ms = jnp.mean(x * x, axis=-1, keepdims=True)
    return x * lax.rsqrt(ms + NORM_EPS) * g


def _dwconv_carry(u, w_ref, carry_ref, first):
    tm = u.shape[0]

    if first is not None:
        @pl.when(first)
        def _():
            carry_ref[...] = jnp.zeros(carry_ref.shape, F32)

    row = lax.broadcasted_iota(jnp.int32, u.shape, 0)
    c6 = carry_ref[6:7, :]
    c7 = carry_ref[7:8, :]
    um1 = jnp.where(row == 0, c7, pltpu.roll(u, 1, axis=0))
    um2 = jnp.where(row == 0, c6, jnp.where(row == 1, c7, pltpu.roll(u, 2, axis=0)))
    y = w_ref[0:1, :] * um2 + w_ref[1:2, :] * um1 + w_ref[2:3, :] * u
    carry_ref[...] = u[tm - SUBLANES:tm, :]
    return y


def _dwconv_tmajor(u, w_ref, hist, nb):
    tm = u.shape[0]
    full = jnp.concatenate([hist, u], axis=0)
    y = (w_ref[0:1, :] * full[0:tm] + w_ref[1:2, :] * full[nb:nb + tm]
         + w_ref[2:3, :] * full[2 * nb:2 * nb + tm])
    return y, full[tm:tm + 2 * nb]


def _inproj_kernel(*refs, tmajor, nb):
    if tmajor:
        (x_ref, nw_ref, w_ref, cos_ref, sin_ref, gq_ref, gk_ref, fb_ref, cw_ref, bd_ref, hist_ref,
         ret_ref, fq_ref, fk32_ref, fv32_ref, fkb_ref, fvb_ref, lf_ref, lft_ref, oc_ref, st_ref) = refs
    else:
        (x_ref, nw_ref, w_ref, cos_ref, sin_ref, gq_ref, gk_ref, fb_ref, cw_ref, bd_ref,
         _, _, ret_ref, fq_ref, fk32_ref, fv32_ref, fkb_ref, fvb_ref, lf_ref, lft_ref, oc_ref, st_ref,
         carry_ref) = refs
    h = _rms(x_ref[...], nw_ref[...]).astype(BF16)

    a = _dot(h, w_ref[:, 0:1024])
    f = _dot(h, w_ref[:, 1024:2560])
    c = _dot(h, w_ref[:, 2560:3328])
    z = _dot(h, w_ref[:, 3328:3456]) + fb_ref[...]

    cos = cos_ref[...]
    sin = sin_ref[...]
    q1, q2, k1, k2 = a[:, 0:128], a[:, 128:256], a[:, 256:384], a[:, 384:512]
    ret_ref[:, 0:128] = (q1 * cos - q2 * sin).astype(BF16)
    ret_ref[:, 128:256] = (q1 * sin + q2 * cos).astype(BF16)
    ret_ref[:, 256:384] = ((k1 * cos - k2 * sin) * 0.125).astype(BF16)
    ret_ref[:, 384:512] = ((k1 * sin + k2 * cos) * 0.125).astype(BF16)
    ret_ref[:, 512:1024] = a[:, 512:1024].astype(BF16)

    fq, fk, fv = f[:, 0:512], f[:, 512:1024], f[:, 1024:1536]
    bd = bd_ref[...]
    inv_d = 1.0 / FOX_HEAD_DIM
    ssq = _dot((fq * fq).astype(BF16), bd)
    fq_ref[...] = (fq * lax.rsqrt(ssq * inv_d + NORM_EPS) * gq_ref[...] * FOX_Q_SCALE).astype(BF16)
    ssk = _group_sum(fk * fk, bd)
    fkn = fk * lax.rsqrt(ssk * inv_d + NORM_EPS) * gk_ref[...]
    fkb_ref[...] = fkn.astype(BF16)
    if tmajor:
        fk32_ref[...] = fkn
        fv32_ref[...] = fv
        fvb_ref[...] = fv.astype(BF16)
    else:
        fk32_ref[...] = fkn.T
        fvt = fv.T
        fv32_ref[...] = fvt
        fvb_ref[...] = fvt.astype(BF16)

    cb, cc, ch = c[:, 0:256], c[:, 256:512], c[:, 512:768]
    u = cc * ch
    if tmajor:
        y, new_hist = _dwconv_tmajor(u, cw_ref, hist_ref[...], nb)
        st_ref[...] = new_hist
    else:
        y = _dwconv_carry(u, cw_ref, carry_ref, pl.program_id(1) == 0)
        st_ref[...] = carry_ref[6:8, :]
    oc_ref[...] = (cb * y).astype(BF16)

    ls = jnp.minimum(z, 0.0) - jnp.log1p(jnp.exp(-jnp.abs(z)))
    lf_ref[...] = ls[:, 0:FOX_HEADS]
    lft_ref[...] = ls.T[0:FOX_HEADS, :]


def _inproj(l, x, prm, cos, sin, bd, *, kbuf=None, vbuf=None, hist=None, nb=0):
    tmajor = hist is not None
    g, r, _ = x.shape
    tm = min(TOKEN_TILE, r)
    grid = (g, r // tm)
    depth = prm["w_in"].shape[0]
    lsel = lambda b, i: (l, 0, 0)
    in_specs = [
        pl.BlockSpec((None, tm, D_MODEL), lambda b, i: (b, i, 0)),
        pl.BlockSpec((None, 1, D_MODEL), lsel),
        pl.BlockSpec((None, D_MODEL, PROJ_PAD), lsel, pipeline_mode=pl.Buffered(1)),
        pl.BlockSpec((tm, LANES), lambda b, i: (i, 0)),
        pl.BlockSpec((tm, LANES), lambda b, i: (i, 0)),
        pl.BlockSpec((None, 1, FOX_WIDTH), lsel),
        pl.BlockSpec((None, 1, FOX_WIDTH), lsel),
        pl.BlockSpec((None, 1, LANES), lsel),
        pl.BlockSpec((None, 3, CONV_DIM), lsel),
        pl.BlockSpec((FOX_WIDTH, FOX_WIDTH), lambda b, i: (0, 0)),
    ]
    args = [x, prm["norm_mix"], prm["w_in"], cos, sin, prm["gq"], prm["gk"], prm["fb"], prm["conv_w"], bd]
    tok = lambda width: pl.BlockSpec((None, tm, width), lambda b, i: (b, i, 0))
    if tmajor:
        in_specs.append(pl.BlockSpec((2 * nb, CONV_DIM), lambda b, i: (0, 0)))
        args.append(hist)
        kv_shape = jax.ShapeDtypeStruct((g, r, FOX_WIDTH), F32)
        kv_spec = tok(FOX_WIDTH)
        vb_shape = jax.ShapeDtypeStruct((g, r, FOX_WIDTH), BF16)
        vb_spec = tok(FOX_WIDTH)
        st_shape = jax.ShapeDtypeStruct((2 * nb, CONV_DIM), F32)
        st_spec = pl.BlockSpec((2 * nb, CONV_DIM), lambda b, i: (0, 0))
        aliases = {}
        scratch = []
    else:
        in_specs += [pl.BlockSpec(memory_space=pl.ANY), pl.BlockSpec(memory_space=pl.ANY)]
        args += [kbuf, vbuf]
        kv_shape = jax.ShapeDtypeStruct((depth, g, FOX_WIDTH, r), F32)
        kv_spec = pl.BlockSpec((None, None, FOX_WIDTH, tm), lambda b, i: (l, b, 0, i))
        vb_shape = jax.ShapeDtypeStruct((g, FOX_WIDTH, r), BF16)
        vb_spec = pl.BlockSpec((None, FOX_WIDTH, tm), lambda b, i: (b, 0, i))
        st_shape = jax.ShapeDtypeStruct((g, 2, CONV_DIM), F32)
        st_spec = pl.BlockSpec((None, 2, CONV_DIM), lambda b, i: (b, 0, 0))
        aliases = {10: 2, 11: 3}
        scratch = [pltpu.VMEM((SUBLANES, CONV_DIM), F32)]
    out_shape = [
        jax.ShapeDtypeStruct((g, r, 1024), BF16),
        jax.ShapeDtypeStruct((g, r, FOX_WIDTH), BF16),
        kv_shape, kv_shape,
        jax.ShapeDtypeStruct((g, r, FOX_WIDTH), BF16),
        vb_shape,
        jax.ShapeDtypeStruct((g, r, FOX_HEADS), F32),
        jax.ShapeDtypeStruct((g, FOX_HEADS, r), F32),
        jax.ShapeDtypeStruct((g, r, CONV_DIM), BF16),
        st_shape,
    ]
    out_specs = [tok(1024), tok(FOX_WIDTH), kv_spec, kv_spec, tok(FOX_WIDTH), vb_spec,
                 tok(FOX_HEADS), pl.BlockSpec((None, FOX_HEADS, tm), lambda b, i: (b, 0, i)),
                 tok(CONV_DIM), st_spec]
    return pl.pallas_call(
        functools.partial(_inproj_kernel, tmajor=tmajor, nb=nb),
        grid=grid, in_specs=in_specs, out_specs=out_specs, out_shape=out_shape,
        scratch_shapes=scratch, input_output_aliases=aliases,
        compiler_params=pltpu.CompilerParams(dimension_semantics=("arbitrary", "arbitrary"),
                                             vmem_limit_bytes=VMEM_LIMIT),
        name="inproj",
    )(*args)


def _ret_kernel(x_ref, s0_ref, gain_ref, dec_ref, qs_ref, ks_ref, gbm_ref, bm_ref, bdv_ref,
                o_ref, sout_ref, s_ref, *, blk):
    @pl.when(pl.program_id(1) == 0)
    def _():
        s_ref[...] = s0_ref[...]

    inv_d = 1.0 / 64.0
    bdv = bdv_ref[...]
    lane = lax.broadcasted_iota(jnp.int32, (1, RET_WIDTH), 1)
    head_k = (lane % 128) // 32
    head_v = lane // 64
    nsb = x_ref.shape[0] // blk
    intra = []
    for sb in range(nsb):
        rows = slice(sb * blk, (sb + 1) * blk)
        q = x_ref[rows, 0:256]
        qst = jnp.concatenate([jnp.where(head_k == hd, q, jnp.zeros_like(q)) for hd in range(RET_HEADS)], axis=0)
        a = (_dot_nt(qst, x_ref[rows, 256:512]) * dec_ref[...]).astype(BF16)
        oa = _dot(a, x_ref[rows, 512:768])
        acc = jnp.where(head_v == 0, oa[0:blk, :], 0.0)
        for hd in range(1, RET_HEADS):
            acc = acc + jnp.where(head_v == hd, oa[hd * blk:(hd + 1) * blk, :], 0.0)
        intra.append(acc)
    for sb in range(nsb):
        rows = slice(sb * blk, (sb + 1) * blk)
        q = x_ref[rows, 0:256]
        k = x_ref[rows, 256:512]
        v = x_ref[rows, 512:768]
        gate = x_ref[rows, 768:1024].astype(F32)
        state = s_ref[...]
        qw = (q.astype(F32) * qs_ref[...]).astype(BF16)
        o = intra[sb] + _dot(qw, state.astype(BF16))
        kw_t = (k.astype(F32) * ks_ref[...]).T.astype(BF16)
        s_ref[...] = gbm_ref[...] * state + bm_ref[...] * _dot(kw_t, v)
        hi, lo = _split2(o)
        mu2 = _dot(jnp.concatenate([hi, lo], axis=0), bdv)
        d = o - (mu2[0:blk] + mu2[blk:2 * blk]) * inv_d
        hi, lo = _split2(d * d)
        var2 = _dot(jnp.concatenate([hi, lo], axis=0), bdv)
        var = (var2[0:blk] + var2[blk:2 * blk]) * inv_d
        on = d * lax.rsqrt(var + NORM_EPS) * gain_ref[...]
        o_ref[rows, :] = (on * (gate / (1.0 + jnp.exp(-gate)))).astype(BF16)
    sout_ref[...] = s_ref[...]


def _ret_constants(blk, chunk_len):
    log_g = jnp.log1p(-jnp.exp2(-5.0 - jnp.arange(RET_HEADS, dtype=F32)))
    lane = np.arange(RET_WIDTH)
    head_k = (lane % 128) // 32
    head_v = lane // 64
    i = jnp.arange(blk, dtype=F32)
    diff = i[:, None] - i[None, :]
    dec = jnp.where(diff[None] >= 0.0, jnp.exp(jnp.maximum(diff, 0.0)[None] * log_g[:, None, None]), 0.0)
    dec = dec.reshape(RET_HEADS * blk, blk)
    lg_k = log_g[head_k]
    qs = jnp.exp((i + 1.0)[:, None] * lg_k[None, :])
    ks = jnp.exp((chunk_len - 1.0 - i)[:, None] * lg_k[None, :])
    gbm = jnp.broadcast_to(jnp.exp(chunk_len * lg_k)[:, None], (RET_WIDTH, RET_WIDTH))
    bm = jnp.asarray(head_k[:, None] == head_v[None, :], F32)
    bdv = jnp.asarray(head_v[:, None] == head_v[None, :], BF16)
    return dec, qs, ks, gbm, bm, bdv


def _retention(l, ret_in, s0, gain, consts, blk, rt):
    g, r, _ = ret_in.shape
    grid = (g, r // rt)
    const2 = lambda shape: pl.BlockSpec(shape, lambda b, i: (0,) * len(shape))
    dec, qs, ks, gbm, bm, bdv = consts
    in_specs = [
        pl.BlockSpec((None, rt, 1024), lambda b, i: (b, i, 0)),
        pl.BlockSpec((None, RET_WIDTH, RET_WIDTH), lambda b, i: (b, 0, 0)),
        pl.BlockSpec((None, 1, RET_WIDTH), lambda b, i: (l, 0, 0)),
        const2(dec.shape), const2(qs.shape), const2(ks.shape), const2(gbm.shape), const2(bm.shape),
        const2(bdv.shape),
    ]
    return pl.pallas_call(
        functools.partial(_ret_kernel, blk=blk),
        grid=grid, in_specs=in_specs,
        out_specs=[pl.BlockSpec((None, rt, RET_WIDTH), lambda b, i: (b, i, 0)),
                   pl.BlockSpec((None, RET_WIDTH, RET_WIDTH), lambda b, i: (b, 0, 0))],
        out_shape=[jax.ShapeDtypeStruct((g, r, RET_WIDTH), BF16),
                   jax.ShapeDtypeStruct((g, RET_WIDTH, RET_WIDTH), F32)],
        scratch_shapes=[pltpu.VMEM((RET_WIDTH, RET_WIDTH), F32)],
        compiler_params=pltpu.CompilerParams(dimension_semantics=("arbitrary", "arbitrary"),
                                             vmem_limit_bytes=VMEM_LIMIT),
        name="retention",
    )(ret_in, s0, gain, dec, qs, ks, gbm, bm, bdv)


def _state_to_bd(st):
    g = st.shape[0]
    t = jnp.einsum("bhpie,hg->bphige", st.reshape(g, RET_HEADS, 2, 32, 64), jnp.eye(RET_HEADS, dtype=st.dtype))
    return t.reshape(g, RET_WIDTH, RET_WIDTH)


def _bd_to_state(sbd):
    g = sbd.shape[0]
    t = jnp.einsum("bphihe->bhpie", sbd.reshape(g, 2, RET_HEADS, 32, RET_HEADS, 64))
    return t.reshape(g, RET_HEADS, 64, 64)


def _cumsum_kernel(x_ref, u_ref, o_ref):
    nc = x_ref.shape[1] // LANES
    xs = jnp.concatenate([x_ref[:, LANES * j:LANES * (j + 1)] for j in range(nc)], axis=0)
    x1 = xs.astype(BF16)
    r1 = xs - x1.astype(F32)
    x2 = r1.astype(BF16)
    x3 = (r1 - x2.astype(F32)).astype(BF16)
    tri = u_ref[...]
    loc = _dot(x1, tri) + _dot(x2, tri) + _dot(x3, tri)
    off = jnp.zeros((SUBLANES, 1), F32)
    for j in range(nc):
        lj = loc[SUBLANES * j:SUBLANES * (j + 1), :]
        o_ref[:, LANES * j:LANES * (j + 1)] = (lj + off) * (-LOG2E)
        off = off + lj[:, LANES - 1:LANES]


def _cumsum_lanes(x):
    n, h, w = x.shape
    tri = jnp.asarray(np.arange(LANES)[:, None] <= np.arange(LANES)[None, :], BF16)
    return pl.pallas_call(
        _cumsum_kernel,
        grid=(n,),
        in_specs=[pl.BlockSpec((None, h, w), lambda b: (b, 0, 0)),
                  pl.BlockSpec((LANES, LANES), lambda b: (0, 0))],
        out_specs=pl.BlockSpec((None, h, w), lambda b: (b, 0, 0)),
        out_shape=jax.ShapeDtypeStruct((n, h, w), F32),
        compiler_params=pltpu.CompilerParams(dimension_semantics=("arbitrary",)),
        name="forget_cumsum",
    )(x, tri)


def _fox_prompt_kernel(q_ref, k_ref, vt_ref, nc_ref, o_ref, va_ref, ncol_ref, qh_ref, sd_ref, sa_ref, sb_ref,
                       m_ref, acc_ref, *, tile):
    nq = q_ref.shape[0] // tile
    lane = lax.broadcasted_iota(jnp.int32, (1, LANES), 1)
    krow = lax.broadcasted_iota(jnp.int32, (tile, tile), 0)
    qcol = lax.broadcasted_iota(jnp.int32, (tile, tile), 1)
    for j in range(nq):
        for hd in range(2):
            va_ref[hd, j, 0:FOX_HEAD_DIM, :] = vt_ref[FOX_HEAD_DIM * hd:FOX_HEAD_DIM * (hd + 1),
                                                       tile * j:tile * (j + 1)]
            va_ref[hd, j, FOX_HEAD_DIM:FOX_VROWS, :] = jnp.ones((FOX_VROWS - FOX_HEAD_DIM, tile), BF16)
        rows8 = jnp.concatenate([nc_ref[j], jnp.zeros((SUBLANES - 2, tile), F32)], axis=0)
        ncol_ref[j] = rows8.T

    def set_q(slot, qi):
        q = q_ref[pl.ds(pl.multiple_of(qi * tile, tile), tile), :]
        for hd in range(2):
            qh_ref[slot, hd] = jnp.where(lane // FOX_HEAD_DIM == hd, q, jnp.zeros_like(q))

    def park(dst_ref, qslot, kj, causal):
        kt = k_ref[pl.ds(pl.multiple_of(kj * tile, tile), tile), :]
        ncol = ncol_ref[kj]
        for hd in range(2):
            s = _dot_nt(kt, qh_ref[qslot, hd]) + ncol[:, hd:hd + 1]
            dst_ref[hd] = jnp.where(krow <= qcol, s, -jnp.inf) if causal else s

    def consume(src_ref, kj):
        for hd in range(2):
            s = src_ref[hd]
            m_prev = m_ref[hd]
            m_new = jnp.maximum(m_prev, jnp.max(s, axis=0, keepdims=True))
            alpha = jnp.exp2(m_prev - m_new)
            p = jnp.exp2(s - m_new).astype(BF16)
            acc_ref[hd] = alpha * acc_ref[hd] + _dot(va_ref[hd, kj], p)
            m_ref[hd] = m_new

    def park_next_diag(qi):
        nxt = jnp.minimum(qi + 1, nq - 1)
        set_q(1 - qi % 2, nxt)
        park(sd_ref, 1 - qi % 2, nxt, True)

    set_q(0, 0)
    park(sd_ref, 0, 0, True)

    def q_body(qi, carry):
        qs = qi % 2
        m_ref[...] = jnp.full(m_ref.shape, -jnp.inf, F32)
        acc_ref[...] = jnp.zeros(acc_ref.shape, F32)

        @pl.when(qi == 0)
        def _():
            consume(sd_ref, 0)
            park_next_diag(qi)

        @pl.when(qi > 0)
        def _():
            park(sa_ref, qs, 0, False)
            consume(sd_ref, qi)

            def kv_pair(jp, c2):
                j = 2 * jp + 1
                park(sb_ref, qs, j, False)
                consume(sa_ref, j - 1)
                park(sa_ref, qs, j + 1, False)
                consume(sb_ref, j)
                return c2

            lax.fori_loop(0, (qi - 1) // 2, kv_pair, 0)

            @pl.when(qi % 2 == 0)
            def _():
                park(sb_ref, qs, qi - 1, False)
                consume(sa_ref, qi - 2)
                park_next_diag(qi)
                consume(sb_ref, qi - 1)

            @pl.when(qi % 2 == 1)
            def _():
                park_next_diag(qi)
                consume(sa_ref, qi - 1)

        ot = jnp.concatenate(
            [acc_ref[hd, 0:FOX_HEAD_DIM, :] / acc_ref[hd, FOX_HEAD_DIM:FOX_HEAD_DIM + 1, :] for hd in range(2)],
            axis=0)
        q0 = pl.multiple_of(qi * tile, tile)
        o_ref[pl.ds(q0, tile), :] = ot.T.astype(BF16)
        return carry

    lax.fori_loop(0, nq, q_body, 0)


def _fox_prompt(q, k, vt, nc_rows, tile):
    b, s, _ = q.shape
    nt = s // tile
    qk_spec = pl.BlockSpec((None, s, LANES), lambda bi, hp: (bi, 0, hp))
    return pl.pallas_call(
        functools.partial(_fox_prompt_kernel, tile=tile),
        grid=(b, FOX_HEADS // 2),
        in_specs=[qk_spec, qk_spec, pl.BlockSpec((None, LANES, s), lambda bi, hp: (bi, hp, 0)),
                  pl.BlockSpec((None, None, nt, 2, tile), lambda bi, hp: (bi, hp, 0, 0, 0))],
        out_specs=qk_spec,
        out_shape=jax.ShapeDtypeStruct((b, s, FOX_WIDTH), BF16),
        scratch_shapes=[pltpu.VMEM((2, nt, FOX_VROWS, tile), BF16), pltpu.VMEM((nt, tile, SUBLANES), F32),
                        pltpu.VMEM((2, 2, tile, LANES), BF16),
                        pltpu.VMEM((2, tile, tile), F32), pltpu.VMEM((2, tile, tile), F32),
                        pltpu.VMEM((2, tile, tile), F32),
                        pltpu.VMEM((2, 1, tile), F32), pltpu.VMEM((2, FOX_VROWS, tile), F32)],
        compiler_params=pltpu.CompilerParams(dimension_semantics=("arbitrary", "arbitrary"),
                                             vmem_limit_bytes=VMEM_LIMIT),
        name="fox_prompt",
    )(q, k, vt, nc_rows)


def _fox_sample_kernel(q_ref, kt_ref, vt_ref, knt_ref, vnt_ref, ncc_ref, ncn_ref, o_ref,
                       m_ref, l_ref, acc_ref):
    chunk = pl.program_id(1)
    n = q_ref.shape[0]
    lane = lax.broadcasted_iota(jnp.int32, (1, FOX_WIDTH), 1)

    @pl.when(chunk == 0)
    def _():
        m_ref[...] = jnp.full(m_ref.shape, -jnp.inf, F32)
        l_ref[...] = jnp.zeros(l_ref.shape, F32)
        acc_ref[...] = jnp.zeros(acc_ref.shape, F32)

    q = q_ref[...]
    qbd = jnp.concatenate(
        [jnp.where(lane // FOX_HEAD_DIM == hd, q, jnp.zeros_like(q)) for hd in range(FOX_HEADS)], axis=0)

    def head_rows(nc):
        return jnp.concatenate(
            [jnp.broadcast_to(nc[hd:hd + 1, :], (n, nc.shape[1])) for hd in range(FOX_HEADS)], axis=0)

    def update(s, vt):
        m_prev = m_ref[...]
        m_new = jnp.maximum(m_prev, jnp.max(s, axis=-1, keepdims=True))
        alpha = jnp.exp2(m_prev - m_new)
        p = jnp.exp2(s - m_new[:, 0:1])
        l_ref[...] = alpha * l_ref[...] + jnp.sum(p, axis=-1, keepdims=True)
        acc_ref[...] = alpha[:, 0:1] * acc_ref[...] + _dot_nt(p.astype(BF16), vt)
        m_ref[...] = m_new

    update(_dot(qbd, kt_ref[...].astype(BF16)) + head_rows(ncc_ref[...]), vt_ref[...].astype(BF16))

    @pl.when(chunk == pl.num_programs(1) - 1)
    def _():
        r = lax.broadcasted_iota(jnp.int32, (FOX_HEADS * n, n), 0)
        c = lax.broadcasted_iota(jnp.int32, (FOX_HEADS * n, n), 1)
        sn = _dot(qbd, knt_ref[...].astype(BF16)) + head_rows(ncn_ref[:, 0:n])
        update(jnp.where(c <= r % n, sn, -jnp.inf), vnt_ref[...].astype(BF16))
        o_all = acc_ref[...] / l_ref[:, 0:1]
        o = jnp.zeros((n, FOX_WIDTH), F32)
        for hd in range(FOX_HEADS):
            o = o + jnp.where(lane // FOX_HEAD_DIM == hd, o_all[n * hd:n * (hd + 1), :], 0.0)
        o_ref[...] = o.astype(BF16)


def _fox_sample(l, q, knt, vnt, cache_kt, cache_vt, nc):
    nb, n, _ = q.shape
    past = cache_kt.shape[3]
    ck = min(FOX_SAMPLE_CHUNK, past)
    q_spec = pl.BlockSpec((None, n, FOX_WIDTH), lambda b, c: (b, 0, 0))
    new_spec = pl.BlockSpec((None, FOX_WIDTH, n), lambda b, c: (b, 0, 0))
    cache_spec = pl.BlockSpec((None, None, FOX_WIDTH, ck), lambda b, c: (l, b, 0, c))
    rows = FOX_HEADS * n
    return pl.pallas_call(
        _fox_sample_kernel,
        grid=(nb, past // ck),
        in_specs=[q_spec, cache_spec, cache_spec, new_spec, new_spec,
                  pl.BlockSpec((None, FOX_HEADS, ck), lambda b, c: (b, 0, c)),
                  pl.BlockSpec((None, FOX_HEADS, LANES), lambda b, c: (b, 0, past // LANES))],
        out_specs=q_spec,
        out_shape=jax.ShapeDtypeStruct((nb, n, FOX_WIDTH), BF16),
        scratch_shapes=[pltpu.VMEM((rows, LANES), F32), pltpu.VMEM((rows, LANES), F32),
                        pltpu.VMEM((rows, FOX_WIDTH), F32)],
        compiler_params=pltpu.CompilerParams(dimension_semantics=("arbitrary", "arbitrary"),
                                             vmem_limit_bytes=VMEM_LIMIT),
        name="fox_sample",
    )(q, cache_kt, cache_vt, knt, vnt, nc, nc)


def _mlp_kernel(*refs, tmajor, nb):
    if tmajor:
        (x_ref, oret_ref, ofox_ref, oconv_ref, wout_ref, nw_ref, wg_ref, wu_ref, wd_ref, fcw_ref, hist_ref,
         y_ref, st_ref) = refs
    else:
        (x_ref, oret_ref, ofox_ref, oconv_ref, wout_ref, nw_ref, wg_ref, wu_ref, wd_ref, fcw_ref,
         y_ref, st_ref, carry_ref) = refs
    tm = x_ref.shape[0]
    rb = tm if tmajor else min(MLP_ROW_BLOCK, tm)
    for r0 in range(0, tm, rb):
        rows = slice(r0, r0 + rb)
        x1 = (x_ref[rows, :] + _dot(oret_ref[rows, :], wout_ref[0:256, :])
              + _dot(ofox_ref[rows, :], wout_ref[256:768, :]) + _dot(oconv_ref[rows, :], wout_ref[768:1024, :]))
        h2 = _rms(x1, nw_ref[...]).astype(BF16)
        down = None
        for c0, c1 in FF_CHUNKS:
            cols = slice(c0, c1)
            gate_pre = _dot(h2, wg_ref[:, cols])
            if tmajor:
                gate_c, new_hist = _dwconv_tmajor(gate_pre, fcw_ref.at[:, cols], hist_ref[:, cols], nb)
                st_ref[:, cols] = new_hist
            else:
                first = (pl.program_id(1) == 0) if r0 == 0 else None
                gate_c = _dwconv_carry(gate_pre, fcw_ref.at[:, cols], carry_ref.at[:, cols], first)
                if r0 + rb == tm:
                    st_ref[:, cols] = carry_ref[6:8, cols]
            up = _dot(h2, wu_ref[:, cols])
            act = (gate_c / (1.0 + jnp.exp(-gate_c)) * up).astype(BF16)
            part = _dot(act, wd_ref[cols, :])
            down = part if down is None else down + part
        y_ref[rows, :] = x1 + down


def _mlp(l, x, oret, ofox, oconv, prm, *, hist=None, nb=0):
    tmajor = hist is not None
    g, r, _ = x.shape
    tm = min(TOKEN_TILE, r)
    grid = (g, r // tm)
    lsel = lambda b, i: (l, 0, 0)
    tok = lambda width: pl.BlockSpec((None, tm, width), lambda b, i: (b, i, 0))
    wspec = lambda rows, cols: pl.BlockSpec((None, rows, cols), lsel, pipeline_mode=pl.Buffered(1))
    in_specs = [tok(D_MODEL), tok(RET_WIDTH), tok(FOX_WIDTH), tok(CONV_DIM),
                wspec(D_MODEL, D_MODEL), pl.BlockSpec((None, 1, D_MODEL), lsel),
                wspec(D_MODEL, D_FF), wspec(D_MODEL, D_FF), wspec(D_FF, D_MODEL),
                pl.BlockSpec((None, 3, D_FF), lsel)]
    args = [x, oret, ofox, oconv, prm["w_out"], prm["norm_ffn"], prm["w_gate"], prm["w_up"], prm["w_down"],
            prm["ffn_conv_w"]]
    if tmajor:
        in_specs.append(pl.BlockSpec((2 * nb, D_FF), lambda b, i: (0, 0)))
        args.append(hist)
        st_shape = jax.ShapeDtypeStruct((2 * nb, D_FF), F32)
        st_spec = pl.BlockSpec((2 * nb, D_FF), lambda b, i: (0, 0))
        scratch = []
    else:
        st_shape = jax.ShapeDtypeStruct((g, 2, D_FF), F32)
        st_spec = pl.BlockSpec((None, 2, D_FF), lambda b, i: (b, 0, 0))
        scratch = [pltpu.VMEM((SUBLANES, D_FF), F32)]
    return pl.pallas_call(
        functools.partial(_mlp_kernel, tmajor=tmajor, nb=nb),
        grid=grid, in_specs=in_specs,
        out_specs=[tok(D_MODEL), st_spec],
        out_shape=[jax.ShapeDtypeStruct((g, r, D_MODEL), F32), st_shape],
        scratch_shapes=scratch,
        compiler_params=pltpu.CompilerParams(dimension_semantics=("arbitrary", "arbitrary"),
                                             vmem_limit_bytes=VMEM_LIMIT),
        name="outproj_mlp",
    )(*args)


def _win_permutation():
    idx = []
    for base in (0, 256):
        for half in range(2):
            for hd in range(RET_HEADS):
                idx += [base + 64 * hd + 32 * half + i for i in range(32)]
    idx += list(range(512, 1024))
    idx += list(range(1024, 2560))
    idx += list(range(2568, 3336))
    idx += list(range(2560, 2568))
    return np.asarray(idx, np.int32)


def _rope_tables(pos):
    half = 32
    inv_freq = ROPE_BASE ** (-jnp.arange(half, dtype=F32) / half)
    ang = pos.astype(F32)[:, None] * inv_freq[None, :]
    return jnp.tile(jnp.cos(ang), (1, RET_HEADS)), jnp.tile(jnp.sin(ang), (1, RET_HEADS))


def _to_bmajor(a, n, nb):
    return a.reshape(n, nb, a.shape[-1]).transpose(1, 0, 2)


def _to_tmajor(a):
    nb, n, c = a.shape
    return a.transpose(1, 0, 2).reshape(1, n * nb, c)


def kernel(x_prompt, x_sample, cache_fox_k, cache_fox_v, cache_fox_logf, state_ret, state_conv, state_ffn_conv,
           norm_mix, w_in, ret_gn_gain, fox_q_gain, fox_k_gain, fox_f_bias, conv_w, w_out, norm_ffn, w_gate,
           w_up, ffn_conv_w, w_down):
    depth = w_in.shape[0]
    b, s, _ = x_prompt.shape
    nb, n, _ = x_sample.shape
    past = cache_fox_k.shape[2]

    w_perm = jnp.take(w_in, _win_permutation(), axis=2)
    w_perm = jnp.pad(w_perm, ((0, 0), (0, 0), (0, PROJ_PAD - w_perm.shape[2]))).astype(BF16)
    prm = {
        "w_in": w_perm,
        "norm_mix": norm_mix.reshape(depth, 1, D_MODEL),
        "gq": fox_q_gain.reshape(depth, 1, FOX_WIDTH),
        "gk": fox_k_gain.reshape(depth, 1, FOX_WIDTH),
        "fb": jnp.pad(fox_f_bias, ((0, 0), (0, LANES - FOX_HEADS))).reshape(depth, 1, LANES),
        "conv_w": conv_w,
        "w_out": w_out.astype(BF16),
        "norm_ffn": norm_ffn.reshape(depth, 1, D_MODEL),
        "w_gate": w_gate.astype(BF16),
        "w_up": w_up.astype(BF16),
        "w_down": w_down.astype(BF16),
        "ffn_conv_w": ffn_conv_w,
    }
    gain = ret_gn_gain.reshape(depth, 1, RET_WIDTH)
    lane = np.arange(FOX_WIDTH)
    bd = jnp.asarray(lane[:, None] // FOX_HEAD_DIM == lane[None, :] // FOX_HEAD_DIM, BF16)

    cos_p, sin_p = _rope_tables(jnp.arange(s, dtype=jnp.int32))
    pos_s = past + jnp.arange(n, dtype=jnp.int32)
    cos_s, sin_s = _rope_tables(jnp.repeat(pos_s, nb))

    ret_blk = min(RET_BLOCK, s)
    ret_rt = min(TOKEN_TILE, s)
    consts_p = _ret_constants(ret_blk, float(ret_blk))
    samp_rows = LANES
    consts_s = _ret_constants(samp_rows, float(n))
    fox_tile = min(FOX_TILE, s)

    cache_k = jnp.transpose(cache_fox_k, (0, 1, 3, 4, 2)).reshape(depth, nb, FOX_WIDTH, past)
    cache_v = jnp.transpose(cache_fox_v, (0, 1, 3, 4, 2)).reshape(depth, nb, FOX_WIDTH, past)
    cache_lf_t = jnp.transpose(cache_fox_logf, (0, 1, 3, 2))
    c_width = -(-(past + n) // LANES) * LANES

    kbuf = jnp.zeros((depth, b, FOX_WIDTH, s), F32)
    vbuf = jnp.zeros((depth, b, FOX_WIDTH, s), F32)
    zero_state = jnp.zeros((b, RET_WIDTH, RET_WIDTH), F32)

    xp = x_prompt
    xs = jnp.transpose(x_sample, (1, 0, 2)).reshape(1, n * nb, D_MODEL)
    p_lf, p_ret, p_conv, p_ffn = [], [], [], []
    s_k, s_v, s_lf, s_ret, s_conv, s_ffn = [], [], [], [], [], []
    for l in range(depth):
        ret_in, fq, kbuf, vbuf, fkb, fvb, lf, lf_t, oconv, conv_st = _inproj(
            l, xp, prm, cos_p, sin_p, bd, kbuf=kbuf, vbuf=vbuf)
        oret, sbd = _retention(l, ret_in, zero_state, gain, consts_p, ret_blk, ret_rt)
        c_rows = _cumsum_lanes(lf_t)
        nc_rows = c_rows.reshape(b, FOX_HEADS // 2, 2, s // fox_tile, fox_tile).transpose(0, 1, 3, 2, 4)
        ofox = _fox_prompt(fq, fkb, fvb, nc_rows, fox_tile)
        xp, ffn_st = _mlp(l, xp, oret, ofox, oconv, prm)
        p_lf.append(lf)
        p_ret.append(_bd_to_state(sbd))
        p_conv.append(conv_st)
        p_ffn.append(ffn_st)

        hist_c = jnp.transpose(state_conv[l], (1, 0, 2)).reshape(2 * nb, CONV_DIM)
        ret_in, fq, fk32, fv32, fkb, fvb, lf, _, oconv, conv_st = _inproj(
            l, xs, prm, cos_s, sin_s, bd, hist=hist_c, nb=nb)
        ret_b = jnp.pad(_to_bmajor(ret_in, n, nb), ((0, 0), (0, samp_rows - n), (0, 0)))
        oret, sbd = _retention(l, ret_b, _state_to_bd(state_ret[l]), gain, consts_s, samp_rows, samp_rows)
        lf_b = _to_bmajor(lf, n, nb)
        lf_all = jnp.concatenate(
            [cache_lf_t[l], jnp.transpose(lf_b, (0, 2, 1)),
             jnp.zeros((nb, FOX_HEADS, c_width - past - n), F32)], axis=2)
        nc = _cumsum_lanes(lf_all)
        k_new = _to_bmajor(fk32, n, nb)
        v_new = _to_bmajor(fv32, n, nb)
        s_k.append(k_new.reshape(nb, n, FOX_HEADS, FOX_HEAD_DIM))
        s_v.append(v_new.reshape(nb, n, FOX_HEADS, FOX_HEAD_DIM))
        ofox = _fox_sample(l, _to_bmajor(fq, n, nb), jnp.transpose(k_new, (0, 2, 1)),
                           jnp.transpose(v_new, (0, 2, 1)), cache_k, cache_v, nc)
        hist_f = jnp.transpose(state_ffn_conv[l], (1, 0, 2)).reshape(2 * nb, D_FF)
        xs, ffn_st = _mlp(l, xs, _to_tmajor(oret[:, :n]), _to_tmajor(ofox), oconv, prm, hist=hist_f, nb=nb)
        s_lf.append(lf_b)
        s_ret.append(_bd_to_state(sbd))
        s_conv.append(conv_st.reshape(2, nb, CONV_DIM).transpose(1, 0, 2))
        s_ffn.append(ffn_st.reshape(2, nb, D_FF).transpose(1, 0, 2))

    y_sample = xs.reshape(n, nb, D_MODEL).transpose(1, 0, 2)
    stk = lambda ts: jnp.stack(ts, axis=0)
    from_fm = lambda a: jnp.transpose(a.reshape(depth, b, FOX_HEADS, FOX_HEAD_DIM, s), (0, 1, 4, 2, 3))
    return (xp, y_sample,
            from_fm(kbuf), from_fm(vbuf),
            stk(p_lf), stk(p_ret), stk(p_conv), stk(p_ffn),
            stk(s_k), stk(s_v), stk(s_lf), stk(s_ret), stk(s_conv), stk(s_ffn))
```

```python
import functools

import numpy as np
import jax
import jax.numpy as jnp
from jax import lax
from jax.experimental import pallas as pl
from jax.experimental.pallas import tpu as pltpu

F32 = jnp.float32
BF16 = jnp.bfloat16

D_MODEL = 1024
RET_HEADS = 4
RET_WIDTH = 256
FOX_HEADS = 8
FOX_HEAD_DIM = 64
FOX_WIDTH = 512
CONV_DIM = 256
D_FF = 2816
PROJ_PAD = 3456
ROPE_BASE = 10000.0
NORM_EPS = 1e-6
LANES = 128
SUBLANES = 8
VMEM_LIMIT = 56 * 1024 * 1024

TOKEN_TILE = 512
RET_BLOCK = 256
MLP_ROW_BLOCK = 256
FOX_TILE = 512
FOX_SAMPLE_CHUNK = 2048
FOX_VROWS = 80
LOG2E = 1.4426950408889634
FOX_Q_SCALE = FOX_HEAD_DIM ** -0.5 * LOG2E
MXU_DEPTH = 256
FF_CHUNKS = ((0, D_FF),)


def _dot(a, b):
    return jnp.dot(a, b, preferred_element_type=F32)


def _dot_nt(a, b):
    return lax.dot_general(a, b, (((1,), (1,)), ((), ())), preferred_element_type=F32)


def _split2(x):
    hi = x.astype(BF16)
    lo = (x - hi.astype(F32)).astype(BF16)
    return hi, lo


def _group_sum(x, ones_bd):
    hi, lo = _split2(x)
    return _dot(hi, ones_bd) + _dot(lo, ones_bd)


def _rms(x, g):
    ms = jnp.mean(x * x, axis=-1, keepdims=True)
    return x * lax.rsqrt(ms + NORM_EPS) * g


def _dwconv_carry(u, w_ref, carry_ref, first):
    tm = u.shape[0]

    if first is not None:
        @pl.when(first)
        def _():
            carry_ref[...] = jnp.zeros(carry_ref.shape, F32)

    row = lax.broadcasted_iota(jnp.int32, u.shape, 0)
    c6 = carry_ref[6:7, :]
    c7 = carry_ref[7:8, :]
    um1 = jnp.where(row == 0, c7, pltpu.roll(u, 1, axis=0))
    um2 = jnp.where(row == 0, c6, jnp.where(row == 1, c7, pltpu.roll(u, 2, axis=0)))
    y = w_ref[0:1, :] * um2 + w_ref[1:2, :] * um1 + w_ref[2:3, :] * u
    carry_ref[...] = u[tm - SUBLANES:tm, :]
    return y


def _dwconv_tmajor(u, w_ref, hist, nb):
    tm = u.shape[0]
    full = jnp.concatenate([hist, u], axis=0)
    y = (w_ref[0:1, :] * full[0:tm] + w_ref[1:2, :] * full[nb:nb + tm]
         + w_ref[2:3, :] * full[2 * nb:2 * nb + tm])
    return y, full[tm:tm + 2 * nb]


def _inproj_kernel(*refs, tmajor, nb):
    if tmajor:
        (x_ref, nw_ref, w_ref, cos_ref, sin_ref, gq_ref, gk_ref, fb_ref, cw_ref, bd_ref, hist_ref,
         ret_ref, fq_ref, fk32_ref, fv32_ref, fkb_ref, fvb_ref, lf_ref, lft_ref, oc_ref, st_ref) = refs
    else:
        (x_ref, nw_ref, w_ref, cos_ref, sin_ref, gq_ref, gk_ref, fb_ref, cw_ref, bd_ref,
         _, _, ret_ref, fq_ref, fk32_ref, fv32_ref, fkb_ref, fvb_ref, lf_ref, lft_ref, oc_ref, st_ref,
         carry_ref) = refs
    h = _rms(x_ref[...], nw_ref[...]).astype(BF16)

    a = _dot(h, w_ref[:, 0:1024])
    f = _dot(h, w_ref[:, 1024:2560])
    c = _dot(h, w_ref[:, 2560:3328])
    z = _dot(h, w_ref[:, 3328:3456]) + fb_ref[...]

    cos = cos_ref[...]
    sin = sin_ref[...]
    q1, q2, k1, k2 = a[:, 0:128], a[:, 128:256], a[:, 256:384], a[:, 384:512]
    ret_ref[:, 0:128] = (q1 * cos - q2 * sin).astype(BF16)
    ret_ref[:, 128:256] = (q1 * sin + q2 * cos).astype(BF16)
    ret_ref[:, 256:384] = ((k1 * cos - k2 * sin) * 0.125).astype(BF16)
    ret_ref[:, 384:512] = ((k1 * sin + k2 * cos) * 0.125).astype(BF16)
    ret_ref[:, 512:1024] = a[:, 512:1024].astype(BF16)

    fq, fk, fv = f[:, 0:512], f[:, 512:1024], f[:, 1024:1536]
    bd = bd_ref[...]
    inv_d = 1.0 / FOX_HEAD_DIM
    ssq = _dot((fq * fq).astype(BF16), bd)
    fq_ref[...] = (fq * lax.rsqrt(ssq * inv_d + NORM_EPS) * gq_ref[...] * FOX_Q_SCALE).astype(BF16)
    ssk = _group_sum(fk * fk, bd)
    fkn = fk * lax.rsqrt(ssk * inv_d + NORM_EPS) * gk_ref[...]
    fkb_ref[...] = fkn.astype(BF16)
    if tmajor:
        fk32_ref[...] = fkn
        fv32_ref[...] = fv
        fvb_ref[...] = fv.astype(BF16)
    else:
        fk32_ref[...] = fkn.T
        fvt = fv.T
        fv32_ref[...] = fvt
        fvb_ref[...] = fvt.astype(BF16)

    cb, cc, ch = c[:, 0:256], c[:, 256:512], c[:, 512:768]
    u = cc * ch
    if tmajor:
        y, new_hist = _dwconv_tmajor(u, cw_ref, hist_ref[...], nb)
        st_ref[...] = new_hist
    else:
        y = _dwconv_carry(u, cw_ref, carry_ref, pl.program_id(1) == 0)
        st_ref[...] = carry_ref[6:8, :]
    oc_ref[...] = (cb * y).astype(BF16)

    ls = jnp.minimum(z, 0.0) - jnp.log1p(jnp.exp(-jnp.abs(z)))
    lf_ref[...] = ls[:, 0:FOX_HEADS]
    lft_ref[...] = ls.T[0:FOX_HEADS, :]


def _inproj(l, x, prm, cos, sin, bd, *, kbuf=None, vbuf=None, hist=None, nb=0):
    tmajor = hist is not None
    g, r, _ = x.shape
    tm = min(TOKEN_TILE, r)
    grid = (g, r // tm)
    depth = prm["w_in"].shape[0]
    lsel = lambda b, i: (l, 0, 0)
    in_specs = [
        pl.BlockSpec((None, tm, D_MODEL), lambda b, i: (b, i, 0)),
        pl.BlockSpec((None, 1, D_MODEL), lsel),
        pl.BlockSpec((None, D_MODEL, PROJ_PAD), lsel, pipeline_mode=pl.Buffered(1)),
        pl.BlockSpec((tm, LANES), lambda b, i: (i, 0)),
        pl.BlockSpec((tm, LANES), lambda b, i: (i, 0)),
        pl.BlockSpec((None, 1, FOX_WIDTH), lsel),
        pl.BlockSpec((None, 1, FOX_WIDTH), lsel),
        pl.BlockSpec((None, 1, LANES), lsel),
        pl.BlockSpec((None, 3, CONV_DIM), lsel),
        pl.BlockSpec((FOX_WIDTH, FOX_WIDTH), lambda b, i: (0, 0)),
    ]
    args = [x, prm["norm_mix"], prm["w_in"], cos, sin, prm["gq"], prm["gk"], prm["fb"], prm["conv_w"], bd]
    tok = lambda width: pl.BlockSpec((None, tm, width), lambda b, i: (b, i, 0))
    if tmajor:
        in_specs.append(pl.BlockSpec((2 * nb, CONV_DIM), lambda b, i: (0, 0)))
        args.append(hist)
        kv_shape = jax.ShapeDtypeStruct((g, r, FOX_WIDTH), F32)
        kv_spec = tok(FOX_WIDTH)
        vb_shape = jax.ShapeDtypeStruct((g, r, FOX_WIDTH), BF16)
        vb_spec = tok(FOX_WIDTH)
        st_shape = jax.ShapeDtypeStruct((2 * nb, CONV_DIM), F32)
        st_spec = pl.BlockSpec((2 * nb, CONV_DIM), lambda b, i: (0, 0))
        aliases = {}
        scratch = []
    else:
        in_specs += [pl.BlockSpec(memory_space=pl.ANY), pl.BlockSpec(memory_space=pl.ANY)]
        args += [kbuf, vbuf]
        kv_shape = jax.ShapeDtypeStruct((depth, g, FOX_WIDTH, r), F32)
        kv_spec = pl.BlockSpec((None, None, FOX_WIDTH, tm), lambda b, i: (l, b, 0, i))
        vb_shape = jax.ShapeDtypeStruct((g, FOX_WIDTH, r), BF16)
        vb_spec = pl.BlockSpec((None, FOX_WIDTH, tm), lambda b, i: (b, 0, i))
        st_shape = jax.ShapeDtypeStruct((g, 2, CONV_DIM), F32)
        st_spec = pl.BlockSpec((None, 2, CONV_DIM), lambda b, i: (b, 0, 0))
        aliases = {10: 2, 11: 3}
        scratch = [pltpu.VMEM((SUBLANES, CONV_DIM), F32)]
    out_shape = [
        jax.ShapeDtypeStruct((g, r, 1024), BF16),
        jax.ShapeDtypeStruct((g, r, FOX_WIDTH), BF16),
        kv_shape, kv_shape,
        jax.ShapeDtypeStruct((g, r, FOX_WIDTH), BF16),
        vb_shape,
        jax.ShapeDtypeStruct((g, r, FOX_HEADS), F32),
        jax.ShapeDtypeStruct((g, FOX_HEADS, r), F32),
        jax.ShapeDtypeStruct((g, r, CONV_DIM), BF16),
        st_shape,
    ]
    out_specs = [tok(1024), tok(FOX_WIDTH), kv_spec, kv_spec, tok(FOX_WIDTH), vb_spec,
                 tok(FOX_HEADS), pl.BlockSpec((None, FOX_HEADS, tm), lambda b, i: (b, 0, i)),
                 tok(CONV_DIM), st_spec]
    return pl.pallas_call(
        functools.partial(_inproj_kernel, tmajor=tmajor, nb=nb),
        grid=grid, in_specs=in_specs, out_specs=out_specs, out_shape=out_shape,
        scratch_shapes=scratch, input_output_aliases=aliases,
        compiler_params=pltpu.CompilerParams(dimension_semantics=("arbitrary", "arbitrary"),
                                             vmem_limit_bytes=VMEM_LIMIT),
        name="inproj",
    )(*args)


def _ret_kernel(x_ref, s0_ref, gain_ref, dec_ref, qs_ref, ks_ref, gbm_ref, bm_ref, bdv_ref,
                o_ref, sout_ref, s_ref, *, blk):
    @pl.when(pl.program_id(1) == 0)
    def _():
        s_ref[...] = s0_ref[...]

    inv_d = 1.0 / 64.0
    bdv = bdv_ref[...]
    lane = lax.broadcasted_iota(jnp.int32, (1, RET_WIDTH), 1)
    head_k = (lane % 128) // 32
    head_v = lane // 64
    nsb = x_ref.shape[0] // blk
    intra = []
    for sb in range(nsb):
        rows = slice(sb * blk, (sb + 1) * blk)
        q = x_ref[rows, 0:256]
        qst = jnp.concatenate([jnp.where(head_k == hd, q, jnp.zeros_like(q)) for hd in range(RET_HEADS)], axis=0)
        a = (_dot_nt(qst, x_ref[rows, 256:512]) * dec_ref[...]).astype(BF16)
        oa = _dot(a, x_ref[rows, 512:768])
        acc = jnp.where(head_v == 0, oa[0:blk, :], 0.0)
        for hd in range(1, RET_HEADS):
            acc = acc + jnp.where(head_v == hd, oa[hd * blk:(hd + 1) * blk, :], 0.0)
        intra.append(acc)
    for sb in range(nsb):
        rows = slice(sb * blk, (sb + 1) * blk)
        q = x_ref[rows, 0:256]
        k = x_ref[rows, 256:512]
        v = x_ref[rows, 512:768]
        gate = x_ref[rows, 768:1024].astype(F32)
        state = s_ref[...]
        qw = (q.astype(F32) * qs_ref[...]).astype(BF16)
        o = intra[sb] + _dot(qw, state.astype(BF16))
        kw_t = (k.astype(F32) * ks_ref[...]).T.astype(BF16)
        s_ref[...] = gbm_ref[...] * state + bm_ref[...] * _dot(kw_t, v)
        hi, lo = _split2(o)
        mu2 = _dot(jnp.concatenate([hi, lo], axis=0), bdv)
        d = o - (mu2[0:blk] + mu2[blk:2 * blk]) * inv_d
        hi, lo = _split2(d * d)
        var2 = _dot(jnp.concatenate([hi, lo], axis=0), bdv)
        var = (var2[0:blk] + var2[blk:2 * blk]) * inv_d
        on = d * lax.rsqrt(var + NORM_EPS) * gain_ref[...]
        o_ref[rows, :] = (on * (gate / (1.0 + jnp.exp(-gate)))).astype(BF16)
    sout_ref[...] = s_ref[...]


def _ret_constants(blk, chunk_len):
    log_g = jnp.log1p(-jnp.exp2(-5.0 - jnp.arange(RET_HEADS, dtype=F32)))
    lane = np.arange(RET_WIDTH)
    head_k = (lane % 128) // 32
    head_v = lane // 64
    i = jnp.arange(blk, dtype=F32)
    diff = i[:, None] - i[None, :]
    dec = jnp.where(diff[None] >= 0.0, jnp.exp(jnp.maximum(diff, 0.0)[None] * log_g[:, None, None]), 0.0)
    dec = dec.reshape(RET_HEADS * blk, blk)
    lg_k = log_g[head_k]
    qs = jnp.exp((i + 1.0)[:, None] * lg_k[None, :])
    ks = jnp.exp((chunk_len - 1.0 - i)[:, None] * lg_k[None, :])
    gbm = jnp.broadcast_to(jnp.exp(chunk_len * lg_k)[:, None], (RET_WIDTH, RET_WIDTH))
    bm = jnp.asarray(head_k[:, None] == head_v[None, :], F32)
    bdv = jnp.asarray(head_v[:, None] == head_v[None, :], BF16)
    return dec, qs, ks, gbm, bm, bdv


def _retention(l, ret_in, s0, gain, consts, blk, rt):
    g, r, _ = ret_in.shape
    grid = (g, r // rt)
    const2 = lambda shape: pl.BlockSpec(shape, lambda b, i: (0,) * len(shape))
    dec, qs, ks, gbm, bm, bdv = consts
    in_specs = [
        pl.BlockSpec((None, rt, 1024), lambda b, i: (b, i, 0)),
        pl.BlockSpec((None, RET_WIDTH, RET_WIDTH), lambda b, i: (b, 0, 0)),
        pl.BlockSpec((None, 1, RET_WIDTH), lambda b, i: (l, 0, 0)),
        const2(dec.shape), const2(qs.shape), const2(ks.shape), const2(gbm.shape), const2(bm.shape),
        const2(bdv.shape),
    ]
    return pl.pallas_call(
        functools.partial(_ret_kernel, blk=blk),
        grid=grid, in_specs=in_specs,
        out_specs=[pl.BlockSpec((None, rt, RET_WIDTH), lambda b, i: (b, i, 0)),
                   pl.BlockSpec((None, RET_WIDTH, RET_WIDTH), lambda b, i: (b, 0, 0))],
        out_shape=[jax.ShapeDtypeStruct((g, r, RET_WIDTH), BF16),
                   jax.ShapeDtypeStruct((g, RET_WIDTH, RET_WIDTH), F32)],
        scratch_shapes=[pltpu.VMEM((RET_WIDTH, RET_WIDTH), F32)],
        compiler_params=pltpu.CompilerParams(dimension_semantics=("arbitrary", "arbitrary"),
                                             vmem_limit_bytes=VMEM_LIMIT),
        name="retention",
    )(ret_in, s0, gain, dec, qs, ks, gbm, bm, bdv)


def _state_to_bd(st):
    g = st.shape[0]
    t = jnp.einsum("bhpie,hg->bphige", st.reshape(g, RET_HEADS, 2, 32, 64), jnp.eye(RET_HEADS, dtype=st.dtype))
    return t.reshape(g, RET_WIDTH, RET_WIDTH)


def _bd_to_state(sbd):
    g = sbd.shape[0]
    t = jnp.einsum("bphihe->bhpie", sbd.reshape(g, 2, RET_HEADS, 32, RET_HEADS, 64))
    return t.reshape(g, RET_HEADS, 64, 64)


def _cumsum_kernel(x_ref, u_ref, o_ref):
    nc = x_ref.shape[1] // LANES
    xs = jnp.concatenate([x_ref[:, LANES * j:LANES * (j + 1)] for j in range(nc)], axis=0)
    x1 = xs.astype(BF16)
    r1 = xs - x1.astype(F32)
    x2 = r1.astype(BF16)
    x3 = (r1 - x2.astype(F32)).astype(BF16)
    tri = u_ref[...]
    loc = _dot(x1, tri) + _dot(x2, tri) + _dot(x3, tri)
    off = jnp.zeros((SUBLANES, 1), F32)
    for j in range(nc):
        lj = loc[SUBLANES * j:SUBLANES * (j + 1), :]
        o_ref[:, LANES * j:LANES * (j + 1)] = (lj + off) * (-LOG2E)
        off = off + lj[:, LANES - 1:LANES]


def _cumsum_lanes(x):
    n, h, w = x.shape
    tri = jnp.asarray(np.arange(LANES)[:, None] <= np.arange(LANES)[None, :], BF16)
    return pl.pallas_call(
        _cumsum_kernel,
        grid=(n,),
        in_specs=[pl.BlockSpec((None, h, w), lambda b: (b, 0, 0)),
                  pl.BlockSpec((LANES, LANES), lambda b: (0, 0))],
        out_specs=pl.BlockSpec((None, h, w), lambda b: (b, 0, 0)),
        out_shape=jax.ShapeDtypeStruct((n, h, w), F32),
        compiler_params=pltpu.CompilerParams(dimension_semantics=("arbitrary",)),
        name="forget_cumsum",
    )(x, tri)


def _fox_prompt_kernel(q_ref, k_ref, vt_ref, nc_ref, o_ref, va_ref, ncol_ref, qh_ref, sd_ref, sa_ref, sb_ref,
                       m_ref, acc_ref, *, tile):
    nq = q_ref.shape[0] // tile
    lane = lax.broadcasted_iota(jnp.int32, (1, LANES), 1)
    krow = lax.broadcasted_iota(jnp.int32, (tile, tile), 0)
    qcol = lax.broadcasted_iota(jnp.int32, (tile, tile), 1)
    for j in range(nq):
        for hd in range(2):
            va_ref[hd, j, 0:FOX_HEAD_DIM, :] = vt_ref[FOX_HEAD_DIM * hd:FOX_HEAD_DIM * (hd + 1),
                                                       tile * j:tile * (j + 1)]
            va_ref[hd, j, FOX_HEAD_DIM:FOX_VROWS, :] = jnp.ones((FOX_VROWS - FOX_HEAD_DIM, tile), BF16)
        rows8 = jnp.concatenate([nc_ref[j], jnp.zeros((SUBLANES - 2, tile), F32)], axis=0)
        ncol_ref[j] = rows8.T

    def set_q(slot, qi):
        q = q_ref[pl.ds(pl.multiple_of(qi * tile, tile), tile), :]
        for hd in range(2):
            qh_ref[slot, hd] = jnp.where(lane // FOX_HEAD_DIM == hd, q, jnp.zeros_like(q))

    def park(dst_ref, qslot, kj, causal):
        kt = k_ref[pl.ds(pl.multiple_of(kj * tile, tile), tile), :]
        ncol = ncol_ref[kj]
        for hd in range(2):
            s = _dot_nt(kt, qh_ref[qslot, hd]) + ncol[:, hd:hd + 1]
            dst_ref[hd] = jnp.where(krow <= qcol, s, -jnp.inf) if causal else s

    def consume(src_ref, kj, first=False):
        for hd in range(2):
            s = src_ref[hd]
            smax = jnp.max(s, axis=0, keepdims=True)
            if first:
                m_new = smax
                acc_ref[hd] = _dot(va_ref[hd, kj], jnp.exp2(s - m_new).astype(BF16))
            else:
                m_prev = m_ref[hd]
                m_new = jnp.maximum(m_prev, smax)
                alpha = jnp.exp2(m_prev - m_new)
                p = jnp.exp2(s - m_new).astype(BF16)
                acc_ref[hd] = alpha * acc_ref[hd] + _dot(va_ref[hd, kj], p)
            m_ref[hd] = m_new

    def finish(qi):
        ot = jnp.concatenate(
            [acc_ref[hd, 0:FOX_HEAD_DIM, :] / acc_ref[hd, FOX_HEAD_DIM:FOX_HEAD_DIM + 1, :] for hd in range(2)],
            axis=0)
        o_ref[pl.ds(pl.multiple_of(qi * tile, tile), tile), :] = ot.T.astype(BF16)

    def park_next_diag(qi):
        nxt = jnp.minimum(qi + 1, nq - 1)
        set_q(1 - qi % 2, nxt)
        park(sd_ref, 1 - qi % 2, nxt, True)

    set_q(0, 0)
    park(sd_ref, 0, 0, True)

    def q_body(qi, carry):
        qs = qi % 2

        def head(qi):
            park(sa_ref, qs, 0, False)
            consume(sd_ref, qi, first=True)

            def kv_pair(jp, c2):
                j = 2 * jp + 1
                park(sb_ref, qs, j, False)
                consume(sa_ref, j - 1)
                park(sa_ref, qs, j + 1, False)
                consume(sb_ref, j)
                return c2

            lax.fori_loop(0, (qi - 1) // 2, kv_pair, 0)

        @pl.when(qi == 0)
        def _():
            consume(sd_ref, 0, first=True)
            park_next_diag(qi)
            finish(qi)

        @pl.when(jnp.logical_and(qi > 0, qi % 2 == 0))
        def _():
            head(qi)
            park(sb_ref, qs, qi - 1, False)
            consume(sa_ref, qi - 2)
            park_next_diag(qi)
            consume(sb_ref, qi - 1)
            finish(qi)

        @pl.when(qi % 2 == 1)
        def _():
            head(qi)
            park_next_diag(qi)
            consume(sa_ref, qi - 1)
            finish(qi)

        return carry

    lax.fori_loop(0, nq, q_body, 0)


def _fox_prompt(q, k, vt, nc_rows, tile):
    b, s, _ = q.shape
    nt = s // tile
    qk_spec = pl.BlockSpec((None, s, LANES), lambda bi, hp: (bi, 0, hp))
    return pl.pallas_call(
        functools.partial(_fox_prompt_kernel, tile=tile),
        grid=(b, FOX_HEADS // 2),
        in_specs=[qk_spec, qk_spec, pl.BlockSpec((None, LANES, s), lambda bi, hp: (bi, hp, 0)),
                  pl.BlockSpec((None, None, nt, 2, tile), lambda bi, hp: (bi, hp, 0, 0, 0))],
        out_specs=qk_spec,
        out_shape=jax.ShapeDtypeStruct((b, s, FOX_WIDTH), BF16),
        scratch_shapes=[pltpu.VMEM((2, nt, FOX_VROWS, tile), BF16), pltpu.VMEM((nt, tile, SUBLANES), F32),
                        pltpu.VMEM((2, 2, tile, LANES), BF16),
                        pltpu.VMEM((2, tile, tile), F32), pltpu.VMEM((2, tile, tile), F32),
                        pltpu.VMEM((2, tile, tile), F32),
                        pltpu.VMEM((2, 1, tile), F32), pltpu.VMEM((2, FOX_VROWS, tile), F32)],
        compiler_params=pltpu.CompilerParams(dimension_semantics=("arbitrary", "arbitrary"),
                                             vmem_limit_bytes=VMEM_LIMIT),
        name="fox_prompt",
    )(q, k, vt, nc_rows)


def _fox_sample_kernel(q_ref, kt_ref, vt_ref, knt_ref, vnt_ref, ncc_ref, ncn_ref, o_ref,
                       m_ref, l_ref, acc_ref):
    chunk = pl.program_id(1)
    n = q_ref.shape[0]
    lane = lax.broadcasted_iota(jnp.int32, (1, FOX_WIDTH), 1)

    @pl.when(chunk == 0)
    def _():
        m_ref[...] = jnp.full(m_ref.shape, -jnp.inf, F32)
        l_ref[...] = jnp.zeros(l_ref.shape, F32)
        acc_ref[...] = jnp.zeros(acc_ref.shape, F32)

    q = q_ref[...]
    qbd = jnp.concatenate(
        [jnp.where(lane // FOX_HEAD_DIM == hd, q, jnp.zeros_like(q)) for hd in range(FOX_HEADS)], axis=0)

    def head_rows(nc):
        return jnp.concatenate(
            [jnp.broadcast_to(nc[hd:hd + 1, :], (n, nc.shape[1])) for hd in range(FOX_HEADS)], axis=0)

    def update(s, vt):
        m_prev = m_ref[...]
        m_new = jnp.maximum(m_prev, jnp.max(s, axis=-1, keepdims=True))
        alpha = jnp.exp2(m_prev - m_new)
        p = jnp.exp2(s - m_new[:, 0:1])
        l_ref[...] = alpha * l_ref[...] + jnp.sum(p, axis=-1, keepdims=True)
        acc_ref[...] = alpha[:, 0:1] * acc_ref[...] + _dot_nt(p.astype(BF16), vt)
        m_ref[...] = m_new

    update(_dot(qbd, kt_ref[...].astype(BF16)) + head_rows(ncc_ref[...]), vt_ref[...].astype(BF16))

    @pl.when(chunk == pl.num_programs(1) - 1)
    def _():
        r = lax.broadcasted_iota(jnp.int32, (FOX_HEADS * n, n), 0)
        c = lax.broadcasted_iota(jnp.int32, (FOX_HEADS * n, n), 1)
        sn = _dot(qbd, knt_ref[...].astype(BF16)) + head_rows(ncn_ref[:, 0:n])
        update(jnp.where(c <= r % n, sn, -jnp.inf), vnt_ref[...].astype(BF16))
        o_all = acc_ref[...] / l_ref[:, 0:1]
        o = jnp.zeros((n, FOX_WIDTH), F32)
        for hd in range(FOX_HEADS):
            o = o + jnp.where(lane // FOX_HEAD_DIM == hd, o_all[n * hd:n * (hd + 1), :], 0.0)
        o_ref[...] = o.astype(BF16)


def _fox_sample(l, q, knt, vnt, cache_kt, cache_vt, nc):
    nb, n, _ = q.shape
    past = cache_kt.shape[3]
    ck = min(FOX_SAMPLE_CHUNK, past)
    q_spec = pl.BlockSpec((None, n, FOX_WIDTH), lambda b, c: (b, 0, 0))
    new_spec = pl.BlockSpec((None, FOX_WIDTH, n), lambda b, c: (b, 0, 0))
    cache_spec = pl.BlockSpec((None, None, FOX_WIDTH, ck), lambda b, c: (l, b, 0, c))
    rows = FOX_HEADS * n
    return pl.pallas_call(
        _fox_sample_kernel,
        grid=(nb, past // ck),
        in_specs=[q_spec, cache_spec, cache_spec, new_spec, new_spec,
                  pl.BlockSpec((None, FOX_HEADS, ck), lambda b, c: (b, 0, c)),
                  pl.BlockSpec((None, FOX_HEADS, LANES), lambda b, c: (b, 0, past // LANES))],
        out_specs=q_spec,
        out_shape=jax.ShapeDtypeStruct((nb, n, FOX_WIDTH), BF16),
        scratch_shapes=[pltpu.VMEM((rows, LANES), F32), pltpu.VMEM((rows, LANES), F32),
                        pltpu.VMEM((rows, FOX_WIDTH), F32)],
        compiler_params=pltpu.CompilerParams(dimension_semantics=("arbitrary", "arbitrary"),
                                             vmem_limit_bytes=VMEM_LIMIT),
        name="fox_sample",
    )(q, cache_kt, cache_vt, knt, vnt, nc, nc)


def _mlp_kernel(*refs, tmajor, nb):
    if tmajor:
        (x_ref, oret_ref, ofox_ref, oconv_ref, wout_ref, nw_ref, wg_ref, wu_ref, wd_ref, fcw_ref, hist_ref,
         y_ref, st_ref) = refs
    else:
        (x_ref, oret_ref, ofox_ref, oconv_ref, wout_ref, nw_ref, wg_ref, wu_ref, wd_ref, fcw_ref,
         y_ref, st_ref, carry_ref) = refs
    tm = x_ref.shape[0]
    rb = tm if tmajor else min(MLP_ROW_BLOCK, tm)
    for r0 in range(0, tm, rb):
        rows = slice(r0, r0 + rb)
        x1 = (x_ref[rows, :] + _dot(oret_ref[rows, :], wout_ref[0:256, :])
              + _dot(ofox_ref[rows, :], wout_ref[256:768, :]) + _dot(oconv_ref[rows, :], wout_ref[768:1024, :]))
        h2 = _rms(x1, nw_ref[...]).astype(BF16)
        down = None
        for c0, c1 in FF_CHUNKS:
            cols = slice(c0, c1)
            gate_pre = _dot(h2, wg_ref[:, cols])
            if tmajor:
                gate_c, new_hist = _dwconv_tmajor(gate_pre, fcw_ref.at[:, cols], hist_ref[:, cols], nb)
                st_ref[:, cols] = new_hist
            else:
                first = (pl.program_id(1) == 0) if r0 == 0 else None
                gate_c = _dwconv_carry(gate_pre, fcw_ref.at[:, cols], carry_ref.at[:, cols], first)
                if r0 + rb == tm:
                    st_ref[:, cols] = carry_ref[6:8, cols]
            up = _dot(h2, wu_ref[:, cols])
            act = (gate_c / (1.0 + jnp.exp(-gate_c)) * up).astype(BF16)
            part = _dot(act, wd_ref[cols, :])
            down = part if down is None else down + part
        y_ref[rows, :] = x1 + down


def _mlp(l, x, oret, ofox, oconv, prm, *, hist=None, nb=0):
    tmajor = hist is not None
    g, r, _ = x.shape
    tm = min(TOKEN_TILE, r)
    grid = (g, r // tm)
    lsel = lambda b, i: (l, 0, 0)
    tok = lambda width: pl.BlockSpec((None, tm, width), lambda b, i: (b, i, 0))
    wspec = lambda rows, cols: pl.BlockSpec((None, rows, cols), lsel, pipeline_mode=pl.Buffered(1))
    in_specs = [tok(D_MODEL), tok(RET_WIDTH), tok(FOX_WIDTH), tok(CONV_DIM),
                wspec(D_MODEL, D_MODEL), pl.BlockSpec((None, 1, D_MODEL), lsel),
                wspec(D_MODEL, D_FF), wspec(D_MODEL, D_FF), wspec(D_FF, D_MODEL),
                pl.BlockSpec((None, 3, D_FF), lsel)]
    args = [x, oret, ofox, oconv, prm["w_out"], prm["norm_ffn"], prm["w_gate"], prm["w_up"], prm["w_down"],
            prm["ffn_conv_w"]]
    if tmajor:
        in_specs.append(pl.BlockSpec((2 * nb, D_FF), lambda b, i: (0, 0)))
        args.append(hist)
        st_shape = jax.ShapeDtypeStruct((2 * nb, D_FF), F32)
        st_spec = pl.BlockSpec((2 * nb, D_FF), lambda b, i: (0, 0))
        scratch = []
    else:
        st_shape = jax.ShapeDtypeStruct((g, 2, D_FF), F32)
        st_spec = pl.BlockSpec((None, 2, D_FF), lambda b, i: (b, 0, 0))
        scratch = [pltpu.VMEM((SUBLANES, D_FF), F32)]
    return pl.pallas_call(
        functools.partial(_mlp_kernel, tmajor=tmajor, nb=nb),
        grid=grid, in_specs=in_specs,
        out_specs=[tok(D_MODEL), st_spec],
        out_shape=[jax.ShapeDtypeStruct((g, r, D_MODEL), F32), st_shape],
        scratch_shapes=scratch,
        compiler_params=pltpu.CompilerParams(dimension_semantics=("arbitrary", "arbitrary"),
                                             vmem_limit_bytes=VMEM_LIMIT),
        name="outproj_mlp",
    )(*args)


def _permute_w_in(w_in):
    w = w_in.astype(BF16)
    depth, rows, _ = w.shape

    def halves_first(blk):
        return blk.reshape(depth, rows, RET_HEADS, 2, 32).transpose(0, 1, 3, 2, 4).reshape(depth, rows, RET_WIDTH)

    parts = [halves_first(w[..., 0:256]), halves_first(w[..., 256:512]), w[..., 512:2560], w[..., 2568:3336],
             w[..., 2560:2568], jnp.zeros((depth, rows, PROJ_PAD - 3336), BF16)]
    return jnp.concatenate(parts, axis=2)


def _rope_tables(pos):
    half = 32
    inv_freq = ROPE_BASE ** (-jnp.arange(half, dtype=F32) / half)
    ang = pos.astype(F32)[:, None] * inv_freq[None, :]
    return jnp.tile(jnp.cos(ang), (1, RET_HEADS)), jnp.tile(jnp.sin(ang), (1, RET_HEADS))


def _to_bmajor(a, n, nb):
    return a.reshape(n, nb, a.shape[-1]).transpose(1, 0, 2)


def _to_tmajor(a):
    nb, n, c = a.shape
    return a.transpose(1, 0, 2).reshape(1, n * nb, c)


def kernel(x_prompt, x_sample, cache_fox_k, cache_fox_v, cache_fox_logf, state_ret, state_conv, state_ffn_conv,
           norm_mix, w_in, ret_gn_gain, fox_q_gain, fox_k_gain, fox_f_bias, conv_w, w_out, norm_ffn, w_gate,
           w_up, ffn_conv_w, w_down):
    depth = w_in.shape[0]
    b, s, _ = x_prompt.shape
    nb, n, _ = x_sample.shape
    past = cache_fox_k.shape[2]

    w_perm = _permute_w_in(w_in)
    prm = {
        "w_in": w_perm,
        "norm_mix": norm_mix.reshape(depth, 1, D_MODEL),
        "gq": fox_q_gain.reshape(depth, 1, FOX_WIDTH),
        "gk": fox_k_gain.reshape(depth, 1, FOX_WIDTH),
        "fb": jnp.pad(fox_f_bias, ((0, 0), (0, LANES - FOX_HEADS))).reshape(depth, 1, LANES),
        "conv_w": conv_w,
        "w_out": w_out.astype(BF16),
        "norm_ffn": norm_ffn.reshape(depth, 1, D_MODEL),
        "w_gate": w_gate.astype(BF16),
        "w_up": w_up.astype(BF16),
        "w_down": w_down.astype(BF16),
        "ffn_conv_w": ffn_conv_w,
    }
    gain = ret_gn_gain.reshape(depth, 1, RET_WIDTH)
    lane = np.arange(FOX_WIDTH)
    bd = jnp.asarray(lane[:, None] // FOX_HEAD_DIM == lane[None, :] // FOX_HEAD_DIM, BF16)

    cos_p, sin_p = _rope_tables(jnp.arange(s, dtype=jnp.int32))
    pos_s = past + jnp.arange(n, dtype=jnp.int32)
    cos_s, sin_s = _rope_tables(jnp.repeat(pos_s, nb))

    ret_blk = min(RET_BLOCK, s)
    ret_rt = min(TOKEN_TILE, s)
    consts_p = _ret_constants(ret_blk, float(ret_blk))
    samp_rows = LANES
    consts_s = _ret_constants(samp_rows, float(n))
    fox_tile = min(FOX_TILE, s)

    cache_k = jnp.transpose(cache_fox_k, (0, 1, 3, 4, 2)).reshape(depth, nb, FOX_WIDTH, past)
    cache_v = jnp.transpose(cache_fox_v, (0, 1, 3, 4, 2)).reshape(depth, nb, FOX_WIDTH, past)
    cache_lf_t = jnp.transpose(cache_fox_logf, (0, 1, 3, 2))
    c_width = -(-(past + n) // LANES) * LANES

    kbuf = jnp.zeros((depth, b, FOX_WIDTH, s), F32)
    vbuf = jnp.zeros((depth, b, FOX_WIDTH, s), F32)
    zero_state = jnp.zeros((b, RET_WIDTH, RET_WIDTH), F32)

    xp = x_prompt
    xs = jnp.transpose(x_sample, (1, 0, 2)).reshape(1, n * nb, D_MODEL)
    p_lf, p_ret, p_conv, p_ffn = [], [], [], []
    s_k, s_v, s_lf, s_ret, s_conv, s_ffn = [], [], [], [], [], []
    for l in range(depth):
        ret_in, fq, kbuf, vbuf, fkb, fvb, lf, lf_t, oconv, conv_st = _inproj(
            l, xp, prm, cos_p, sin_p, bd, kbuf=kbuf, vbuf=vbuf)
        oret, sbd = _retention(l, ret_in, zero_state, gain, consts_p, ret_blk, ret_rt)
        c_rows = _cumsum_lanes(lf_t)
        nc_rows = c_rows.reshape(b, FOX_HEADS // 2, 2, s // fox_tile, fox_tile).transpose(0, 1, 3, 2, 4)
        ofox = _fox_prompt(fq, fkb, fvb, nc_rows, fox_tile)
        xp, ffn_st = _mlp(l, xp, oret, ofox, oconv, prm)
        p_lf.append(lf)
        p_ret.append(_bd_to_state(sbd))
        p_conv.append(conv_st)
        p_ffn.append(ffn_st)

        hist_c = jnp.transpose(state_conv[l], (1, 0, 2)).reshape(2 * nb, CONV_DIM)
        ret_in, fq, fk32, fv32, fkb, fvb, lf, _, oconv, conv_st = _inproj(
            l, xs, prm, cos_s, sin_s, bd, hist=hist_c, nb=nb)
        ret_b = jnp.pad(_to_bmajor(ret_in, n, nb), ((0, 0), (0, samp_rows - n), (0, 0)))
        oret, sbd = _retention(l, ret_b, _state_to_bd(state_ret[l]), gain, consts_s, samp_rows, samp_rows)
        lf_b = _to_bmajor(lf, n, nb)
        lf_all = jnp.concatenate(
            [cache_lf_t[l], jnp.transpose(lf_b, (0, 2, 1)),
             jnp.zeros((nb, FOX_HEADS, c_width - past - n), F32)], axis=2)
        nc = _cumsum_lanes(lf_all)
        k_new = _to_bmajor(fk32, n, nb)
        v_new = _to_bmajor(fv32, n, nb)
        s_k.append(k_new.reshape(nb, n, FOX_HEADS, FOX_HEAD_DIM))
        s_v.append(v_new.reshape(nb, n, FOX_HEADS, FOX_HEAD_DIM))
        ofox = _fox_sample(l, _to_bmajor(fq, n, nb), jnp.transpose(k_new, (0, 2, 1)),
                           jnp.transpose(v_new, (0, 2, 1)), cache_k, cache_v, nc)
        hist_f = jnp.transpose(state_ffn_conv[l], (1, 0, 2)).reshape(2 * nb, D_FF)
        xs, ffn_st = _mlp(l, xs, _to_tmajor(oret[:, :n]), _to_tmajor(ofox), oconv, prm, hist=hist_f, nb=nb)
        s_lf.append(lf_b)
        s_ret.append(_bd_to_state(sbd))
        s_conv.append(conv_st.reshape(2, nb, CONV_DIM).transpose(1, 0, 2))
        s_ffn.append(ffn_st.reshape(2, nb, D_FF).transpose(1, 0, 2))

    y_sample = xs.reshape(n, nb, D_MODEL).transpose(1, 0, 2)
    stk = lambda ts: jnp.stack(ts, axis=0)
    from_fm = lambda a: jnp.transpose(a.reshape(depth, b, FOX_HEADS, FOX_HEAD_DIM, s), (0, 1, 4, 2, 3))
    return (xp, y_sample,
            from_fm(kbuf), from_fm(vbuf),
            stk(p_lf), stk(p_ret), stk(p_conv), stk(p_ffn),
            stk(s_k), stk(s_v), stk(s_lf), stk(s_ret), stk(s_conv), stk(s_ffn))
```

```python
import functools

import numpy as np
import jax
import jax.numpy as jnp
from jax import lax
from jax.experimental import pallas as pl
from jax.experimental.pallas import tpu as pltpu

F32 = jnp.float32
BF16 = jnp.bfloat16

D_MODEL = 1024
RET_HEADS = 4
RET_WIDTH = 256
FOX_HEADS = 8
FOX_HEAD_DIM = 64
FOX_WIDTH = 512
CONV_DIM = 256
D_FF = 2816
PROJ_PAD = 3456
ROPE_BASE = 10000.0
NORM_EPS = 1e-6
LANES = 128
SUBLANES = 8
VMEM_LIMIT = 56 * 1024 * 1024

TOKEN_TILE = 512
RET_BLOCK = 256
MLP_ROW_BLOCK = 256
FOX_TILE = 512
FOX_SAMPLE_CHUNK = 2048
FOX_VROWS = 80
LOG2E = 1.4426950408889634
FOX_Q_SCALE = FOX_HEAD_DIM ** -0.5 * LOG2E
MXU_DEPTH = 256
FF_CHUNKS = ((0, D_FF),)


def _dot(a, b):
    return jnp.dot(a, b, preferred_element_type=F32)


def _dot_nt(a, b):
    return lax.dot_general(a, b, (((1,), (1,)), ((), ())), preferred_element_type=F32)


def _split2(x):
    hi = x.astype(BF16)
    lo = (x - hi.astype(F32)).astype(BF16)
    return hi, lo


def _group_sum(x, ones_bd):
    hi, lo = _split2(x)
    return _dot(hi, ones_bd) + _dot(lo, ones_bd)


def _rms(x, g):
    ms = jnp.mean(x * x, axis=-1, keepdims=True)
    return x * lax.rsqrt(ms + NORM_EPS) * g


def _dwconv_carry(u, w_ref, carry_ref, first):
    tm = u.shape[0]

    if first is not None:
        @pl.when(first)
        def _():
            carry_ref[...] = jnp.zeros(carry_ref.shape, F32)

    row = lax.broadcasted_iota(jnp.int32, u.shape, 0)
    c6 = carry_ref[6:7, :]
    c7 = carry_ref[7:8, :]
    um1 = jnp.where(row == 0, c7, pltpu.roll(u, 1, axis=0))
    um2 = jnp.where(row == 0, c6, jnp.where(row == 1, c7, pltpu.roll(u, 2, axis=0)))
    y = w_ref[0:1, :] * um2 + w_ref[1:2, :] * um1 + w_ref[2:3, :] * u
    carry_ref[...] = u[tm - SUBLANES:tm, :]
    return y


def _dwconv_tmajor(u, w_ref, hist, nb):
    tm = u.shape[0]
    full = jnp.concatenate([hist, u], axis=0)
    y = (w_ref[0:1, :] * full[0:tm] + w_ref[1:2, :] * full[nb:nb + tm]
         + w_ref[2:3, :] * full[2 * nb:2 * nb + tm])
    return y, full[tm:tm + 2 * nb]


def _inproj_kernel(*refs, tmajor, nb):
    if tmajor:
        (x_ref, nw_ref, w_ref, cos_ref, sin_ref, gq_ref, gk_ref, fb_ref, cw_ref, bd_ref, hist_ref,
         ret_ref, fq_ref, fk32_ref, fv32_ref, fkb_ref, fvb_ref, lf_ref, lft_ref, oc_ref, st_ref) = refs
    else:
        (x_ref, nw_ref, w_ref, cos_ref, sin_ref, gq_ref, gk_ref, fb_ref, cw_ref, bd_ref,
         _, _, ret_ref, fq_ref, fk32_ref, fv32_ref, fkb_ref, fvb_ref, lf_ref, lft_ref, oc_ref, st_ref,
         carry_ref) = refs
    h = _rms(x_ref[...], nw_ref[...]).astype(BF16)

    a = _dot(h, w_ref[:, 0:1024])
    f = _dot(h, w_ref[:, 1024:2560])
    c = _dot(h, w_ref[:, 2560:3328])
    z = _dot(h, w_ref[:, 3328:3456]) + fb_ref[...]

    cos = cos_ref[...]
    sin = sin_ref[...]
    q1, q2, k1, k2 = a[:, 0:128], a[:, 128:256], a[:, 256:384], a[:, 384:512]
    ret_ref[:, 0:128] = (q1 * cos - q2 * sin).astype(BF16)
    ret_ref[:, 128:256] = (q1 * sin + q2 * cos).astype(BF16)
    ret_ref[:, 256:384] = ((k1 * cos - k2 * sin) * 0.125).astype(BF16)
    ret_ref[:, 384:512] = ((k1 * sin + k2 * cos) * 0.125).astype(BF16)
    ret_ref[:, 512:1024] = a[:, 512:1024].astype(BF16)

    fq, fk, fv = f[:, 0:512], f[:, 512:1024], f[:, 1024:1536]
    bd = bd_ref[...]
    inv_d = 1.0 / FOX_HEAD_DIM
    ssq = _dot((fq * fq).astype(BF16), bd)
    fq_ref[...] = (fq * lax.rsqrt(ssq * inv_d + NORM_EPS) * gq_ref[...] * FOX_Q_SCALE).astype(BF16)
    ssk = _group_sum(fk * fk, bd)
    fkn = fk * lax.rsqrt(ssk * inv_d + NORM_EPS) * gk_ref[...]
    fkb_ref[...] = fkn.astype(BF16)
    if tmajor:
        fk32_ref[...] = fkn
        fv32_ref[...] = fv
        fvb_ref[...] = fv.astype(BF16)
    else:
        fk32_ref[...] = fkn.T
        fvt = fv.T
        fv32_ref[...] = fvt
        fvb_ref[...] = fvt.astype(BF16)

    cb, cc, ch = c[:, 0:256], c[:, 256:512], c[:, 512:768]
    u = cc * ch
    if tmajor:
        y, new_hist = _dwconv_tmajor(u, cw_ref, hist_ref[...], nb)
        st_ref[...] = new_hist
    else:
        y = _dwconv_carry(u, cw_ref, carry_ref, pl.program_id(1) == 0)
        st_ref[...] = carry_ref[6:8, :]
    oc_ref[...] = (cb * y).astype(BF16)

    ls = jnp.minimum(z, 0.0) - jnp.log1p(jnp.exp(-jnp.abs(z)))
    lf_ref[...] = ls[:, 0:FOX_HEADS]
    lft_ref[...] = ls.T[0:FOX_HEADS, :]


def _inproj(l, x, prm, cos, sin, bd, *, kbuf=None, vbuf=None, hist=None, nb=0):
    tmajor = hist is not None
    g, r, _ = x.shape
    tm = min(TOKEN_TILE, r)
    grid = (g, r // tm)
    depth = prm["w_in"].shape[0]
    lsel = lambda b, i: (l, 0, 0)
    in_specs = [
        pl.BlockSpec((None, tm, D_MODEL), lambda b, i: (b, i, 0)),
        pl.BlockSpec((None, 1, D_MODEL), lsel),
        pl.BlockSpec((None, D_MODEL, PROJ_PAD), lsel, pipeline_mode=pl.Buffered(1)),
        pl.BlockSpec((tm, LANES), lambda b, i: (i, 0)),
        pl.BlockSpec((tm, LANES), lambda b, i: (i, 0)),
        pl.BlockSpec((None, 1, FOX_WIDTH), lsel),
        pl.BlockSpec((None, 1, FOX_WIDTH), lsel),
        pl.BlockSpec((None, 1, LANES), lsel),
        pl.BlockSpec((None, 3, CONV_DIM), lsel),
        pl.BlockSpec((FOX_WIDTH, FOX_WIDTH), lambda b, i: (0, 0)),
    ]
    args = [x, prm["norm_mix"], prm["w_in"], cos, sin, prm["gq"], prm["gk"], prm["fb"], prm["conv_w"], bd]
    tok = lambda width: pl.BlockSpec((None, tm, width), lambda b, i: (b, i, 0))
    if tmajor:
        in_specs.append(pl.BlockSpec((2 * nb, CONV_DIM), lambda b, i: (0, 0)))
        args.append(hist)
        kv_shape = jax.ShapeDtypeStruct((g, r, FOX_WIDTH), F32)
        kv_spec = tok(FOX_WIDTH)
        vb_shape = jax.ShapeDtypeStruct((g, r, FOX_WIDTH), BF16)
        vb_spec = tok(FOX_WIDTH)
        st_shape = jax.ShapeDtypeStruct((2 * nb, CONV_DIM), F32)
        st_spec = pl.BlockSpec((2 * nb, CONV_DIM), lambda b, i: (0, 0))
        aliases = {}
        scratch = []
    else:
        in_specs += [pl.BlockSpec(memory_space=pl.ANY), pl.BlockSpec(memory_space=pl.ANY)]
        args += [kbuf, vbuf]
        kv_shape = jax.ShapeDtypeStruct((depth, g, FOX_WIDTH, r), F32)
        kv_spec = pl.BlockSpec((None, None, FOX_WIDTH, tm), lambda b, i: (l, b, 0, i))
        vb_shape = jax.ShapeDtypeStruct((g, FOX_WIDTH, r), BF16)
        vb_spec = pl.BlockSpec((None, FOX_WIDTH, tm), lambda b, i: (b, 0, i))
        st_shape = jax.ShapeDtypeStruct((g, 2, CONV_DIM), F32)
        st_spec = pl.BlockSpec((None, 2, CONV_DIM), lambda b, i: (b, 0, 0))
        aliases = {10: 2, 11: 3}
        scratch = [pltpu.VMEM((SUBLANES, CONV_DIM), F32)]
    out_shape = [
        jax.ShapeDtypeStruct((g, r, 1024), BF16),
        jax.ShapeDtypeStruct((g, r, FOX_WIDTH), BF16),
        kv_shape, kv_shape,
        jax.ShapeDtypeStruct((g, r, FOX_WIDTH), BF16),
        vb_shape,
        jax.ShapeDtypeStruct((g, r, FOX_HEADS), F32),
        jax.ShapeDtypeStruct((g, FOX_HEADS, r), F32),
        jax.ShapeDtypeStruct((g, r, CONV_DIM), BF16),
        st_shape,
    ]
    out_specs = [tok(1024), tok(FOX_WIDTH), kv_spec, kv_spec, tok(FOX_WIDTH), vb_spec,
                 tok(FOX_HEADS), pl.BlockSpec((None, FOX_HEADS, tm), lambda b, i: (b, 0, i)),
                 tok(CONV_DIM), st_spec]
    return pl.pallas_call(
        functools.partial(_inproj_kernel, tmajor=tmajor, nb=nb),
        grid=grid, in_specs=in_specs, out_specs=out_specs, out_shape=out_shape,
        scratch_shapes=scratch, input_output_aliases=aliases,
        compiler_params=pltpu.CompilerParams(dimension_semantics=("arbitrary", "arbitrary"),
                                             vmem_limit_bytes=VMEM_LIMIT),
        name="inproj",
    )(*args)


def _ret_kernel(x_ref, s0_ref, gain_ref, dec_ref, qs_ref, ks_ref, gbm_ref, bm_ref, bdv_ref,
                o_ref, sout_ref, s_ref, *, blk):
    @pl.when(pl.program_id(1) == 0)
    def _():
        s_ref[...] = s0_ref[...]

    inv_d = 1.0 / 64.0
    bdv = bdv_ref[...]
    lane = lax.broadcasted_iota(jnp.int32, (1, RET_WIDTH), 1)
    head_k = (lane % 128) // 32
    head_v = lane // 64
    nsb = x_ref.shape[0] // blk
    intra = []
    for sb in range(nsb):
        rows = slice(sb * blk, (sb + 1) * blk)
        q = x_ref[rows, 0:256]
        qst = jnp.concatenate([jnp.where(head_k == hd, q, jnp.zeros_like(q)) for hd in range(RET_HEADS)], axis=0)
        a = (_dot_nt(qst, x_ref[rows, 256:512]) * dec_ref[...]).astype(BF16)
        oa = _dot(a, x_ref[rows, 512:768])
        acc = jnp.where(head_v == 0, oa[0:blk, :], 0.0)
        for hd in range(1, RET_HEADS):
            acc = acc + jnp.where(head_v == hd, oa[hd * blk:(hd + 1) * blk, :], 0.0)
        intra.append(acc)
    for sb in range(nsb):
        rows = slice(sb * blk, (sb + 1) * blk)
        q = x_ref[rows, 0:256]
        k = x_ref[rows, 256:512]
        v = x_ref[rows, 512:768]
        gate = x_ref[rows, 768:1024].astype(F32)
        state = s_ref[...]
        qw = (q.astype(F32) * qs_ref[...]).astype(BF16)
        o = intra[sb] + _dot(qw, state.astype(BF16))
        kw_t = (k.astype(F32) * ks_ref[...]).T.astype(BF16)
        s_ref[...] = gbm_ref[...] * state + bm_ref[...] * _dot(kw_t, v)
        hi, lo = _split2(o)
        mu2 = _dot(jnp.concatenate([hi, lo], axis=0), bdv)
        d = o - (mu2[0:blk] + mu2[blk:2 * blk]) * inv_d
        hi, lo = _split2(d * d)
        var2 = _dot(jnp.concatenate([hi, lo], axis=0), bdv)
        var = (var2[0:blk] + var2[blk:2 * blk]) * inv_d
        on = d * lax.rsqrt(var + NORM_EPS) * gain_ref[...]
        o_ref[rows, :] = (on * (gate / (1.0 + jnp.exp(-gate)))).astype(BF16)
    sout_ref[...] = s_ref[...]


def _ret_constants(blk, chunk_len):
    log_g = jnp.log1p(-jnp.exp2(-5.0 - jnp.arange(RET_HEADS, dtype=F32)))
    lane = np.arange(RET_WIDTH)
    head_k = (lane % 128) // 32
    head_v = lane // 64
    i = jnp.arange(blk, dtype=F32)
    diff = i[:, None] - i[None, :]
    dec = jnp.where(diff[None] >= 0.0, jnp.exp(jnp.maximum(diff, 0.0)[None] * log_g[:, None, None]), 0.0)
    dec = dec.reshape(RET_HEADS * blk, blk)
    lg_k = log_g[head_k]
    qs = jnp.exp((i + 1.0)[:, None] * lg_k[None, :])
    ks = jnp.exp((chunk_len - 1.0 - i)[:, None] * lg_k[None, :])
    gbm = jnp.broadcast_to(jnp.exp(chunk_len * lg_k)[:, None], (RET_WIDTH, RET_WIDTH))
    bm = jnp.asarray(head_k[:, None] == head_v[None, :], F32)
    bdv = jnp.asarray(head_v[:, None] == head_v[None, :], BF16)
    return dec, qs, ks, gbm, bm, bdv


def _retention(l, ret_in, s0, gain, consts, blk, rt):
    g, r, _ = ret_in.shape
    grid = (g, r // rt)
    const2 = lambda shape: pl.BlockSpec(shape, lambda b, i: (0,) * len(shape))
    dec, qs, ks, gbm, bm, bdv = consts
    in_specs = [
        pl.BlockSpec((None, rt, 1024), lambda b, i: (b, i, 0)),
        pl.BlockSpec((None, RET_WIDTH, RET_WIDTH), lambda b, i: (b, 0, 0)),
        pl.BlockSpec((None, 1, RET_WIDTH), lambda b, i: (l, 0, 0)),
        const2(dec.shape), const2(qs.shape), const2(ks.shape), const2(gbm.shape), const2(bm.shape),
        const2(bdv.shape),
    ]
    return pl.pallas_call(
        functools.partial(_ret_kernel, blk=blk),
        grid=grid, in_specs=in_specs,
        out_specs=[pl.BlockSpec((None, rt, RET_WIDTH), lambda b, i: (b, i, 0)),
                   pl.BlockSpec((None, RET_WIDTH, RET_WIDTH), lambda b, i: (b, 0, 0))],
        out_shape=[jax.ShapeDtypeStruct((g, r, RET_WIDTH), BF16),
                   jax.ShapeDtypeStruct((g, RET_WIDTH, RET_WIDTH), F32)],
        scratch_shapes=[pltpu.VMEM((RET_WIDTH, RET_WIDTH), F32)],
        compiler_params=pltpu.CompilerParams(dimension_semantics=("arbitrary", "arbitrary"),
                                             vmem_limit_bytes=VMEM_LIMIT),
        name="retention",
    )(ret_in, s0, gain, dec, qs, ks, gbm, bm, bdv)


def _state_to_bd(st):
    g = st.shape[0]
    t = jnp.einsum("bhpie,hg->bphige", st.reshape(g, RET_HEADS, 2, 32, 64), jnp.eye(RET_HEADS, dtype=st.dtype))
    return t.reshape(g, RET_WIDTH, RET_WIDTH)


def _bd_to_state(sbd):
    g = sbd.shape[0]
    t = jnp.einsum("bphihe->bhpie", sbd.reshape(g, 2, RET_HEADS, 32, RET_HEADS, 64))
    return t.reshape(g, RET_HEADS, 64, 64)


def _cumsum_kernel(x_ref, u_ref, o_ref):
    nc = x_ref.shape[1] // LANES
    xs = jnp.concatenate([x_ref[:, LANES * j:LANES * (j + 1)] for j in range(nc)], axis=0)
    x1 = xs.astype(BF16)
    r1 = xs - x1.astype(F32)
    x2 = r1.astype(BF16)
    x3 = (r1 - x2.astype(F32)).astype(BF16)
    tri = u_ref[...]
    loc = _dot(x1, tri) + _dot(x2, tri) + _dot(x3, tri)
    off = jnp.zeros((SUBLANES, 1), F32)
    for j in range(nc):
        lj = loc[SUBLANES * j:SUBLANES * (j + 1), :]
        o_ref[:, LANES * j:LANES * (j + 1)] = (lj + off) * (-LOG2E)
        off = off + lj[:, LANES - 1:LANES]


def _cumsum_lanes(x):
    n, h, w = x.shape
    tri = jnp.asarray(np.arange(LANES)[:, None] <= np.arange(LANES)[None, :], BF16)
    return pl.pallas_call(
        _cumsum_kernel,
        grid=(n,),
        in_specs=[pl.BlockSpec((None, h, w), lambda b: (b, 0, 0)),
                  pl.BlockSpec((LANES, LANES), lambda b: (0, 0))],
        out_specs=pl.BlockSpec((None, h, w), lambda b: (b, 0, 0)),
        out_shape=jax.ShapeDtypeStruct((n, h, w), F32),
        compiler_params=pltpu.CompilerParams(dimension_semantics=("arbitrary",)),
        name="forget_cumsum",
    )(x, tri)


def _fox_prompt_kernel(q_ref, k_ref, vt_ref, nc_ref, o_ref, va_ref, ncol_ref, qh_ref, sd_ref, sa_ref, sb_ref,
                       m_ref, acc_ref, *, tile):
    nq = q_ref.shape[0] // tile
    lane = lax.broadcasted_iota(jnp.int32, (1, LANES), 1)
    half = tile // 2
    krow = lax.broadcasted_iota(jnp.int32, (half, tile), 0)
    qcol = lax.broadcasted_iota(jnp.int32, (half, tile), 1)
    brow = lax.broadcasted_iota(jnp.int32, (half, half), 0)
    bcol = lax.broadcasted_iota(jnp.int32, (half, half), 1)
    for j in range(nq):
        for hd in range(2):
            va_ref[hd, j, 0:FOX_HEAD_DIM, :] = vt_ref[FOX_HEAD_DIM * hd:FOX_HEAD_DIM * (hd + 1),
                                                       tile * j:tile * (j + 1)]
            va_ref[hd, j, FOX_HEAD_DIM:FOX_VROWS, :] = jnp.ones((FOX_VROWS - FOX_HEAD_DIM, tile), BF16)
        rows8 = jnp.concatenate([nc_ref[j], jnp.zeros((SUBLANES - 2, tile), F32)], axis=0)
        ncol_ref[j] = rows8.T

    def set_q(slot, qi):
        q = q_ref[pl.ds(pl.multiple_of(qi * tile, tile), tile), :]
        for hd in range(2):
            qh_ref[slot, hd] = jnp.where(lane // FOX_HEAD_DIM == hd, q, jnp.zeros_like(q))

    def park(dst_ref, qslot, kj):
        kt = k_ref[pl.ds(pl.multiple_of(kj * tile, tile), tile), :]
        ncol = ncol_ref[kj]
        for hd in range(2):
            dst_ref[hd] = _dot_nt(kt, qh_ref[qslot, hd]) + ncol[:, hd:hd + 1]

    def consume(src_ref, kj):
        for hd in range(2):
            s = src_ref[hd]
            m_prev = m_ref[hd]
            m_new = jnp.maximum(m_prev, jnp.max(s, axis=0, keepdims=True))
            alpha = jnp.exp2(m_prev - m_new)
            p = jnp.exp2(s - m_new).astype(BF16)
            acc_ref[hd] = alpha * acc_ref[hd] + _dot(va_ref[hd, kj], p)
            m_ref[hd] = m_new

    def park_diag(qslot, kj):
        k0 = pl.multiple_of(kj * tile, tile)
        for hd in range(2):
            top = (_dot_nt(k_ref[pl.ds(k0, half), :], qh_ref[qslot, hd])
                   + ncol_ref[kj, 0:half, :][:, hd:hd + 1])
            sd_ref[hd, 0:half, :] = jnp.where(krow <= qcol, top, -jnp.inf)
            bot = (_dot_nt(k_ref[pl.ds(pl.multiple_of(k0 + half, half), half), :], qh_ref[qslot, hd, half:tile, :])
                   + ncol_ref[kj, half:tile, :][:, hd:hd + 1])
            sd_ref[hd, half:tile, half:tile] = jnp.where(brow <= bcol, bot, -jnp.inf)

    def consume_diag(kj):
        for hd in range(2):
            top_l = sd_ref[hd, 0:half, 0:half]
            top_r = sd_ref[hd, 0:half, half:tile]
            bot = sd_ref[hd, half:tile, half:tile]
            m_l = jnp.max(top_l, axis=0, keepdims=True)
            m_r = jnp.maximum(jnp.max(top_r, axis=0, keepdims=True), jnp.max(bot, axis=0, keepdims=True))
            va_a = va_ref[hd, kj, :, 0:half]
            acc_ref[hd, :, 0:half] = _dot(va_a, jnp.exp2(top_l - m_l).astype(BF16))
            acc_ref[hd, :, half:tile] = (_dot(va_a, jnp.exp2(top_r - m_r).astype(BF16))
                                         + _dot(va_ref[hd, kj, :, half:tile], jnp.exp2(bot - m_r).astype(BF16)))
            m_ref[hd, :, 0:half] = m_l
            m_ref[hd, :, half:tile] = m_r

    def finish(qi):
        ot = jnp.concatenate(
            [acc_ref[hd, 0:FOX_HEAD_DIM, :] / acc_ref[hd, FOX_HEAD_DIM:FOX_HEAD_DIM + 1, :] for hd in range(2)],
            axis=0)
        o_ref[pl.ds(pl.multiple_of(qi * tile, tile), tile), :] = ot.T.astype(BF16)

    def park_next_diag(qi):
        nxt = jnp.minimum(qi + 1, nq - 1)
        set_q(1 - qi % 2, nxt)
        park_diag(1 - qi % 2, nxt)

    set_q(0, 0)
    park_diag(0, 0)

    def q_body(qi, carry):
        qs = qi % 2

        def head(qi):
            park(sa_ref, qs, 0)
            consume_diag(qi)

            def kv_pair(jp, c2):
                j = 2 * jp + 1
                park(sb_ref, qs, j)
                consume(sa_ref, j - 1)
                park(sa_ref, qs, j + 1)
                consume(sb_ref, j)
                return c2

            lax.fori_loop(0, (qi - 1) // 2, kv_pair, 0)

        @pl.when(qi == 0)
        def _():
            consume_diag(0)
            park_next_diag(qi)
            finish(qi)

        @pl.when(jnp.logical_and(qi > 0, qi % 2 == 0))
        def _():
            head(qi)
            park(sb_ref, qs, qi - 1)
            consume(sa_ref, qi - 2)
            park_next_diag(qi)
            consume(sb_ref, qi - 1)
            finish(qi)

        @pl.when(qi % 2 == 1)
        def _():
            head(qi)
            park_next_diag(qi)
            consume(sa_ref, qi - 1)
            finish(qi)

        return carry

    lax.fori_loop(0, nq, q_body, 0)


def _fox_prompt(q, k, vt, nc_rows, tile):
    b, s, _ = q.shape
    nt = s // tile
    qk_spec = pl.BlockSpec((None, s, LANES), lambda bi, hp: (bi, 0, hp))
    return pl.pallas_call(
        functools.partial(_fox_prompt_kernel, tile=tile),
        grid=(b, FOX_HEADS // 2),
        in_specs=[qk_spec, qk_spec, pl.BlockSpec((None, LANES, s), lambda bi, hp: (bi, hp, 0)),
                  pl.BlockSpec((None, None, nt, 2, tile), lambda bi, hp: (bi, hp, 0, 0, 0))],
        out_specs=qk_spec,
        out_shape=jax.ShapeDtypeStruct((b, s, FOX_WIDTH), BF16),
        scratch_shapes=[pltpu.VMEM((2, nt, FOX_VROWS, tile), BF16), pltpu.VMEM((nt, tile, SUBLANES), F32),
                        pltpu.VMEM((2, 2, tile, LANES), BF16),
                        pltpu.VMEM((2, tile, tile), F32), pltpu.VMEM((2, tile, tile), F32),
                        pltpu.VMEM((2, tile, tile), F32),
                        pltpu.VMEM((2, 1, tile), F32), pltpu.VMEM((2, FOX_VROWS, tile), F32)],
        compiler_params=pltpu.CompilerParams(dimension_semantics=("arbitrary", "arbitrary"),
                                             vmem_limit_bytes=VMEM_LIMIT),
        name="fox_prompt",
    )(q, k, vt, nc_rows)


def _fox_sample_kernel(q_ref, kt_ref, vt_ref, knt_ref, vnt_ref, ncc_ref, ncn_ref, o_ref,
                       m_ref, l_ref, acc_ref):
    chunk = pl.program_id(1)
    n = q_ref.shape[0]
    lane = lax.broadcasted_iota(jnp.int32, (1, FOX_WIDTH), 1)

    @pl.when(chunk == 0)
    def _():
        m_ref[...] = jnp.full(m_ref.shape, -jnp.inf, F32)
        l_ref[...] = jnp.zeros(l_ref.shape, F32)
        acc_ref[...] = jnp.zeros(acc_ref.shape, F32)

    q = q_ref[...]
    qbd = jnp.concatenate(
        [jnp.where(lane // FOX_HEAD_DIM == hd, q, jnp.zeros_like(q)) for hd in range(FOX_HEADS)], axis=0)

    def head_rows(nc):
        return jnp.concatenate(
            [jnp.broadcast_to(nc[hd:hd + 1, :], (n, nc.shape[1])) for hd in range(FOX_HEADS)], axis=0)

    def update(s, vt):
        m_prev = m_ref[...]
        m_new = jnp.maximum(m_prev, jnp.max(s, axis=-1, keepdims=True))
        alpha = jnp.exp2(m_prev - m_new)
        p = jnp.exp2(s - m_new[:, 0:1])
        l_ref[...] = alpha * l_ref[...] + jnp.sum(p, axis=-1, keepdims=True)
        acc_ref[...] = alpha[:, 0:1] * acc_ref[...] + _dot_nt(p.astype(BF16), vt)
        m_ref[...] = m_new

    update(_dot(qbd, kt_ref[...].astype(BF16)) + head_rows(ncc_ref[...]), vt_ref[...].astype(BF16))

    @pl.when(chunk == pl.num_programs(1) - 1)
    def _():
        r = lax.broadcasted_iota(jnp.int32, (FOX_HEADS * n, n), 0)
        c = lax.broadcasted_iota(jnp.int32, (FOX_HEADS * n, n), 1)
        sn = _dot(qbd, knt_ref[...].astype(BF16)) + head_rows(ncn_ref[:, 0:n])
        update(jnp.where(c <= r % n, sn, -jnp.inf), vnt_ref[...].astype(BF16))
        o_all = acc_ref[...] / l_ref[:, 0:1]
        o = jnp.zeros((n, FOX_WIDTH), F32)
        for hd in range(FOX_HEADS):
            o = o + jnp.where(lane // FOX_HEAD_DIM == hd, o_all[n * hd:n * (hd + 1), :], 0.0)
        o_ref[...] = o.astype(BF16)


def _fox_sample(l, q, knt, vnt, cache_kt, cache_vt, nc):
    nb, n, _ = q.shape
    past = cache_kt.shape[3]
    ck = min(FOX_SAMPLE_CHUNK, past)
    q_spec = pl.BlockSpec((None, n, FOX_WIDTH), lambda b, c: (b, 0, 0))
    new_spec = pl.BlockSpec((None, FOX_WIDTH, n), lambda b, c: (b, 0, 0))
    cache_spec = pl.BlockSpec((None, None, FOX_WIDTH, ck), lambda b, c: (l, b, 0, c))
    rows = FOX_HEADS * n
    return pl.pallas_call(
        _fox_sample_kernel,
        grid=(nb, past // ck),
        in_specs=[q_spec, cache_spec, cache_spec, new_spec, new_spec,
                  pl.BlockSpec((None, FOX_HEADS, ck), lambda b, c: (b, 0, c)),
                  pl.BlockSpec((None, FOX_HEADS, LANES), lambda b, c: (b, 0, past // LANES))],
        out_specs=q_spec,
        out_shape=jax.ShapeDtypeStruct((nb, n, FOX_WIDTH), BF16),
        scratch_shapes=[pltpu.VMEM((rows, LANES), F32), pltpu.VMEM((rows, LANES), F32),
                        pltpu.VMEM((rows, FOX_WIDTH), F32)],
        compiler_params=pltpu.CompilerParams(dimension_semantics=("arbitrary", "arbitrary"),
                                             vmem_limit_bytes=VMEM_LIMIT),
        name="fox_sample",
    )(q, cache_kt, cache_vt, knt, vnt, nc, nc)


def _mlp_kernel(*refs, tmajor, nb):
    if tmajor:
        (x_ref, oret_ref, ofox_ref, oconv_ref, wout_ref, nw_ref, wg_ref, wu_ref, wd_ref, fcw_ref, hist_ref,
         y_ref, st_ref) = refs
    else:
        (x_ref, oret_ref, ofox_ref, oconv_ref, wout_ref, nw_ref, wg_ref, wu_ref, wd_ref, fcw_ref,
         y_ref, st_ref, carry_ref) = refs
    tm = x_ref.shape[0]
    rb = tm if tmajor else min(MLP_ROW_BLOCK, tm)
    for r0 in range(0, tm, rb):
        rows = slice(r0, r0 + rb)
        x1 = (x_ref[rows, :] + _dot(oret_ref[rows, :], wout_ref[0:256, :])
              + _dot(ofox_ref[rows, :], wout_ref[256:768, :]) + _dot(oconv_ref[rows, :], wout_ref[768:1024, :]))
        h2 = _rms(x1, nw_ref[...]).astype(BF16)
        down = None
        for c0, c1 in FF_CHUNKS:
            cols = slice(c0, c1)
            gate_pre = _dot(h2, wg_ref[:, cols])
            if tmajor:
                gate_c, new_hist = _dwconv_tmajor(gate_pre, fcw_ref.at[:, cols], hist_ref[:, cols], nb)
                st_ref[:, cols] = new_hist
            else:
                first = (pl.program_id(1) == 0) if r0 == 0 else None
                gate_c = _dwconv_carry(gate_pre, fcw_ref.at[:, cols], carry_ref.at[:, cols], first)
                if r0 + rb == tm:
                    st_ref[:, cols] = carry_ref[6:8, cols]
            up = _dot(h2, wu_ref[:, cols])
            act = (gate_c / (1.0 + jnp.exp(-gate_c)) * up).astype(BF16)
            part = _dot(act, wd_ref[cols, :])
            down = part if down is None else down + part
        y_ref[rows, :] = x1 + down


def _mlp(l, x, oret, ofox, oconv, prm, *, hist=None, nb=0):
    tmajor = hist is not None
    g, r, _ = x.shape
    tm = min(TOKEN_TILE, r)
    grid = (g, r // tm)
    lsel = lambda b, i: (l, 0, 0)
    tok = lambda width: pl.BlockSpec((None, tm, width), lambda b, i: (b, i, 0))
    wspec = lambda rows, cols: pl.BlockSpec((None, rows, cols), lsel, pipeline_mode=pl.Buffered(1))
    in_specs = [tok(D_MODEL), tok(RET_WIDTH), tok(FOX_WIDTH), tok(CONV_DIM),
                wspec(D_MODEL, D_MODEL), pl.BlockSpec((None, 1, D_MODEL), lsel),
                wspec(D_MODEL, D_FF), wspec(D_MODEL, D_FF), wspec(D_FF, D_MODEL),
                pl.BlockSpec((None, 3, D_FF), lsel)]
    args = [x, oret, ofox, oconv, prm["w_out"], prm["norm_ffn"], prm["w_gate"], prm["w_up"], prm["w_down"],
            prm["ffn_conv_w"]]
    if tmajor:
        in_specs.append(pl.BlockSpec((2 * nb, D_FF), lambda b, i: (0, 0)))
        args.append(hist)
        st_shape = jax.ShapeDtypeStruct((2 * nb, D_FF), F32)
        st_spec = pl.BlockSpec((2 * nb, D_FF), lambda b, i: (0, 0))
        scratch = []
    else:
        st_shape = jax.ShapeDtypeStruct((g, 2, D_FF), F32)
        st_spec = pl.BlockSpec((None, 2, D_FF), lambda b, i: (b, 0, 0))
        scratch = [pltpu.VMEM((SUBLANES, D_FF), F32)]
    return pl.pallas_call(
        functools.partial(_mlp_kernel, tmajor=tmajor, nb=nb),
        grid=grid, in_specs=in_specs,
        out_specs=[tok(D_MODEL), st_spec],
        out_shape=[jax.ShapeDtypeStruct((g, r, D_MODEL), F32), st_shape],
        scratch_shapes=scratch,
        compiler_params=pltpu.CompilerParams(dimension_semantics=("arbitrary", "arbitrary"),
                                             vmem_limit_bytes=VMEM_LIMIT),
        name="outproj_mlp",
    )(*args)


def _permute_w_in(w_in):
    w = w_in.astype(BF16)
    depth, rows, _ = w.shape

    def halves_first(blk):
        return blk.reshape(depth, rows, RET_HEADS, 2, 32).transpose(0, 1, 3, 2, 4).reshape(depth, rows, RET_WIDTH)

    parts = [halves_first(w[..., 0:256]), halves_first(w[..., 256:512]), w[..., 512:2560], w[..., 2568:3336],
             w[..., 2560:2568], jnp.zeros((depth, rows, PROJ_PAD - 3336), BF16)]
    return jnp.concatenate(parts, axis=2)


def _rope_tables(pos):
    half = 32
    inv_freq = ROPE_BASE ** (-jnp.arange(half, dtype=F32) / half)
    ang = pos.astype(F32)[:, None] * inv_freq[None, :]
    return jnp.tile(jnp.cos(ang), (1, RET_HEADS)), jnp.tile(jnp.sin(ang), (1, RET_HEADS))


def _to_bmajor(a, n, nb):
    return a.reshape(n, nb, a.shape[-1]).transpose(1, 0, 2)


def _to_tmajor(a):
    nb, n, c = a.shape
    return a.transpose(1, 0, 2).reshape(1, n * nb, c)


def kernel(x_prompt, x_sample, cache_fox_k, cache_fox_v, cache_fox_logf, state_ret, state_conv, state_ffn_conv,
           norm_mix, w_in, ret_gn_gain, fox_q_gain, fox_k_gain, fox_f_bias, conv_w, w_out, norm_ffn, w_gate,
           w_up, ffn_conv_w, w_down):
    depth = w_in.shape[0]
    b, s, _ = x_prompt.shape
    nb, n, _ = x_sample.shape
    past = cache_fox_k.shape[2]

    w_perm = _permute_w_in(w_in)
    prm = {
        "w_in": w_perm,
        "norm_mix": norm_mix.reshape(depth, 1, D_MODEL),
        "gq": fox_q_gain.reshape(depth, 1, FOX_WIDTH),
        "gk": fox_k_gain.reshape(depth, 1, FOX_WIDTH),
        "fb": jnp.pad(fox_f_bias, ((0, 0), (0, LANES - FOX_HEADS))).reshape(depth, 1, LANES),
        "conv_w": conv_w,
        "w_out": w_out.astype(BF16),
        "norm_ffn": norm_ffn.reshape(depth, 1, D_MODEL),
        "w_gate": w_gate.astype(BF16),
        "w_up": w_up.astype(BF16),
        "w_down": w_down.astype(BF16),
        "ffn_conv_w": ffn_conv_w,
    }
    gain = ret_gn_gain.reshape(depth, 1, RET_WIDTH)
    lane = np.arange(FOX_WIDTH)
    bd = jnp.asarray(lane[:, None] // FOX_HEAD_DIM == lane[None, :] // FOX_HEAD_DIM, BF16)

    cos_p, sin_p = _rope_tables(jnp.arange(s, dtype=jnp.int32))
    pos_s = past + jnp.arange(n, dtype=jnp.int32)
    cos_s, sin_s = _rope_tables(jnp.repeat(pos_s, nb))

    ret_blk = min(RET_BLOCK, s)
    ret_rt = min(TOKEN_TILE, s)
    consts_p = _ret_constants(ret_blk, float(ret_blk))
    samp_rows = LANES
    consts_s = _ret_constants(samp_rows, float(n))
    fox_tile = min(FOX_TILE, s)

    cache_k = jnp.transpose(cache_fox_k, (0, 1, 3, 4, 2)).reshape(depth, nb, FOX_WIDTH, past)
    cache_v = jnp.transpose(cache_fox_v, (0, 1, 3, 4, 2)).reshape(depth, nb, FOX_WIDTH, past)
    cache_lf_t = jnp.transpose(cache_fox_logf, (0, 1, 3, 2))
    c_width = -(-(past + n) // LANES) * LANES

    kbuf = jnp.zeros((depth, b, FOX_WIDTH, s), F32)
    vbuf = jnp.zeros((depth, b, FOX_WIDTH, s), F32)
    zero_state = jnp.zeros((b, RET_WIDTH, RET_WIDTH), F32)
    hist_c_all = jnp.transpose(state_conv, (0, 2, 1, 3)).reshape(depth, 2 * nb, CONV_DIM)
    hist_f_all = jnp.transpose(state_ffn_conv, (0, 2, 1, 3)).reshape(depth, 2 * nb, D_FF)
    state_bd_all = _state_to_bd(state_ret.reshape(depth * nb, RET_HEADS, 64, 64)).reshape(
        depth, nb, RET_WIDTH, RET_WIDTH)

    xp = x_prompt
    xs = jnp.transpose(x_sample, (1, 0, 2)).reshape(1, n * nb, D_MODEL)
    p_lf, p_ret, p_conv, p_ffn = [], [], [], []
    s_k, s_v, s_lf, s_ret, s_conv, s_ffn = [], [], [], [], [], []
    for l in range(depth):
        ret_in, fq, kbuf, vbuf, fkb, fvb, lf, lf_t, oconv, conv_st = _inproj(
            l, xp, prm, cos_p, sin_p, bd, kbuf=kbuf, vbuf=vbuf)
        oret, sbd = _retention(l, ret_in, zero_state, gain, consts_p, ret_blk, ret_rt)
        c_rows = _cumsum_lanes(lf_t)
        nc_rows = c_rows.reshape(b, FOX_HEADS // 2, 2, s // fox_tile, fox_tile).transpose(0, 1, 3, 2, 4)
        ofox = _fox_prompt(fq, fkb, fvb, nc_rows, fox_tile)
        xp, ffn_st = _mlp(l, xp, oret, ofox, oconv, prm)
        p_lf.append(lf)
        p_ret.append(sbd)
        p_conv.append(conv_st)
        p_ffn.append(ffn_st)

        hist_c = hist_c_all[l]
        ret_in, fq, fk32, fv32, fkb, fvb, lf, _, oconv, conv_st = _inproj(
            l, xs, prm, cos_s, sin_s, bd, hist=hist_c, nb=nb)
        ret_b = jnp.pad(_to_bmajor(ret_in, n, nb), ((0, 0), (0, samp_rows - n), (0, 0)))
        oret, sbd = _retention(l, ret_b, state_bd_all[l], gain, consts_s, samp_rows, samp_rows)
        lf_b = _to_bmajor(lf, n, nb)
        lf_all = jnp.concatenate(
            [cache_lf_t[l], jnp.transpose(lf_b, (0, 2, 1)),
             jnp.zeros((nb, FOX_HEADS, c_width - past - n), F32)], axis=2)
        nc = _cumsum_lanes(lf_all)
        k_new = _to_bmajor(fk32, n, nb)
        v_new = _to_bmajor(fv32, n, nb)
        s_k.append(k_new.reshape(nb, n, FOX_HEADS, FOX_HEAD_DIM))
        s_v.append(v_new.reshape(nb, n, FOX_HEADS, FOX_HEAD_DIM))
        ofox = _fox_sample(l, _to_bmajor(fq, n, nb), jnp.transpose(k_new, (0, 2, 1)),
                           jnp.transpose(v_new, (0, 2, 1)), cache_k, cache_v, nc)
        hist_f = hist_f_all[l]
        xs, ffn_st = _mlp(l, xs, _to_tmajor(oret[:, :n]), _to_tmajor(ofox), oconv, prm, hist=hist_f, nb=nb)
        s_lf.append(lf_b)
        s_ret.append(sbd)
        s_conv.append(conv_st)
        s_ffn.append(ffn_st)

    y_sample = xs.reshape(n, nb, D_MODEL).transpose(1, 0, 2)
    stk = lambda ts: jnp.stack(ts, axis=0)
    ret_states = lambda ts, g: _bd_to_state(stk(ts).reshape(depth * g, RET_WIDTH, RET_WIDTH)).reshape(
        depth, g, RET_HEADS, 64, 64)
    conv_states = lambda ts, c: stk(ts).reshape(depth, 2, nb, c).transpose(0, 2, 1, 3)
    from_fm = lambda a: jnp.transpose(a.reshape(depth, b, FOX_HEADS, FOX_HEAD_DIM, s), (0, 1, 4, 2, 3))
    return (xp, y_sample,
            from_fm(kbuf), from_fm(vbuf),
            stk(p_lf), ret_states(p_ret, b), stk(p_conv), stk(p_ffn),
            stk(s_k), stk(s_v), stk(s_lf), ret_states(s_ret, nb), conv_states(s_conv, CONV_DIM),
            conv_states(s_ffn, D_FF))
```

```python
import functools

import numpy as np
import jax
import jax.numpy as jnp
from jax import lax
from jax.experimental import pallas as pl
from jax.experimental.pallas import tpu as pltpu

F32 = jnp.float32
BF16 = jnp.bfloat16

D_MODEL = 1024
RET_HEADS = 4
RET_WIDTH = 256
FOX_HEADS = 8
FOX_HEAD_DIM = 64
FOX_WIDTH = 512
CONV_DIM = 256
D_FF = 2816
PROJ_PAD = 3456
ROPE_BASE = 10000.0
NORM_EPS = 1e-6
LANES = 128
SUBLANES = 8
VMEM_LIMIT = 56 * 1024 * 1024

TOKEN_TILE = 512
RET_BLOCK = 256
MLP_ROW_BLOCK = 256
FOX_TILE = 512
FOX_SAMPLE_CHUNK = 2048
FOX_VROWS = 80
LOG2E = 1.4426950408889634
FOX_Q_SCALE = FOX_HEAD_DIM ** -0.5 * LOG2E
MXU_DEPTH = 256
FF_CHUNKS = ((0, D_FF),)


def _dot(a, b):
    return jnp.dot(a, b, preferred_element_type=F32)


def _dot_nt(a, b):
    return lax.dot_general(a, b, (((1,), (1,)), ((), ())), preferred_element_type=F32)


def _split2(x):
    hi = x.astype(BF16)
    lo = (x - hi.astype(F32)).astype(BF16)
    return hi, lo


def _group_sum(x, ones_bd):
    hi, lo = _split2(x)
    return _dot(hi, ones_bd) + _dot(lo, ones_bd)


def _rms(x, g):
    ms = jnp.mean(x * x, axis=-1, keepdims=True)
    return x * lax.rsqrt(ms + NORM_EPS) * g


def _dwconv_carry(u, w_ref, carry_ref, first):
    tm = u.shape[0]

    if first is not None:
        @pl.when(first)
        def _():
            carry_ref[...] = jnp.zeros(carry_ref.shape, F32)

    row = lax.broadcasted_iota(jnp.int32, u.shape, 0)
    c6 = carry_ref[6:7, :]
    c7 = carry_ref[7:8, :]
    um1 = jnp.where(row == 0, c7, pltpu.roll(u, 1, axis=0))
    um2 = jnp.where(row == 0, c6, jnp.where(row == 1, c7, pltpu.roll(u, 2, axis=0)))
    y = w_ref[0:1, :] * um2 + w_ref[1:2, :] * um1 + w_ref[2:3, :] * u
    carry_ref[...] = u[tm - SUBLANES:tm, :]
    return y


def _dwconv_tmajor(u, w_ref, hist, nb):
    tm = u.shape[0]
    full = jnp.concatenate([hist, u], axis=0)
    y = (w_ref[0:1, :] * full[0:tm] + w_ref[1:2, :] * full[nb:nb + tm]
         + w_ref[2:3, :] * full[2 * nb:2 * nb + tm])
    return y, full[tm:tm + 2 * nb]


def _inproj_kernel(*refs, tmajor, nb):
    if tmajor:
        (x_ref, nw_ref, w_ref, cos_ref, sin_ref, gq_ref, gk_ref, fb_ref, cw_ref, bd_ref, hist_ref,
         ret_ref, fq_ref, fk32_ref, fv32_ref, fkb_ref, fvb_ref, lf_ref, lft_ref, oc_ref, st_ref) = refs
    else:
        (x_ref, nw_ref, w_ref, cos_ref, sin_ref, gq_ref, gk_ref, fb_ref, cw_ref, bd_ref,
         _, _, ret_ref, fq_ref, fk32_ref, fv32_ref, fkb_ref, fvb_ref, lf_ref, lft_ref, oc_ref, st_ref,
         carry_ref) = refs
    h = _rms(x_ref[...], nw_ref[...]).astype(BF16)

    a = _dot(h, w_ref[:, 0:1024])
    f = _dot(h, w_ref[:, 1024:2560])
    c = _dot(h, w_ref[:, 2560:3328])
    z = _dot(h, w_ref[:, 3328:3456]) + fb_ref[...]

    cos = cos_ref[...]
    sin = sin_ref[...]
    q1, q2, k1, k2 = a[:, 0:128], a[:, 128:256], a[:, 256:384], a[:, 384:512]
    ret_ref[:, 0:128] = (q1 * cos - q2 * sin).astype(BF16)
    ret_ref[:, 128:256] = (q1 * sin + q2 * cos).astype(BF16)
    ret_ref[:, 256:384] = ((k1 * cos - k2 * sin) * 0.125).astype(BF16)
    ret_ref[:, 384:512] = ((k1 * sin + k2 * cos) * 0.125).astype(BF16)
    ret_ref[:, 512:1024] = a[:, 512:1024].astype(BF16)

    fq, fk, fv = f[:, 0:512], f[:, 512:1024], f[:, 1024:1536]
    bd = bd_ref[...]
    inv_d = 1.0 / FOX_HEAD_DIM
    ssq = _dot((fq * fq).astype(BF16), bd)
    fq_ref[...] = (fq * lax.rsqrt(ssq * inv_d + NORM_EPS) * gq_ref[...] * FOX_Q_SCALE).astype(BF16)
    ssk = _group_sum(fk * fk, bd)
    fkn = fk * lax.rsqrt(ssk * inv_d + NORM_EPS) * gk_ref[...]
    fkb_ref[...] = fkn.astype(BF16)
    if tmajor:
        fk32_ref[...] = fkn
        fv32_ref[...] = fv
        fvb_ref[...] = fv.astype(BF16)
    else:
        fk32_ref[...] = fkn.T
        fvt = fv.T
        fv32_ref[...] = fvt
        fvb_ref[...] = fvt.astype(BF16)

    cb, cc, ch = c[:, 0:256], c[:, 256:512], c[:, 512:768]
    u = cc * ch
    if tmajor:
        y, new_hist = _dwconv_tmajor(u, cw_ref, hist_ref[...], nb)
        st_ref[...] = new_hist
    else:
        y = _dwconv_carry(u, cw_ref, carry_ref, pl.program_id(1) == 0)
        st_ref[...] = carry_ref[6:8, :]
    oc_ref[...] = (cb * y).astype(BF16)

    ls = jnp.minimum(z, 0.0) - jnp.log1p(jnp.exp(-jnp.abs(z)))
    lf_ref[...] = ls[:, 0:FOX_HEADS]
    lft_ref[...] = ls.T[0:FOX_HEADS, :]


def _inproj(l, x, prm, cos, sin, bd, *, kbuf=None, vbuf=None, hist=None, nb=0):
    tmajor = hist is not None
    g, r, _ = x.shape
    tm = min(TOKEN_TILE, r)
    grid = (g, r // tm)
    depth = prm["w_in"].shape[0]
    lsel = lambda b, i: (l, 0, 0)
    in_specs = [
        pl.BlockSpec((None, tm, D_MODEL), lambda b, i: (b, i, 0)),
        pl.BlockSpec((None, 1, D_MODEL), lsel),
        pl.BlockSpec((None, D_MODEL, PROJ_PAD), lsel, pipeline_mode=pl.Buffered(1)),
        pl.BlockSpec((tm, LANES), lambda b, i: (i, 0)),
        pl.BlockSpec((tm, LANES), lambda b, i: (i, 0)),
        pl.BlockSpec((None, 1, FOX_WIDTH), lsel),
        pl.BlockSpec((None, 1, FOX_WIDTH), lsel),
        pl.BlockSpec((None, 1, LANES), lsel),
        pl.BlockSpec((None, 3, CONV_DIM), lsel),
        pl.BlockSpec((FOX_WIDTH, FOX_WIDTH), lambda b, i: (0, 0)),
    ]
    args = [x, prm["norm_mix"], prm["w_in"], cos, sin, prm["gq"], prm["gk"], prm["fb"], prm["conv_w"], bd]
    tok = lambda width: pl.BlockSpec((None, tm, width), lambda b, i: (b, i, 0))
    if tmajor:
        in_specs.append(pl.BlockSpec((2 * nb, CONV_DIM), lambda b, i: (0, 0)))
        args.append(hist)
        kv_shape = jax.ShapeDtypeStruct((g, r, FOX_WIDTH), F32)
        kv_spec = tok(FOX_WIDTH)
        vb_shape = jax.ShapeDtypeStruct((g, r, FOX_WIDTH), BF16)
        vb_spec = tok(FOX_WIDTH)
        st_shape = jax.ShapeDtypeStruct((2 * nb, CONV_DIM), F32)
        st_spec = pl.BlockSpec((2 * nb, CONV_DIM), lambda b, i: (0, 0))
        aliases = {}
        scratch = []
    else:
        in_specs += [pl.BlockSpec(memory_space=pl.ANY), pl.BlockSpec(memory_space=pl.ANY)]
        args += [kbuf, vbuf]
        kv_shape = jax.ShapeDtypeStruct((depth, g, FOX_WIDTH, r), F32)
        kv_spec = pl.BlockSpec((None, None, FOX_WIDTH, tm), lambda b, i: (l, b, 0, i))
        vb_shape = jax.ShapeDtypeStruct((g, FOX_WIDTH, r), BF16)
        vb_spec = pl.BlockSpec((None, FOX_WIDTH, tm), lambda b, i: (b, 0, i))
        st_shape = jax.ShapeDtypeStruct((g, 2, CONV_DIM), F32)
        st_spec = pl.BlockSpec((None, 2, CONV_DIM), lambda b, i: (b, 0, 0))
        aliases = {10: 2, 11: 3}
        scratch = [pltpu.VMEM((SUBLANES, CONV_DIM), F32)]
    out_shape = [
        jax.ShapeDtypeStruct((g, r, 1024), BF16),
        jax.ShapeDtypeStruct((g, r, FOX_WIDTH), BF16),
        kv_shape, kv_shape,
        jax.ShapeDtypeStruct((g, r, FOX_WIDTH), BF16),
        vb_shape,
        jax.ShapeDtypeStruct((g, r, FOX_HEADS), F32),
        jax.ShapeDtypeStruct((g, FOX_HEADS, r), F32),
        jax.ShapeDtypeStruct((g, r, CONV_DIM), BF16),
        st_shape,
    ]
    out_specs = [tok(1024), tok(FOX_WIDTH), kv_spec, kv_spec, tok(FOX_WIDTH), vb_spec,
                 tok(FOX_HEADS), pl.BlockSpec((None, FOX_HEADS, tm), lambda b, i: (b, 0, i)),
                 tok(CONV_DIM), st_spec]
    return pl.pallas_call(
        functools.partial(_inproj_kernel, tmajor=tmajor, nb=nb),
        grid=grid, in_specs=in_specs, out_specs=out_specs, out_shape=out_shape,
        scratch_shapes=scratch, input_output_aliases=aliases,
        compiler_params=pltpu.CompilerParams(dimension_semantics=("arbitrary", "arbitrary"),
                                             vmem_limit_bytes=VMEM_LIMIT),
        name="inproj",
    )(*args)


def _ret_kernel(x_ref, s0_ref, gain_ref, dec_ref, qs_ref, ks_ref, gbm_ref, bm_ref, bdv_ref,
                o_ref, sout_ref, s_ref, *, blk):
    @pl.when(pl.program_id(1) == 0)
    def _():
        s_ref[...] = s0_ref[...]

    inv_d = 1.0 / 64.0
    bdv = bdv_ref[...]
    lane = lax.broadcasted_iota(jnp.int32, (1, RET_WIDTH), 1)
    head_k = (lane % 128) // 32
    head_v = lane // 64
    nsb = x_ref.shape[0] // blk
    intra = []
    for sb in range(nsb):
        rows = slice(sb * blk, (sb + 1) * blk)
        q = x_ref[rows, 0:256]
        qst = jnp.concatenate([jnp.where(head_k == hd, q, jnp.zeros_like(q)) for hd in range(RET_HEADS)], axis=0)
        a = (_dot_nt(qst, x_ref[rows, 256:512]) * dec_ref[...]).astype(BF16)
        oa = _dot(a, x_ref[rows, 512:768])
        acc = jnp.where(head_v == 0, oa[0:blk, :], 0.0)
        for hd in range(1, RET_HEADS):
            acc = acc + jnp.where(head_v == hd, oa[hd * blk:(hd + 1) * blk, :], 0.0)
        intra.append(acc)
    for sb in range(nsb):
        rows = slice(sb * blk, (sb + 1) * blk)
        q = x_ref[rows, 0:256]
        k = x_ref[rows, 256:512]
        v = x_ref[rows, 512:768]
        gate = x_ref[rows, 768:1024].astype(F32)
        state = s_ref[...]
        qw = (q.astype(F32) * qs_ref[...]).astype(BF16)
        o = intra[sb] + _dot(qw, state.astype(BF16))
        kw_t = (k.astype(F32) * ks_ref[...]).T.astype(BF16)
        s_ref[...] = gbm_ref[...] * state + bm_ref[...] * _dot(kw_t, v)
        hi, lo = _split2(o)
        mu2 = _dot(jnp.concatenate([hi, lo], axis=0), bdv)
        d = o - (mu2[0:blk] + mu2[blk:2 * blk]) * inv_d
        hi, lo = _split2(d * d)
        var2 = _dot(jnp.concatenate([hi, lo], axis=0), bdv)
        var = (var2[0:blk] + var2[blk:2 * blk]) * inv_d
        on = d * lax.rsqrt(var + NORM_EPS) * gain_ref[...]
        o_ref[rows, :] = (on * (gate / (1.0 + jnp.exp(-gate)))).astype(BF16)
    sout_ref[...] = s_ref[...]


def _ret_constants(blk, chunk_len):
    log_g = jnp.log1p(-jnp.exp2(-5.0 - jnp.arange(RET_HEADS, dtype=F32)))
    lane = np.arange(RET_WIDTH)
    head_k = (lane % 128) // 32
    head_v = lane // 64
    i = jnp.arange(blk, dtype=F32)
    diff = i[:, None] - i[None, :]
    dec = jnp.where(diff[None] >= 0.0, jnp.exp(jnp.maximum(diff, 0.0)[None] * log_g[:, None, None]), 0.0)
    dec = dec.reshape(RET_HEADS * blk, blk)
    lg_k = log_g[head_k]
    qs = jnp.exp((i + 1.0)[:, None] * lg_k[None, :])
    ks = jnp.exp((chunk_len - 1.0 - i)[:, None] * lg_k[None, :])
    gbm = jnp.broadcast_to(jnp.exp(chunk_len * lg_k)[:, None], (RET_WIDTH, RET_WIDTH))
    bm = jnp.asarray(head_k[:, None] == head_v[None, :], F32)
    bdv = jnp.asarray(head_v[:, None] == head_v[None, :], BF16)
    return dec, qs, ks, gbm, bm, bdv


def _retention(l, ret_in, s0, gain, consts, blk, rt):
    g, r, _ = ret_in.shape
    grid = (g, r // rt)
    const2 = lambda shape: pl.BlockSpec(shape, lambda b, i: (0,) * len(shape))
    dec, qs, ks, gbm, bm, bdv = consts
    in_specs = [
        pl.BlockSpec((None, rt, 1024), lambda b, i: (b, i, 0)),
        pl.BlockSpec((None, RET_WIDTH, RET_WIDTH), lambda b, i: (b, 0, 0)),
        pl.BlockSpec((None, 1, RET_WIDTH), lambda b, i: (l, 0, 0)),
        const2(dec.shape), const2(qs.shape), const2(ks.shape), const2(gbm.shape), const2(bm.shape),
        const2(bdv.shape),
    ]
    return pl.pallas_call(
        functools.partial(_ret_kernel, blk=blk),
        grid=grid, in_specs=in_specs,
        out_specs=[pl.BlockSpec((None, rt, RET_WIDTH), lambda b, i: (b, i, 0)),
                   pl.BlockSpec((None, RET_WIDTH, RET_WIDTH), lambda b, i: (b, 0, 0))],
        out_shape=[jax.ShapeDtypeStruct((g, r, RET_WIDTH), BF16),
                   jax.ShapeDtypeStruct((g, RET_WIDTH, RET_WIDTH), F32)],
        scratch_shapes=[pltpu.VMEM((RET_WIDTH, RET_WIDTH), F32)],
        compiler_params=pltpu.CompilerParams(dimension_semantics=("arbitrary", "arbitrary"),
                                             vmem_limit_bytes=VMEM_LIMIT),
        name="retention",
    )(ret_in, s0, gain, dec, qs, ks, gbm, bm, bdv)


def _state_to_bd(st):
    g = st.shape[0]
    t = jnp.einsum("bhpie,hg->bphige", st.reshape(g, RET_HEADS, 2, 32, 64), jnp.eye(RET_HEADS, dtype=st.dtype))
    return t.reshape(g, RET_WIDTH, RET_WIDTH)


def _bd_to_state(sbd):
    g = sbd.shape[0]
    t = jnp.einsum("bphihe->bhpie", sbd.reshape(g, 2, RET_HEADS, 32, RET_HEADS, 64))
    return t.reshape(g, RET_HEADS, 64, 64)


def _cumsum_kernel(x_ref, u_ref, o_ref):
    nc = x_ref.shape[1] // LANES
    xs = jnp.concatenate([x_ref[:, LANES * j:LANES * (j + 1)] for j in range(nc)], axis=0)
    x1 = xs.astype(BF16)
    r1 = xs - x1.astype(F32)
    x2 = r1.astype(BF16)
    x3 = (r1 - x2.astype(F32)).astype(BF16)
    tri = u_ref[...]
    loc = _dot(x1, tri) + _dot(x2, tri) + _dot(x3, tri)
    off = jnp.zeros((SUBLANES, 1), F32)
    for j in range(nc):
        lj = loc[SUBLANES * j:SUBLANES * (j + 1), :]
        o_ref[:, LANES * j:LANES * (j + 1)] = (lj + off) * (-LOG2E)
        off = off + lj[:, LANES - 1:LANES]


def _cumsum_lanes(x):
    n, h, w = x.shape
    tri = jnp.asarray(np.arange(LANES)[:, None] <= np.arange(LANES)[None, :], BF16)
    return pl.pallas_call(
        _cumsum_kernel,
        grid=(n,),
        in_specs=[pl.BlockSpec((None, h, w), lambda b: (b, 0, 0)),
                  pl.BlockSpec((LANES, LANES), lambda b: (0, 0))],
        out_specs=pl.BlockSpec((None, h, w), lambda b: (b, 0, 0)),
        out_shape=jax.ShapeDtypeStruct((n, h, w), F32),
        compiler_params=pltpu.CompilerParams(dimension_semantics=("arbitrary",)),
        name="forget_cumsum",
    )(x, tri)


def _fox_prompt_kernel(q_ref, k_ref, vt_ref, nc_ref, o_ref, va_ref, ncol_ref, qh_ref, sd_ref, sa_ref, sb_ref,
                       m_ref, acc_ref, *, tile):
    nq = q_ref.shape[0] // tile
    lane = lax.broadcasted_iota(jnp.int32, (1, LANES), 1)
    half = tile // 2
    krow = lax.broadcasted_iota(jnp.int32, (half, tile), 0)
    qcol = lax.broadcasted_iota(jnp.int32, (half, tile), 1)
    brow = lax.broadcasted_iota(jnp.int32, (half, half), 0)
    bcol = lax.broadcasted_iota(jnp.int32, (half, half), 1)
    for j in range(nq):
        for hd in range(2):
            va_ref[hd, j, 0:FOX_HEAD_DIM, :] = vt_ref[FOX_HEAD_DIM * hd:FOX_HEAD_DIM * (hd + 1),
                                                       tile * j:tile * (j + 1)]
            va_ref[hd, j, FOX_HEAD_DIM:FOX_VROWS, :] = jnp.ones((FOX_VROWS - FOX_HEAD_DIM, tile), BF16)
        rows8 = jnp.concatenate([nc_ref[j], jnp.zeros((SUBLANES - 2, tile), F32)], axis=0)
        ncol_ref[j] = rows8.T

    def set_q(slot, qi):
        q = q_ref[pl.ds(pl.multiple_of(qi * tile, tile), tile), :]
        for hd in range(2):
            qh_ref[slot, hd] = jnp.where(lane // FOX_HEAD_DIM == hd, q, jnp.zeros_like(q))

    def park(dst_ref, qslot, kj, heads=(0, 1)):
        kt = k_ref[pl.ds(pl.multiple_of(kj * tile, tile), tile), :]
        ncol = ncol_ref[kj]
        for hd in heads:
            dst_ref[hd] = _dot_nt(kt, qh_ref[qslot, hd]) + ncol[:, hd:hd + 1]

    def consume(src_ref, kj, heads=(0, 1)):
        for hd in heads:
            s = src_ref[hd]
            m_prev = m_ref[hd]
            m_new = jnp.maximum(m_prev, jnp.max(s, axis=0, keepdims=True))
            alpha = jnp.exp2(m_prev - m_new)
            p = jnp.exp2(s - m_new).astype(BF16)
            acc_ref[hd] = alpha * acc_ref[hd] + _dot(va_ref[hd, kj], p)
            m_ref[hd] = m_new

    def park_diag(qslot, kj, heads=(0, 1)):
        k0 = pl.multiple_of(kj * tile, tile)
        for hd in heads:
            top = (_dot_nt(k_ref[pl.ds(k0, half), :], qh_ref[qslot, hd])
                   + ncol_ref[kj, 0:half, :][:, hd:hd + 1])
            sd_ref[hd, 0:half, :] = jnp.where(krow <= qcol, top, -jnp.inf)
            bot = (_dot_nt(k_ref[pl.ds(pl.multiple_of(k0 + half, half), half), :], qh_ref[qslot, hd, half:tile, :])
                   + ncol_ref[kj, half:tile, :][:, hd:hd + 1])
            sd_ref[hd, half:tile, half:tile] = jnp.where(brow <= bcol, bot, -jnp.inf)

    def consume_diag(kj, heads=(0, 1)):
        for hd in heads:
            top_l = sd_ref[hd, 0:half, 0:half]
            top_r = sd_ref[hd, 0:half, half:tile]
            bot = sd_ref[hd, half:tile, half:tile]
            m_l = jnp.max(top_l, axis=0, keepdims=True)
            m_r = jnp.maximum(jnp.max(top_r, axis=0, keepdims=True), jnp.max(bot, axis=0, keepdims=True))
            va_a = va_ref[hd, kj, :, 0:half]
            acc_ref[hd, :, 0:half] = _dot(va_a, jnp.exp2(top_l - m_l).astype(BF16))
            acc_ref[hd, :, half:tile] = (_dot(va_a, jnp.exp2(top_r - m_r).astype(BF16))
                                         + _dot(va_ref[hd, kj, :, half:tile], jnp.exp2(bot - m_r).astype(BF16)))
            m_ref[hd, :, 0:half] = m_l
            m_ref[hd, :, half:tile] = m_r

    def finish(qi):
        ot = jnp.concatenate(
            [acc_ref[hd, 0:FOX_HEAD_DIM, :] / acc_ref[hd, FOX_HEAD_DIM:FOX_HEAD_DIM + 1, :] for hd in range(2)],
            axis=0)
        o_ref[pl.ds(pl.multiple_of(qi * tile, tile), tile), :] = ot.T.astype(BF16)

    def per_head(first_fn, second_fn):
        for hd in range(2):
            first_fn((hd,))
            second_fn((hd,))

    def next_diag(qi):
        nxt = jnp.minimum(qi + 1, nq - 1)
        set_q(1 - qi % 2, nxt)
        return 1 - qi % 2, nxt

    set_q(0, 0)
    park_diag(0, 0)

    def q_body(qi, carry):
        qs = qi % 2

        def head(qi):
            per_head(lambda h: park(sa_ref, qs, 0, h), lambda h: consume_diag(qi, h))

            def kv_pair(jp, c2):
                j = 2 * jp + 1
                per_head(lambda h: park(sb_ref, qs, j, h), lambda h: consume(sa_ref, j - 1, h))
                per_head(lambda h: park(sa_ref, qs, j + 1, h), lambda h: consume(sb_ref, j, h))
                return c2

            lax.fori_loop(0, (qi - 1) // 2, kv_pair, 0)

        @pl.when(qi == 0)
        def _():
            nslot, nxt = next_diag(qi)
            per_head(lambda h: consume_diag(0, h), lambda h: park_diag(nslot, nxt, h))
            finish(qi)

        @pl.when(jnp.logical_and(qi > 0, qi % 2 == 0))
        def _():
            head(qi)
            per_head(lambda h: park(sb_ref, qs, qi - 1, h), lambda h: consume(sa_ref, qi - 2, h))
            nslot, nxt = next_diag(qi)
            per_head(lambda h: park_diag(nslot, nxt, h), lambda h: consume(sb_ref, qi - 1, h))
            finish(qi)

        @pl.when(qi % 2 == 1)
        def _():
            head(qi)
            nslot, nxt = next_diag(qi)
            per_head(lambda h: park_diag(nslot, nxt, h), lambda h: consume(sa_ref, qi - 1, h))
            finish(qi)

        return carry

    lax.fori_loop(0, nq, q_body, 0)


def _fox_prompt(q, k, vt, nc_rows, tile):
    b, s, _ = q.shape
    nt = s // tile
    qk_spec = pl.BlockSpec((None, s, LANES), lambda bi, hp: (bi, 0, hp))
    return pl.pallas_call(
        functools.partial(_fox_prompt_kernel, tile=tile),
        grid=(b, FOX_HEADS // 2),
        in_specs=[qk_spec, qk_spec, pl.BlockSpec((None, LANES, s), lambda bi, hp: (bi, hp, 0)),
                  pl.BlockSpec((None, None, nt, 2, tile), lambda bi, hp: (bi, hp, 0, 0, 0))],
        out_specs=qk_spec,
        out_shape=jax.ShapeDtypeStruct((b, s, FOX_WIDTH), BF16),
        scratch_shapes=[pltpu.VMEM((2, nt, FOX_VROWS, tile), BF16), pltpu.VMEM((nt, tile, SUBLANES), F32),
                        pltpu.VMEM((2, 2, tile, LANES), BF16),
                        pltpu.VMEM((2, tile, tile), F32), pltpu.VMEM((2, tile, tile), F32),
                        pltpu.VMEM((2, tile, tile), F32),
                        pltpu.VMEM((2, 1, tile), F32), pltpu.VMEM((2, FOX_VROWS, tile), F32)],
        compiler_params=pltpu.CompilerParams(dimension_semantics=("arbitrary", "arbitrary"),
                                             vmem_limit_bytes=VMEM_LIMIT),
        name="fox_prompt",
    )(q, k, vt, nc_rows)


def _fox_sample_kernel(q_ref, kt_ref, vt_ref, knt_ref, vnt_ref, ncc_ref, ncn_ref, o_ref,
                       m_ref, l_ref, acc_ref):
    chunk = pl.program_id(1)
    n = q_ref.shape[0]
    lane = lax.broadcasted_iota(jnp.int32, (1, FOX_WIDTH), 1)

    @pl.when(chunk == 0)
    def _():
        m_ref[...] = jnp.full(m_ref.shape, -jnp.inf, F32)
        l_ref[...] = jnp.zeros(l_ref.shape, F32)
        acc_ref[...] = jnp.zeros(acc_ref.shape, F32)

    q = q_ref[...]
    qbd = jnp.concatenate(
        [jnp.where(lane // FOX_HEAD_DIM == hd, q, jnp.zeros_like(q)) for hd in range(FOX_HEADS)], axis=0)

    def head_rows(nc):
        return jnp.concatenate(
            [jnp.broadcast_to(nc[hd:hd + 1, :], (n, nc.shape[1])) for hd in range(FOX_HEADS)], axis=0)

    def update(s, vt):
        m_prev = m_ref[...]
        m_new = jnp.maximum(m_prev, jnp.max(s, axis=-1, keepdims=True))
        alpha = jnp.exp2(m_prev - m_new)
        p = jnp.exp2(s - m_new[:, 0:1])
        l_ref[...] = alpha * l_ref[...] + jnp.sum(p, axis=-1, keepdims=True)
        acc_ref[...] = alpha[:, 0:1] * acc_ref[...] + _dot_nt(p.astype(BF16), vt)
        m_ref[...] = m_new

    update(_dot(qbd, kt_ref[...].astype(BF16)) + head_rows(ncc_ref[...]), vt_ref[...].astype(BF16))

    @pl.when(chunk == pl.num_programs(1) - 1)
    def _():
        r = lax.broadcasted_iota(jnp.int32, (FOX_HEADS * n, n), 0)
        c = lax.broadcasted_iota(jnp.int32, (FOX_HEADS * n, n), 1)
        sn = _dot(qbd, knt_ref[...].astype(BF16)) + head_rows(ncn_ref[:, 0:n])
        update(jnp.where(c <= r % n, sn, -jnp.inf), vnt_ref[...].astype(BF16))
        o_all = acc_ref[...] / l_ref[:, 0:1]
        o = jnp.zeros((n, FOX_WIDTH), F32)
        for hd in range(FOX_HEADS):
            o = o + jnp.where(lane // FOX_HEAD_DIM == hd, o_all[n * hd:n * (hd + 1), :], 0.0)
        o_ref[...] = o.astype(BF16)


def _fox_sample(l, q, knt, vnt, cache_kt, cache_vt, nc):
    nb, n, _ = q.shape
    past = cache_kt.shape[3]
    ck = min(FOX_SAMPLE_CHUNK, past)
    q_spec = pl.BlockSpec((None, n, FOX_WIDTH), lambda b, c: (b, 0, 0))
    new_spec = pl.BlockSpec((None, FOX_WIDTH, n), lambda b, c: (b, 0, 0))
    cache_spec = pl.BlockSpec((None, None, FOX_WIDTH, ck), lambda b, c: (l, b, 0, c))
    rows = FOX_HEADS * n
    return pl.pallas_call(
        _fox_sample_kernel,
        grid=(nb, past // ck),
        in_specs=[q_spec, cache_spec, cache_spec, new_spec, new_spec,
                  pl.BlockSpec((None, FOX_HEADS, ck), lambda b, c: (b, 0, c)),
                  pl.BlockSpec((None, FOX_HEADS, LANES), lambda b, c: (b, 0, past // LANES))],
        out_specs=q_spec,
        out_shape=jax.ShapeDtypeStruct((nb, n, FOX_WIDTH), BF16),
        scratch_shapes=[pltpu.VMEM((rows, LANES), F32), pltpu.VMEM((rows, LANES), F32),
                        pltpu.VMEM((rows, FOX_WIDTH), F32)],
        compiler_params=pltpu.CompilerParams(dimension_semantics=("arbitrary", "arbitrary"),
                                             vmem_limit_bytes=VMEM_LIMIT),
        name="fox_sample",
    )(q, cache_kt, cache_vt, knt, vnt, nc, nc)


def _mlp_kernel(*refs, tmajor, nb):
    if tmajor:
        (x_ref, oret_ref, ofox_ref, oconv_ref, wout_ref, nw_ref, wg_ref, wu_ref, wd_ref, fcw_ref, hist_ref,
         y_ref, st_ref) = refs
    else:
        (x_ref, oret_ref, ofox_ref, oconv_ref, wout_ref, nw_ref, wg_ref, wu_ref, wd_ref, fcw_ref,
         y_ref, st_ref, carry_ref) = refs
    tm = x_ref.shape[0]
    rb = tm if tmajor else min(MLP_ROW_BLOCK, tm)
    for r0 in range(0, tm, rb):
        rows = slice(r0, r0 + rb)
        x1 = (x_ref[rows, :] + _dot(oret_ref[rows, :], wout_ref[0:256, :])
              + _dot(ofox_ref[rows, :], wout_ref[256:768, :]) + _dot(oconv_ref[rows, :], wout_ref[768:1024, :]))
        h2 = _rms(x1, nw_ref[...]).astype(BF16)
        down = None
        for c0, c1 in FF_CHUNKS:
            cols = slice(c0, c1)
            gate_pre = _dot(h2, wg_ref[:, cols])
            if tmajor:
                gate_c, new_hist = _dwconv_tmajor(gate_pre, fcw_ref.at[:, cols], hist_ref[:, cols], nb)
                st_ref[:, cols] = new_hist
            else:
                first = (pl.program_id(1) == 0) if r0 == 0 else None
                gate_c = _dwconv_carry(gate_pre, fcw_ref.at[:, cols], carry_ref.at[:, cols], first)
                if r0 + rb == tm:
                    st_ref[:, cols] = carry_ref[6:8, cols]
            up = _dot(h2, wu_ref[:, cols])
            act = (gate_c / (1.0 + jnp.exp(-gate_c)) * up).astype(BF16)
            part = _dot(act, wd_ref[cols, :])
            down = part if down is None else down + part
        y_ref[rows, :] = x1 + down


def _mlp(l, x, oret, ofox, oconv, prm, *, hist=None, nb=0):
    tmajor = hist is not None
    g, r, _ = x.shape
    tm = min(TOKEN_TILE, r)
    grid = (g, r // tm)
    lsel = lambda b, i: (l, 0, 0)
    tok = lambda width: pl.BlockSpec((None, tm, width), lambda b, i: (b, i, 0))
    wspec = lambda rows, cols: pl.BlockSpec((None, rows, cols), lsel, pipeline_mode=pl.Buffered(1))
    in_specs = [tok(D_MODEL), tok(RET_WIDTH), tok(FOX_WIDTH), tok(CONV_DIM),
                wspec(D_MODEL, D_MODEL), pl.BlockSpec((None, 1, D_MODEL), lsel),
                wspec(D_MODEL, D_FF), wspec(D_MODEL, D_FF), wspec(D_FF, D_MODEL),
                pl.BlockSpec((None, 3, D_FF), lsel)]
    args = [x, oret, ofox, oconv, prm["w_out"], prm["norm_ffn"], prm["w_gate"], prm["w_up"], prm["w_down"],
            prm["ffn_conv_w"]]
    if tmajor:
        in_specs.append(pl.BlockSpec((2 * nb, D_FF), lambda b, i: (0, 0)))
        args.append(hist)
        st_shape = jax.ShapeDtypeStruct((2 * nb, D_FF), F32)
        st_spec = pl.BlockSpec((2 * nb, D_FF), lambda b, i: (0, 0))
        scratch = []
    else:
        st_shape = jax.ShapeDtypeStruct((g, 2, D_FF), F32)
        st_spec = pl.BlockSpec((None, 2, D_FF), lambda b, i: (b, 0, 0))
        scratch = [pltpu.VMEM((SUBLANES, D_FF), F32)]
    return pl.pallas_call(
        functools.partial(_mlp_kernel, tmajor=tmajor, nb=nb),
        grid=grid, in_specs=in_specs,
        out_specs=[tok(D_MODEL), st_spec],
        out_shape=[jax.ShapeDtypeStruct((g, r, D_MODEL), F32), st_shape],
        scratch_shapes=scratch,
        compiler_params=pltpu.CompilerParams(dimension_semantics=("arbitrary", "arbitrary"),
                                             vmem_limit_bytes=VMEM_LIMIT),
        name="outproj_mlp",
    )(*args)


def _permute_w_in(w_in):
    w = w_in.astype(BF16)
    depth, rows, _ = w.shape

    def halves_first(blk):
        return blk.reshape(depth, rows, RET_HEADS, 2, 32).transpose(0, 1, 3, 2, 4).reshape(depth, rows, RET_WIDTH)

    parts = [halves_first(w[..., 0:256]), halves_first(w[..., 256:512]), w[..., 512:2560], w[..., 2568:3336],
             w[..., 2560:2568], jnp.zeros((depth, rows, PROJ_PAD - 3336), BF16)]
    return jnp.concatenate(parts, axis=2)


def _rope_tables(pos):
    half = 32
    inv_freq = ROPE_BASE ** (-jnp.arange(half, dtype=F32) / half)
    ang = pos.astype(F32)[:, None] * inv_freq[None, :]
    return jnp.tile(jnp.cos(ang), (1, RET_HEADS)), jnp.tile(jnp.sin(ang), (1, RET_HEADS))


def _to_bmajor(a, n, nb):
    return a.reshape(n, nb, a.shape[-1]).transpose(1, 0, 2)


def _to_tmajor(a):
    nb, n, c = a.shape
    return a.transpose(1, 0, 2).reshape(1, n * nb, c)


def kernel(x_prompt, x_sample, cache_fox_k, cache_fox_v, cache_fox_logf, state_ret, state_conv, state_ffn_conv,
           norm_mix, w_in, ret_gn_gain, fox_q_gain, fox_k_gain, fox_f_bias, conv_w, w_out, norm_ffn, w_gate,
           w_up, ffn_conv_w, w_down):
    depth = w_in.shape[0]
    b, s, _ = x_prompt.shape
    nb, n, _ = x_sample.shape
    past = cache_fox_k.shape[2]

    w_perm = _permute_w_in(w_in)
    prm = {
        "w_in": w_perm,
        "norm_mix": norm_mix.reshape(depth, 1, D_MODEL),
        "gq": fox_q_gain.reshape(depth, 1, FOX_WIDTH),
        "gk": fox_k_gain.reshape(depth, 1, FOX_WIDTH),
        "fb": jnp.pad(fox_f_bias, ((0, 0), (0, LANES - FOX_HEADS))).reshape(depth, 1, LANES),
        "conv_w": conv_w,
        "w_out": w_out.astype(BF16),
        "norm_ffn": norm_ffn.reshape(depth, 1, D_MODEL),
        "w_gate": w_gate.astype(BF16),
        "w_up": w_up.astype(BF16),
        "w_down": w_down.astype(BF16),
        "ffn_conv_w": ffn_conv_w,
    }
    gain = ret_gn_gain.reshape(depth, 1, RET_WIDTH)
    lane = np.arange(FOX_WIDTH)
    bd = jnp.asarray(lane[:, None] // FOX_HEAD_DIM == lane[None, :] // FOX_HEAD_DIM, BF16)

    cos_p, sin_p = _rope_tables(jnp.arange(s, dtype=jnp.int32))
    pos_s = past + jnp.arange(n, dtype=jnp.int32)
    cos_s, sin_s = _rope_tables(jnp.repeat(pos_s, nb))

    ret_blk = min(RET_BLOCK, s)
    ret_rt = min(TOKEN_TILE, s)
    consts_p = _ret_constants(ret_blk, float(ret_blk))
    samp_rows = LANES
    consts_s = _ret_constants(samp_rows, float(n))
    fox_tile = min(FOX_TILE, s)

    cache_k = jnp.transpose(cache_fox_k, (0, 1, 3, 4, 2)).reshape(depth, nb, FOX_WIDTH, past)
    cache_v = jnp.transpose(cache_fox_v, (0, 1, 3, 4, 2)).reshape(depth, nb, FOX_WIDTH, past)
    cache_lf_t = jnp.transpose(cache_fox_logf, (0, 1, 3, 2))
    c_width = -(-(past + n) // LANES) * LANES

    kbuf = jnp.zeros((depth, b, FOX_WIDTH, s), F32)
    vbuf = jnp.zeros((depth, b, FOX_WIDTH, s), F32)
    zero_state = jnp.zeros((b, RET_WIDTH, RET_WIDTH), F32)
    hist_c_all = jnp.transpose(state_conv, (0, 2, 1, 3)).reshape(depth, 2 * nb, CONV_DIM)
    hist_f_all = jnp.transpose(state_ffn_conv, (0, 2, 1, 3)).reshape(depth, 2 * nb, D_FF)
    state_bd_all = _state_to_bd(state_ret.reshape(depth * nb, RET_HEADS, 64, 64)).reshape(
        depth, nb, RET_WIDTH, RET_WIDTH)

    xp = x_prompt
    xs = jnp.transpose(x_sample, (1, 0, 2)).reshape(1, n * nb, D_MODEL)
    p_lf, p_ret, p_conv, p_ffn = [], [], [], []
    s_k, s_v, s_lf, s_ret, s_conv, s_ffn = [], [], [], [], [], []
    for l in range(depth):
        ret_in, fq, kbuf, vbuf, fkb, fvb, lf, lf_t, oconv, conv_st = _inproj(
            l, xp, prm, cos_p, sin_p, bd, kbuf=kbuf, vbuf=vbuf)
        oret, sbd = _retention(l, ret_in, zero_state, gain, consts_p, ret_blk, ret_rt)
        c_rows = _cumsum_lanes(lf_t)
        nc_rows = c_rows.reshape(b, FOX_HEADS // 2, 2, s // fox_tile, fox_tile).transpose(0, 1, 3, 2, 4)
        ofox = _fox_prompt(fq, fkb, fvb, nc_rows, fox_tile)
        xp, ffn_st = _mlp(l, xp, oret, ofox, oconv, prm)
        p_lf.append(lf)
        p_ret.append(sbd)
        p_conv.append(conv_st)
        p_ffn.append(ffn_st)

        hist_c = hist_c_all[l]
        ret_in, fq, fk32, fv32, fkb, fvb, lf, _, oconv, conv_st = _inproj(
            l, xs, prm, cos_s, sin_s, bd, hist=hist_c, nb=nb)
        ret_b = jnp.pad(_to_bmajor(ret_in, n, nb), ((0, 0), (0, samp_rows - n), (0, 0)))
        oret, sbd = _retention(l, ret_b, state_bd_all[l], gain, consts_s, samp_rows, samp_rows)
        lf_b = _to_bmajor(lf, n, nb)
        lf_all = jnp.concatenate(
            [cache_lf_t[l], jnp.transpose(lf_b, (0, 2, 1)),
             jnp.zeros((nb, FOX_HEADS, c_width - past - n), F32)], axis=2)
        nc = _cumsum_lanes(lf_all)
        k_new = _to_bmajor(fk32, n, nb)
        v_new = _to_bmajor(fv32, n, nb)
        s_k.append(k_new.reshape(nb, n, FOX_HEADS, FOX_HEAD_DIM))
        s_v.append(v_new.reshape(nb, n, FOX_HEADS, FOX_HEAD_DIM))
        ofox = _fox_sample(l, _to_bmajor(fq, n, nb), jnp.transpose(k_new, (0, 2, 1)),
                           jnp.transpose(v_new, (0, 2, 1)), cache_k, cache_v, nc)
        hist_f = hist_f_all[l]
        xs, ffn_st = _mlp(l, xs, _to_tmajor(oret[:, :n]), _to_tmajor(ofox), oconv, prm, hist=hist_f, nb=nb)
        s_lf.append(lf_b)
        s_ret.append(sbd)
        s_conv.append(conv_st)
        s_ffn.append(ffn_st)

    y_sample = xs.reshape(n, nb, D_MODEL).transpose(1, 0, 2)
    stk = lambda ts: jnp.stack(ts, axis=0)
    ret_states = lambda ts, g: _bd_to_state(stk(ts).reshape(depth * g, RET_WIDTH, RET_WIDTH)).reshape(
        depth, g, RET_HEADS, 64, 64)
    conv_states = lambda ts, c: stk(ts).reshape(depth, 2, nb, c).transpose(0, 2, 1, 3)
    from_fm = lambda a: jnp.transpose(a.reshape(depth, b, FOX_HEADS, FOX_HEAD_DIM, s), (0, 1, 4, 2, 3))
    return (xp, y_sample,
            from_fm(kbuf), from_fm(vbuf),
            stk(p_lf), ret_states(p_ret, b), stk(p_conv), stk(p_ffn),
            stk(s_k), stk(s_v), stk(s_lf), ret_states(s_ret, nb), conv_states(s_conv, CONV_DIM),
            conv_states(s_ffn, D_FF))
```

```python
import functools

import numpy as np
import jax
import jax.numpy as jnp
from jax import lax
from jax.experimental import pallas as pl
from jax.experimental.pallas import tpu as pltpu

F32 = jnp.float32
BF16 = jnp.bfloat16

D_MODEL = 1024
RET_HEADS = 4
RET_WIDTH = 256
FOX_HEADS = 8
FOX_HEAD_DIM = 64
FOX_WIDTH = 512
CONV_DIM = 256
D_FF = 2816
RET_HEAD_DIM = RET_WIDTH // RET_HEADS
RET_K_SCALE = RET_HEAD_DIM ** -0.5
PROJ_DIM = 4 * RET_WIDTH + 3 * FOX_WIDTH + FOX_HEADS + 3 * CONV_DIM
COLS_RET = (0, 4 * RET_WIDTH)
COLS_FOX = (COLS_RET[1], COLS_RET[1] + 3 * FOX_WIDTH)
COLS_CONV = (COLS_FOX[1], COLS_FOX[1] + 3 * CONV_DIM)
COLS_FORGET = (COLS_CONV[1], COLS_CONV[1] + 128)
PROJ_PAD = COLS_FORGET[1]
ROPE_BASE = 10000.0
NORM_EPS = 1e-6
LANES = 128
SUBLANES = 8
VMEM_LIMIT = 56 * 1024 * 1024

TOKEN_TILE = 512
RET_BLOCK = 256
MLP_ROW_BLOCK = 256
FOX_TILE = 512
FOX_SAMPLE_CHUNK = 2048
FOX_VROWS = 80
LOG2E = 1.4426950408889634
FOX_Q_SCALE = FOX_HEAD_DIM ** -0.5 * LOG2E
MXU_DEPTH = 256
FF_CHUNKS = ((0, D_FF),)


def _dot(a, b):
    return jnp.dot(a, b, preferred_element_type=F32)


def _dot_nt(a, b):
    return lax.dot_general(a, b, (((1,), (1,)), ((), ())), preferred_element_type=F32)


def _split2(x):
    hi = x.astype(BF16)
    lo = (x - hi.astype(F32)).astype(BF16)
    return hi, lo


def _group_sum(x, ones_bd):
    hi, lo = _split2(x)
    return _dot(hi, ones_bd) + _dot(lo, ones_bd)


def _rms(x, g):
    ms = jnp.mean(x * x, axis=-1, keepdims=True)
    return x * lax.rsqrt(ms + NORM_EPS) * g


def _dwconv_carry(u, w_ref, carry_ref, first):
    tm = u.shape[0]

    if first is not None:
        @pl.when(first)
        def _():
            carry_ref[...] = jnp.zeros(carry_ref.shape, F32)

    row = lax.broadcasted_iota(jnp.int32, u.shape, 0)
    c6 = carry_ref[6:7, :]
    c7 = carry_ref[7:8, :]
    um1 = jnp.where(row == 0, c7, pltpu.roll(u, 1, axis=0))
    um2 = jnp.where(row == 0, c6, jnp.where(row == 1, c7, pltpu.roll(u, 2, axis=0)))
    y = w_ref[0:1, :] * um2 + w_ref[1:2, :] * um1 + w_ref[2:3, :] * u
    carry_ref[...] = u[tm - SUBLANES:tm, :]
    return y


def _dwconv_tmajor(u, w_ref, hist, nb):
    tm = u.shape[0]
    full = jnp.concatenate([hist, u], axis=0)
    y = (w_ref[0:1, :] * full[0:tm] + w_ref[1:2, :] * full[nb:nb + tm]
         + w_ref[2:3, :] * full[2 * nb:2 * nb + tm])
    return y, full[tm:tm + 2 * nb]


def _inproj_kernel(*refs, tmajor, nb):
    if tmajor:
        (x_ref, nw_ref, w_ref, cos_ref, sin_ref, gq_ref, gk_ref, fb_ref, cw_ref, bd_ref, hist_ref,
         ret_ref, fq_ref, fk32_ref, fv32_ref, fkb_ref, fvb_ref, lf_ref, lft_ref, oc_ref, st_ref) = refs
    else:
        (x_ref, nw_ref, w_ref, cos_ref, sin_ref, gq_ref, gk_ref, fb_ref, cw_ref, bd_ref,
         _, _, ret_ref, fq_ref, fk32_ref, fv32_ref, fkb_ref, fvb_ref, lf_ref, lft_ref, oc_ref, st_ref,
         carry_ref) = refs
    h = _rms(x_ref[...], nw_ref[...]).astype(BF16)

    a = _dot(h, w_ref[:, COLS_RET[0]:COLS_RET[1]])
    f = _dot(h, w_ref[:, COLS_FOX[0]:COLS_FOX[1]])
    c = _dot(h, w_ref[:, COLS_CONV[0]:COLS_CONV[1]])
    z = _dot(h, w_ref[:, COLS_FORGET[0]:COLS_FORGET[1]]) + fb_ref[...]

    cos = cos_ref[...]
    sin = sin_ref[...]
    q1, q2, k1, k2 = a[:, 0:128], a[:, 128:256], a[:, 256:384], a[:, 384:512]
    ret_ref[:, 0:128] = (q1 * cos - q2 * sin).astype(BF16)
    ret_ref[:, 128:256] = (q1 * sin + q2 * cos).astype(BF16)
    ret_ref[:, 256:384] = ((k1 * cos - k2 * sin) * RET_K_SCALE).astype(BF16)
    ret_ref[:, 384:512] = ((k1 * sin + k2 * cos) * RET_K_SCALE).astype(BF16)
    ret_ref[:, 512:1024] = a[:, 512:1024].astype(BF16)

    fq, fk, fv = f[:, 0:512], f[:, 512:1024], f[:, 1024:1536]
    bd = bd_ref[...]
    inv_d = 1.0 / FOX_HEAD_DIM
    ssq = _dot((fq * fq).astype(BF16), bd)
    fq_ref[...] = (fq * lax.rsqrt(ssq * inv_d + NORM_EPS) * gq_ref[...] * FOX_Q_SCALE).astype(BF16)
    ssk = _group_sum(fk * fk, bd)
    fkn = fk * lax.rsqrt(ssk * inv_d + NORM_EPS) * gk_ref[...]
    fkb_ref[...] = fkn.astype(BF16)
    if tmajor:
        fk32_ref[...] = fkn
        fv32_ref[...] = fv
        fvb_ref[...] = fv.astype(BF16)
    else:
        fk32_ref[...] = fkn.T
        fvt = fv.T
        fv32_ref[...] = fvt
        fvb_ref[...] = fvt.astype(BF16)

    cb, cc, ch = c[:, 0:256], c[:, 256:512], c[:, 512:768]
    u = cc * ch
    if tmajor:
        y, new_hist = _dwconv_tmajor(u, cw_ref, hist_ref[...], nb)
        st_ref[...] = new_hist
    else:
        y = _dwconv_carry(u, cw_ref, carry_ref, pl.program_id(1) == 0)
        st_ref[...] = carry_ref[6:8, :]
    oc_ref[...] = (cb * y).astype(BF16)

    ls = jnp.minimum(z, 0.0) - jnp.log1p(jnp.exp(-jnp.abs(z)))
    lf_ref[...] = ls[:, 0:FOX_HEADS]
    lft_ref[...] = ls.T[0:FOX_HEADS, :]


def _inproj(l, x, prm, cos, sin, bd, *, kbuf=None, vbuf=None, hist=None, nb=0):
    tmajor = hist is not None
    g, r, _ = x.shape
    tm = min(TOKEN_TILE, r)
    grid = (g, r // tm)
    depth = prm["w_in"].shape[0]
    lsel = lambda b, i: (l, 0, 0)
    in_specs = [
        pl.BlockSpec((None, tm, D_MODEL), lambda b, i: (b, i, 0)),
        pl.BlockSpec((None, 1, D_MODEL), lsel),
        pl.BlockSpec((None, D_MODEL, PROJ_PAD), lsel, pipeline_mode=pl.Buffered(1)),
        pl.BlockSpec((tm, LANES), lambda b, i: (i, 0)),
        pl.BlockSpec((tm, LANES), lambda b, i: (i, 0)),
        pl.BlockSpec((None, 1, FOX_WIDTH), lsel),
        pl.BlockSpec((None, 1, FOX_WIDTH), lsel),
        pl.BlockSpec((None, 1, LANES), lsel),
        pl.BlockSpec((None, 3, CONV_DIM), lsel),
        pl.BlockSpec((FOX_WIDTH, FOX_WIDTH), lambda b, i: (0, 0)),
    ]
    args = [x, prm["norm_mix"], prm["w_in"], cos, sin, prm["gq"], prm["gk"], prm["fb"], prm["conv_w"], bd]
    tok = lambda width: pl.BlockSpec((None, tm, width), lambda b, i: (b, i, 0))
    if tmajor:
        in_specs.append(pl.BlockSpec((2 * nb, CONV_DIM), lambda b, i: (0, 0)))
        args.append(hist)
        kv_shape = jax.ShapeDtypeStruct((g, r, FOX_WIDTH), F32)
        kv_spec = tok(FOX_WIDTH)
        vb_shape = jax.ShapeDtypeStruct((g, r, FOX_WIDTH), BF16)
        vb_spec = tok(FOX_WIDTH)
        st_shape = jax.ShapeDtypeStruct((2 * nb, CONV_DIM), F32)
        st_spec = pl.BlockSpec((2 * nb, CONV_DIM), lambda b, i: (0, 0))
        aliases = {}
        scratch = []
    else:
        in_specs += [pl.BlockSpec(memory_space=pl.ANY), pl.BlockSpec(memory_space=pl.ANY)]
        args += [kbuf, vbuf]
        kv_shape = jax.ShapeDtypeStruct((depth, g, FOX_WIDTH, r), F32)
        kv_spec = pl.BlockSpec((None, None, FOX_WIDTH, tm), lambda b, i: (l, b, 0, i))
        vb_shape = jax.ShapeDtypeStruct((g, FOX_WIDTH, r), BF16)
        vb_spec = pl.BlockSpec((None, FOX_WIDTH, tm), lambda b, i: (b, 0, i))
        st_shape = jax.ShapeDtypeStruct((g, 2, CONV_DIM), F32)
        st_spec = pl.BlockSpec((None, 2, CONV_DIM), lambda b, i: (b, 0, 0))
        aliases = {10: 2, 11: 3}
        scratch = [pltpu.VMEM((SUBLANES, CONV_DIM), F32)]
    out_shape = [
        jax.ShapeDtypeStruct((g, r, 1024), BF16),
        jax.ShapeDtypeStruct((g, r, FOX_WIDTH), BF16),
        kv_shape, kv_shape,
        jax.ShapeDtypeStruct((g, r, FOX_WIDTH), BF16),
        vb_shape,
        jax.ShapeDtypeStruct((g, r, FOX_HEADS), F32),
        jax.ShapeDtypeStruct((g, FOX_HEADS, r), F32),
        jax.ShapeDtypeStruct((g, r, CONV_DIM), BF16),
        st_shape,
    ]
    out_specs = [tok(1024), tok(FOX_WIDTH), kv_spec, kv_spec, tok(FOX_WIDTH), vb_spec,
                 tok(FOX_HEADS), pl.BlockSpec((None, FOX_HEADS, tm), lambda b, i: (b, 0, i)),
                 tok(CONV_DIM), st_spec]
    return pl.pallas_call(
        functools.partial(_inproj_kernel, tmajor=tmajor, nb=nb),
        grid=grid, in_specs=in_specs, out_specs=out_specs, out_shape=out_shape,
        scratch_shapes=scratch, input_output_aliases=aliases,
        compiler_params=pltpu.CompilerParams(dimension_semantics=("arbitrary", "arbitrary"),
                                             vmem_limit_bytes=VMEM_LIMIT),
        name="inproj",
    )(*args)


def _ret_kernel(x_ref, s0_ref, gain_ref, dec_ref, qs_ref, ks_ref, gbm_ref, bm_ref, bdv_ref,
                o_ref, sout_ref, s_ref, *, blk):
    @pl.when(pl.program_id(1) == 0)
    def _():
        s_ref[...] = s0_ref[...]

    inv_d = 1.0 / RET_HEAD_DIM
    bdv = bdv_ref[...]
    lane = lax.broadcasted_iota(jnp.int32, (1, RET_WIDTH), 1)
    head_k = (lane % LANES) // (RET_HEAD_DIM // 2)
    head_v = lane // RET_HEAD_DIM
    nsb = x_ref.shape[0] // blk
    intra = []
    for sb in range(nsb):
        rows = slice(sb * blk, (sb + 1) * blk)
        q = x_ref[rows, 0:256]
        qst = jnp.concatenate([jnp.where(head_k == hd, q, jnp.zeros_like(q)) for hd in range(RET_HEADS)], axis=0)
        a = (_dot_nt(qst, x_ref[rows, 256:512]) * dec_ref[...]).astype(BF16)
        oa = _dot(a, x_ref[rows, 512:768])
        acc = jnp.where(head_v == 0, oa[0:blk, :], 0.0)
        for hd in range(1, RET_HEADS):
            acc = acc + jnp.where(head_v == hd, oa[hd * blk:(hd + 1) * blk, :], 0.0)
        intra.append(acc)
    for sb in range(nsb):
        rows = slice(sb * blk, (sb + 1) * blk)
        q = x_ref[rows, 0:256]
        k = x_ref[rows, 256:512]
        v = x_ref[rows, 512:768]
        gate = x_ref[rows, 768:1024].astype(F32)
        state = s_ref[...]
        qw = (q.astype(F32) * qs_ref[...]).astype(BF16)
        o = intra[sb] + _dot(qw, state.astype(BF16))
        kw_t = (k.astype(F32) * ks_ref[...]).T.astype(BF16)
        s_ref[...] = gbm_ref[...] * state + bm_ref[...] * _dot(kw_t, v)
        hi, lo = _split2(o)
        mu2 = _dot(jnp.concatenate([hi, lo], axis=0), bdv)
        d = o - (mu2[0:blk] + mu2[blk:2 * blk]) * inv_d
        hi, lo = _split2(d * d)
        var2 = _dot(jnp.concatenate([hi, lo], axis=0), bdv)
        var = (var2[0:blk] + var2[blk:2 * blk]) * inv_d
        on = d * lax.rsqrt(var + NORM_EPS) * gain_ref[...]
        o_ref[rows, :] = (on * (gate / (1.0 + jnp.exp(-gate)))).astype(BF16)
    sout_ref[...] = s_ref[...]


def _ret_constants(blk, chunk_len):
    log_g = jnp.log1p(-jnp.exp2(-5.0 - jnp.arange(RET_HEADS, dtype=F32)))
    lane = np.arange(RET_WIDTH)
    head_k = (lane % LANES) // (RET_HEAD_DIM // 2)
    head_v = lane // RET_HEAD_DIM
    i = jnp.arange(blk, dtype=F32)
    diff = i[:, None] - i[None, :]
    dec = jnp.where(diff[None] >= 0.0, jnp.exp(jnp.maximum(diff, 0.0)[None] * log_g[:, None, None]), 0.0)
    dec = dec.reshape(RET_HEADS * blk, blk)
    lg_k = log_g[head_k]
    qs = jnp.exp((i + 1.0)[:, None] * lg_k[None, :])
    ks = jnp.exp((chunk_len - 1.0 - i)[:, None] * lg_k[None, :])
    gbm = jnp.broadcast_to(jnp.exp(chunk_len * lg_k)[:, None], (RET_WIDTH, RET_WIDTH))
    bm = jnp.asarray(head_k[:, None] == head_v[None, :], F32)
    bdv = jnp.asarray(head_v[:, None] == head_v[None, :], BF16)
    return dec, qs, ks, gbm, bm, bdv


def _retention(l, ret_in, s0, gain, consts, blk, rt):
    g, r, _ = ret_in.shape
    grid = (g, r // rt)
    const2 = lambda shape: pl.BlockSpec(shape, lambda b, i: (0,) * len(shape))
    dec, qs, ks, gbm, bm, bdv = consts
    in_specs = [
        pl.BlockSpec((None, rt, 1024), lambda b, i: (b, i, 0)),
        pl.BlockSpec((None, RET_WIDTH, RET_WIDTH), lambda b, i: (b, 0, 0)),
        pl.BlockSpec((None, 1, RET_WIDTH), lambda b, i: (l, 0, 0)),
        const2(dec.shape), const2(qs.shape), const2(ks.shape), const2(gbm.shape), const2(bm.shape),
        const2(bdv.shape),
    ]
    return pl.pallas_call(
        functools.partial(_ret_kernel, blk=blk),
        grid=grid, in_specs=in_specs,
        out_specs=[pl.BlockSpec((None, rt, RET_WIDTH), lambda b, i: (b, i, 0)),
                   pl.BlockSpec((None, RET_WIDTH, RET_WIDTH), lambda b, i: (b, 0, 0))],
        out_shape=[jax.ShapeDtypeStruct((g, r, RET_WIDTH), BF16),
                   jax.ShapeDtypeStruct((g, RET_WIDTH, RET_WIDTH), F32)],
        scratch_shapes=[pltpu.VMEM((RET_WIDTH, RET_WIDTH), F32)],
        compiler_params=pltpu.CompilerParams(dimension_semantics=("arbitrary", "arbitrary"),
                                             vmem_limit_bytes=VMEM_LIMIT),
        name="retention",
    )(ret_in, s0, gain, dec, qs, ks, gbm, bm, bdv)


def _state_to_bd(st):
    g = st.shape[0]
    t = jnp.einsum("bhpie,hg->bphige", st.reshape(g, RET_HEADS, 2, 32, 64), jnp.eye(RET_HEADS, dtype=st.dtype))
    return t.reshape(g, RET_WIDTH, RET_WIDTH)


def _bd_to_state(sbd):
    g = sbd.shape[0]
    t = jnp.einsum("bphihe->bhpie", sbd.reshape(g, 2, RET_HEADS, 32, RET_HEADS, 64))
    return t.reshape(g, RET_HEADS, 64, 64)


def _cumsum_kernel(x_ref, u_ref, o_ref):
    nc = x_ref.shape[1] // LANES
    xs = jnp.concatenate([x_ref[:, LANES * j:LANES * (j + 1)] for j in range(nc)], axis=0)
    x1 = xs.astype(BF16)
    r1 = xs - x1.astype(F32)
    x2 = r1.astype(BF16)
    x3 = (r1 - x2.astype(F32)).astype(BF16)
    tri = u_ref[...]
    loc = _dot(x1, tri) + _dot(x2, tri) + _dot(x3, tri)
    off = jnp.zeros((SUBLANES, 1), F32)
    for j in range(nc):
        lj = loc[SUBLANES * j:SUBLANES * (j + 1), :]
        o_ref[:, LANES * j:LANES * (j + 1)] = (lj + off) * (-LOG2E)
        off = off + lj[:, LANES - 1:LANES]


def _cumsum_lanes(x):
    n, h, w = x.shape
    tri = jnp.asarray(np.arange(LANES)[:, None] <= np.arange(LANES)[None, :], BF16)
    return pl.pallas_call(
        _cumsum_kernel,
        grid=(n,),
        in_specs=[pl.BlockSpec((None, h, w), lambda b: (b, 0, 0)),
                  pl.BlockSpec((LANES, LANES), lambda b: (0, 0))],
        out_specs=pl.BlockSpec((None, h, w), lambda b: (b, 0, 0)),
        out_shape=jax.ShapeDtypeStruct((n, h, w), F32),
        compiler_params=pltpu.CompilerParams(dimension_semantics=("arbitrary",)),
        name="forget_cumsum",
    )(x, tri)


def _fox_prompt_kernel(q_ref, k_ref, vt_ref, nc_ref, o_ref, va_ref, ncol_ref, qh_ref, sd_ref, sa_ref, sb_ref,
                       m_ref, acc_ref, *, tile):
    nq = q_ref.shape[0] // tile
    lane = lax.broadcasted_iota(jnp.int32, (1, LANES), 1)
    half = tile // 2
    krow = lax.broadcasted_iota(jnp.int32, (half, tile), 0)
    qcol = lax.broadcasted_iota(jnp.int32, (half, tile), 1)
    brow = lax.broadcasted_iota(jnp.int32, (half, half), 0)
    bcol = lax.broadcasted_iota(jnp.int32, (half, half), 1)
    for j in range(nq):
        for hd in range(2):
            va_ref[hd, j, 0:FOX_HEAD_DIM, :] = vt_ref[FOX_HEAD_DIM * hd:FOX_HEAD_DIM * (hd + 1),
                                                       tile * j:tile * (j + 1)]
            va_ref[hd, j, FOX_HEAD_DIM:FOX_VROWS, :] = jnp.ones((FOX_VROWS - FOX_HEAD_DIM, tile), BF16)
        rows8 = jnp.concatenate([nc_ref[j], jnp.zeros((SUBLANES - 2, tile), F32)], axis=0)
        ncol_ref[j] = rows8.T

    def set_q(slot, qi):
        q = q_ref[pl.ds(pl.multiple_of(qi * tile, tile), tile), :]
        for hd in range(2):
            qh_ref[slot, hd] = jnp.where(lane // FOX_HEAD_DIM == hd, q, jnp.zeros_like(q))

    def park(dst_ref, qslot, kj, heads=(0, 1)):
        kt = k_ref[pl.ds(pl.multiple_of(kj * tile, tile), tile), :]
        ncol = ncol_ref[kj]
        for hd in heads:
            dst_ref[hd] = _dot_nt(kt, qh_ref[qslot, hd]) + ncol[:, hd:hd + 1]

    def consume(src_ref, kj, heads=(0, 1)):
        for hd in heads:
            s = src_ref[hd]
            m_prev = m_ref[hd]
            m_new = jnp.maximum(m_prev, jnp.max(s, axis=0, keepdims=True))
            alpha = jnp.exp2(m_prev - m_new)
            p = jnp.exp2(s - m_new).astype(BF16)
            acc_ref[hd] = alpha * acc_ref[hd] + _dot(va_ref[hd, kj], p)
            m_ref[hd] = m_new

    def park_diag(qslot, kj, heads=(0, 1)):
        k0 = pl.multiple_of(kj * tile, tile)
        for hd in heads:
            top = (_dot_nt(k_ref[pl.ds(k0, half), :], qh_ref[qslot, hd])
                   + ncol_ref[kj, 0:half, :][:, hd:hd + 1])
            sd_ref[hd, 0:half, :] = jnp.where(krow <= qcol, top, -jnp.inf)
            bot = (_dot_nt(k_ref[pl.ds(pl.multiple_of(k0 + half, half), half), :], qh_ref[qslot, hd, half:tile, :])
                   + ncol_ref[kj, half:tile, :][:, hd:hd + 1])
            sd_ref[hd, half:tile, half:tile] = jnp.where(brow <= bcol, bot, -jnp.inf)

    def consume_diag(kj, heads=(0, 1)):
        for hd in heads:
            top_l = sd_ref[hd, 0:half, 0:half]
            top_r = sd_ref[hd, 0:half, half:tile]
            bot = sd_ref[hd, half:tile, half:tile]
            m_l = jnp.max(top_l, axis=0, keepdims=True)
            m_r = jnp.maximum(jnp.max(top_r, axis=0, keepdims=True), jnp.max(bot, axis=0, keepdims=True))
            va_a = va_ref[hd, kj, :, 0:half]
            acc_ref[hd, :, 0:half] = _dot(va_a, jnp.exp2(top_l - m_l).astype(BF16))
            acc_ref[hd, :, half:tile] = (_dot(va_a, jnp.exp2(top_r - m_r).astype(BF16))
                                         + _dot(va_ref[hd, kj, :, half:tile], jnp.exp2(bot - m_r).astype(BF16)))
            m_ref[hd, :, 0:half] = m_l
            m_ref[hd, :, half:tile] = m_r

    def finish(qi):
        ot = jnp.concatenate(
            [acc_ref[hd, 0:FOX_HEAD_DIM, :] / acc_ref[hd, FOX_HEAD_DIM:FOX_HEAD_DIM + 1, :] for hd in range(2)],
            axis=0)
        o_ref[pl.ds(pl.multiple_of(qi * tile, tile), tile), :] = ot.T.astype(BF16)

    def per_head(first_fn, second_fn):
        for hd in range(2):
            first_fn((hd,))
            second_fn((hd,))

    def next_diag(qi):
        nxt = jnp.minimum(qi + 1, nq - 1)
        set_q(1 - qi % 2, nxt)
        return 1 - qi % 2, nxt

    set_q(0, 0)
    park_diag(0, 0)

    def q_body(qi, carry):
        qs = qi % 2

        def head(qi):
            per_head(lambda h: park(sa_ref, qs, 0, h), lambda h: consume_diag(qi, h))

            def kv_pair(jp, c2):
                j = 2 * jp + 1
                per_head(lambda h: park(sb_ref, qs, j, h), lambda h: consume(sa_ref, j - 1, h))
                per_head(lambda h: park(sa_ref, qs, j + 1, h), lambda h: consume(sb_ref, j, h))
                return c2

            lax.fori_loop(0, (qi - 1) // 2, kv_pair, 0)

        @pl.when(qi == 0)
        def _():
            nslot, nxt = next_diag(qi)
            per_head(lambda h: consume_diag(0, h), lambda h: park_diag(nslot, nxt, h))
            finish(qi)

        @pl.when(jnp.logical_and(qi > 0, qi % 2 == 0))
        def _():
            head(qi)
            per_head(lambda h: park(sb_ref, qs, qi - 1, h), lambda h: consume(sa_ref, qi - 2, h))
            nslot, nxt = next_diag(qi)
            per_head(lambda h: park_diag(nslot, nxt, h), lambda h: consume(sb_ref, qi - 1, h))
            finish(qi)

        @pl.when(qi % 2 == 1)
        def _():
            head(qi)
            nslot, nxt = next_diag(qi)
            per_head(lambda h: park_diag(nslot, nxt, h), lambda h: consume(sa_ref, qi - 1, h))
            finish(qi)

        return carry

    lax.fori_loop(0, nq, q_body, 0)


def _fox_prompt(q, k, vt, nc_rows, tile):
    b, s, _ = q.shape
    nt = s // tile
    qk_spec = pl.BlockSpec((None, s, LANES), lambda bi, hp: (bi, 0, hp))
    return pl.pallas_call(
        functools.partial(_fox_prompt_kernel, tile=tile),
        grid=(b, FOX_HEADS // 2),
        in_specs=[qk_spec, qk_spec, pl.BlockSpec((None, LANES, s), lambda bi, hp: (bi, hp, 0)),
                  pl.BlockSpec((None, None, nt, 2, tile), lambda bi, hp: (bi, hp, 0, 0, 0))],
        out_specs=qk_spec,
        out_shape=jax.ShapeDtypeStruct((b, s, FOX_WIDTH), BF16),
        scratch_shapes=[pltpu.VMEM((2, nt, FOX_VROWS, tile), BF16), pltpu.VMEM((nt, tile, SUBLANES), F32),
                        pltpu.VMEM((2, 2, tile, LANES), BF16),
                        pltpu.VMEM((2, tile, tile), F32), pltpu.VMEM((2, tile, tile), F32),
                        pltpu.VMEM((2, tile, tile), F32),
                        pltpu.VMEM((2, 1, tile), F32), pltpu.VMEM((2, FOX_VROWS, tile), F32)],
        compiler_params=pltpu.CompilerParams(dimension_semantics=("arbitrary", "arbitrary"),
                                             vmem_limit_bytes=VMEM_LIMIT),
        name="fox_prompt",
    )(q, k, vt, nc_rows)


def _fox_sample_kernel(q_ref, kt_ref, vt_ref, knt_ref, vnt_ref, ncc_ref, ncn_ref, o_ref,
                       m_ref, l_ref, acc_ref):
    chunk = pl.program_id(1)
    n = q_ref.shape[0]
    lane = lax.broadcasted_iota(jnp.int32, (1, FOX_WIDTH), 1)

    @pl.when(chunk == 0)
    def _():
        m_ref[...] = jnp.full(m_ref.shape, -jnp.inf, F32)
        l_ref[...] = jnp.zeros(l_ref.shape, F32)
        acc_ref[...] = jnp.zeros(acc_ref.shape, F32)

    q = q_ref[...]
    qbd = jnp.concatenate(
        [jnp.where(lane // FOX_HEAD_DIM == hd, q, jnp.zeros_like(q)) for hd in range(FOX_HEADS)], axis=0)

    def head_rows(nc):
        return jnp.concatenate(
            [jnp.broadcast_to(nc[hd:hd + 1, :], (n, nc.shape[1])) for hd in range(FOX_HEADS)], axis=0)

    def update(s, vt):
        m_prev = m_ref[...]
        m_new = jnp.maximum(m_prev, jnp.max(s, axis=-1, keepdims=True))
        alpha = jnp.exp2(m_prev - m_new)
        p = jnp.exp2(s - m_new[:, 0:1])
        l_ref[...] = alpha * l_ref[...] + jnp.sum(p, axis=-1, keepdims=True)
        acc_ref[...] = alpha[:, 0:1] * acc_ref[...] + _dot_nt(p.astype(BF16), vt)
        m_ref[...] = m_new

    update(_dot(qbd, kt_ref[...].astype(BF16)) + head_rows(ncc_ref[...]), vt_ref[...].astype(BF16))

    @pl.when(chunk == pl.num_programs(1) - 1)
    def _():
        r = lax.broadcasted_iota(jnp.int32, (FOX_HEADS * n, n), 0)
        c = lax.broadcasted_iota(jnp.int32, (FOX_HEADS * n, n), 1)
        sn = _dot(qbd, knt_ref[...].astype(BF16)) + head_rows(ncn_ref[:, 0:n])
        update(jnp.where(c <= r % n, sn, -jnp.inf), vnt_ref[...].astype(BF16))
        o_all = acc_ref[...] / l_ref[:, 0:1]
        o = jnp.zeros((n, FOX_WIDTH), F32)
        for hd in range(FOX_HEADS):
            o = o + jnp.where(lane // FOX_HEAD_DIM == hd, o_all[n * hd:n * (hd + 1), :], 0.0)
        o_ref[...] = o.astype(BF16)


def _fox_sample(l, q, knt, vnt, cache_kt, cache_vt, nc):
    nb, n, _ = q.shape
    past = cache_kt.shape[3]
    ck = min(FOX_SAMPLE_CHUNK, past)
    q_spec = pl.BlockSpec((None, n, FOX_WIDTH), lambda b, c: (b, 0, 0))
    new_spec = pl.BlockSpec((None, FOX_WIDTH, n), lambda b, c: (b, 0, 0))
    cache_spec = pl.BlockSpec((None, None, FOX_WIDTH, ck), lambda b, c: (l, b, 0, c))
    rows = FOX_HEADS * n
    return pl.pallas_call(
        _fox_sample_kernel,
        grid=(nb, past // ck),
        in_specs=[q_spec, cache_spec, cache_spec, new_spec, new_spec,
                  pl.BlockSpec((None, FOX_HEADS, ck), lambda b, c: (b, 0, c)),
                  pl.BlockSpec((None, FOX_HEADS, LANES), lambda b, c: (b, 0, past // LANES))],
        out_specs=q_spec,
        out_shape=jax.ShapeDtypeStruct((nb, n, FOX_WIDTH), BF16),
        scratch_shapes=[pltpu.VMEM((rows, LANES), F32), pltpu.VMEM((rows, LANES), F32),
                        pltpu.VMEM((rows, FOX_WIDTH), F32)],
        compiler_params=pltpu.CompilerParams(dimension_semantics=("arbitrary", "arbitrary"),
                                             vmem_limit_bytes=VMEM_LIMIT),
        name="fox_sample",
    )(q, cache_kt, cache_vt, knt, vnt, nc, nc)


def _mlp_kernel(*refs, tmajor, nb):
    if tmajor:
        (x_ref, oret_ref, ofox_ref, oconv_ref, wout_ref, nw_ref, wg_ref, wu_ref, wd_ref, fcw_ref, hist_ref,
         y_ref, st_ref) = refs
    else:
        (x_ref, oret_ref, ofox_ref, oconv_ref, wout_ref, nw_ref, wg_ref, wu_ref, wd_ref, fcw_ref,
         y_ref, st_ref, carry_ref) = refs
    tm = x_ref.shape[0]
    rb = tm if tmajor else min(MLP_ROW_BLOCK, tm)
    for r0 in range(0, tm, rb):
        rows = slice(r0, r0 + rb)
        x1 = (x_ref[rows, :] + _dot(oret_ref[rows, :], wout_ref[0:256, :])
              + _dot(ofox_ref[rows, :], wout_ref[256:768, :]) + _dot(oconv_ref[rows, :], wout_ref[768:1024, :]))
        h2 = _rms(x1, nw_ref[...]).astype(BF16)
        down = None
        for c0, c1 in FF_CHUNKS:
            cols = slice(c0, c1)
            gate_pre = _dot(h2, wg_ref[:, cols])
            if tmajor:
                gate_c, new_hist = _dwconv_tmajor(gate_pre, fcw_ref.at[:, cols], hist_ref[:, cols], nb)
                st_ref[:, cols] = new_hist
            else:
                first = (pl.program_id(1) == 0) if r0 == 0 else None
                gate_c = _dwconv_carry(gate_pre, fcw_ref.at[:, cols], carry_ref.at[:, cols], first)
                if r0 + rb == tm:
                    st_ref[:, cols] = carry_ref[6:8, cols]
            up = _dot(h2, wu_ref[:, cols])
            act = (gate_c / (1.0 + jnp.exp(-gate_c)) * up).astype(BF16)
            part = _dot(act, wd_ref[cols, :])
            down = part if down is None else down + part
        y_ref[rows, :] = x1 + down


def _mlp(l, x, oret, ofox, oconv, prm, *, hist=None, nb=0):
    tmajor = hist is not None
    g, r, _ = x.shape
    tm = min(TOKEN_TILE, r)
    grid = (g, r // tm)
    lsel = lambda b, i: (l, 0, 0)
    tok = lambda width: pl.BlockSpec((None, tm, width), lambda b, i: (b, i, 0))
    wspec = lambda rows, cols: pl.BlockSpec((None, rows, cols), lsel, pipeline_mode=pl.Buffered(1))
    in_specs = [tok(D_MODEL), tok(RET_WIDTH), tok(FOX_WIDTH), tok(CONV_DIM),
                wspec(D_MODEL, D_MODEL), pl.BlockSpec((None, 1, D_MODEL), lsel),
                wspec(D_MODEL, D_FF), wspec(D_MODEL, D_FF), wspec(D_FF, D_MODEL),
                pl.BlockSpec((None, 3, D_FF), lsel)]
    args = [x, oret, ofox, oconv, prm["w_out"], prm["norm_ffn"], prm["w_gate"], prm["w_up"], prm["w_down"],
            prm["ffn_conv_w"]]
    if tmajor:
        in_specs.append(pl.BlockSpec((2 * nb, D_FF), lambda b, i: (0, 0)))
        args.append(hist)
        st_shape = jax.ShapeDtypeStruct((2 * nb, D_FF), F32)
        st_spec = pl.BlockSpec((2 * nb, D_FF), lambda b, i: (0, 0))
        scratch = []
    else:
        st_shape = jax.ShapeDtypeStruct((g, 2, D_FF), F32)
        st_spec = pl.BlockSpec((None, 2, D_FF), lambda b, i: (b, 0, 0))
        scratch = [pltpu.VMEM((SUBLANES, D_FF), F32)]
    return pl.pallas_call(
        functools.partial(_mlp_kernel, tmajor=tmajor, nb=nb),
        grid=grid, in_specs=in_specs,
        out_specs=[tok(D_MODEL), st_spec],
        out_shape=[jax.ShapeDtypeStruct((g, r, D_MODEL), F32), st_shape],
        scratch_shapes=scratch,
        compiler_params=pltpu.CompilerParams(dimension_semantics=("arbitrary", "arbitrary"),
                                             vmem_limit_bytes=VMEM_LIMIT),
        name="outproj_mlp",
    )(*args)


def _permute_w_in(w_in):
    w = w_in.astype(BF16)
    depth, rows, _ = w.shape

    def halves_first(blk):
        return blk.reshape(depth, rows, RET_HEADS, 2, RET_HEAD_DIM // 2).transpose(0, 1, 3, 2, 4).reshape(
            depth, rows, RET_WIDTH)

    fl0 = 4 * RET_WIDTH + 3 * FOX_WIDTH
    parts = [halves_first(w[..., 0:RET_WIDTH]), halves_first(w[..., RET_WIDTH:2 * RET_WIDTH]),
             w[..., 2 * RET_WIDTH:fl0], w[..., fl0 + FOX_HEADS:PROJ_DIM], w[..., fl0:fl0 + FOX_HEADS],
             jnp.zeros((depth, rows, PROJ_PAD - PROJ_DIM), BF16)]
    return jnp.concatenate(parts, axis=2)


def _rope_tables(pos):
    half = 32
    inv_freq = ROPE_BASE ** (-jnp.arange(half, dtype=F32) / half)
    ang = pos.astype(F32)[:, None] * inv_freq[None, :]
    return jnp.tile(jnp.cos(ang), (1, RET_HEADS)), jnp.tile(jnp.sin(ang), (1, RET_HEADS))


def _to_bmajor(a, n, nb):
    return a.reshape(n, nb, a.shape[-1]).transpose(1, 0, 2)


def _to_tmajor(a):
    nb, n, c = a.shape
    return a.transpose(1, 0, 2).reshape(1, n * nb, c)


def kernel(x_prompt, x_sample, cache_fox_k, cache_fox_v, cache_fox_logf, state_ret, state_conv, state_ffn_conv,
           norm_mix, w_in, ret_gn_gain, fox_q_gain, fox_k_gain, fox_f_bias, conv_w, w_out, norm_ffn, w_gate,
           w_up, ffn_conv_w, w_down):
    depth = w_in.shape[0]
    b, s, _ = x_prompt.shape
    nb, n, _ = x_sample.shape
    past = cache_fox_k.shape[2]

    w_perm = _permute_w_in(w_in)
    prm = {
        "w_in": w_perm,
        "norm_mix": norm_mix.reshape(depth, 1, D_MODEL),
        "gq": fox_q_gain.reshape(depth, 1, FOX_WIDTH),
        "gk": fox_k_gain.reshape(depth, 1, FOX_WIDTH),
        "fb": jnp.pad(fox_f_bias, ((0, 0), (0, LANES - FOX_HEADS))).reshape(depth, 1, LANES),
        "conv_w": conv_w,
        "w_out": w_out.astype(BF16),
        "norm_ffn": norm_ffn.reshape(depth, 1, D_MODEL),
        "w_gate": w_gate.astype(BF16),
        "w_up": w_up.astype(BF16),
        "w_down": w_down.astype(BF16),
        "ffn_conv_w": ffn_conv_w,
    }
    gain = ret_gn_gain.reshape(depth, 1, RET_WIDTH)
    lane = np.arange(FOX_WIDTH)
    bd = jnp.asarray(lane[:, None] // FOX_HEAD_DIM == lane[None, :] // FOX_HEAD_DIM, BF16)

    cos_p, sin_p = _rope_tables(jnp.arange(s, dtype=jnp.int32))
    pos_s = past + jnp.arange(n, dtype=jnp.int32)
    cos_s, sin_s = _rope_tables(jnp.repeat(pos_s, nb))

    ret_blk = min(RET_BLOCK, s)
    ret_rt = min(TOKEN_TILE, s)
    consts_p = _ret_constants(ret_blk, float(ret_blk))
    samp_rows = LANES
    consts_s = _ret_constants(samp_rows, float(n))
    fox_tile = min(FOX_TILE, s)

    cache_k = jnp.transpose(cache_fox_k, (0, 1, 3, 4, 2)).reshape(depth, nb, FOX_WIDTH, past)
    cache_v = jnp.transpose(cache_fox_v, (0, 1, 3, 4, 2)).reshape(depth, nb, FOX_WIDTH, past)
    cache_lf_t = jnp.transpose(cache_fox_logf, (0, 1, 3, 2))
    c_width = -(-(past + n) // LANES) * LANES

    kbuf = jnp.zeros((depth, b, FOX_WIDTH, s), F32)
    vbuf = jnp.zeros((depth, b, FOX_WIDTH, s), F32)
    zero_state = jnp.zeros((b, RET_WIDTH, RET_WIDTH), F32)
    hist_c_all = jnp.transpose(state_conv, (0, 2, 1, 3)).reshape(depth, 2 * nb, CONV_DIM)
    hist_f_all = jnp.transpose(state_ffn_conv, (0, 2, 1, 3)).reshape(depth, 2 * nb, D_FF)
    state_bd_all = _state_to_bd(state_ret.reshape(depth * nb, RET_HEADS, 64, 64)).reshape(
        depth, nb, RET_WIDTH, RET_WIDTH)

    xp = x_prompt
    xs = jnp.transpose(x_sample, (1, 0, 2)).reshape(1, n * nb, D_MODEL)
    p_lf, p_ret, p_conv, p_ffn = [], [], [], []
    s_k, s_v, s_lf, s_ret, s_conv, s_ffn = [], [], [], [], [], []
    for l in range(depth):
        ret_in, fq, kbuf, vbuf, fkb, fvb, lf, lf_t, oconv, conv_st = _inproj(
            l, xp, prm, cos_p, sin_p, bd, kbuf=kbuf, vbuf=vbuf)
        oret, sbd = _retention(l, ret_in, zero_state, gain, consts_p, ret_blk, ret_rt)
        c_rows = _cumsum_lanes(lf_t)
        nc_rows = c_rows.reshape(b, FOX_HEADS // 2, 2, s // fox_tile, fox_tile).transpose(0, 1, 3, 2, 4)
        ofox = _fox_prompt(fq, fkb, fvb, nc_rows, fox_tile)
        xp, ffn_st = _mlp(l, xp, oret, ofox, oconv, prm)
        p_lf.append(lf)
        p_ret.append(sbd)
        p_conv.append(conv_st)
        p_ffn.append(ffn_st)

        hist_c = hist_c_all[l]
        ret_in, fq, fk32, fv32, fkb, fvb, lf, _, oconv, conv_st = _inproj(
            l, xs, prm, cos_s, sin_s, bd, hist=hist_c, nb=nb)
        ret_b = jnp.pad(_to_bmajor(ret_in, n, nb), ((0, 0), (0, samp_rows - n), (0, 0)))
        oret, sbd = _retention(l, ret_b, state_bd_all[l], gain, consts_s, samp_rows, samp_rows)
        lf_b = _to_bmajor(lf, n, nb)
        lf_all = jnp.concatenate(
            [cache_lf_t[l], jnp.transpose(lf_b, (0, 2, 1)),
             jnp.zeros((nb, FOX_HEADS, c_width - past - n), F32)], axis=2)
        nc = _cumsum_lanes(lf_all)
        k_new = _to_bmajor(fk32, n, nb)
        v_new = _to_bmajor(fv32, n, nb)
        s_k.append(k_new.reshape(nb, n, FOX_HEADS, FOX_HEAD_DIM))
        s_v.append(v_new.reshape(nb, n, FOX_HEADS, FOX_HEAD_DIM))
        ofox = _fox_sample(l, _to_bmajor(fq, n, nb), jnp.transpose(k_new, (0, 2, 1)),
                           jnp.transpose(v_new, (0, 2, 1)), cache_k, cache_v, nc)
        hist_f = hist_f_all[l]
        xs, ffn_st = _mlp(l, xs, _to_tmajor(oret[:, :n]), _to_tmajor(ofox), oconv, prm, hist=hist_f, nb=nb)
        s_lf.append(lf_b)
        s_ret.append(sbd)
        s_conv.append(conv_st)
        s_ffn.append(ffn_st)

    y_sample = xs.reshape(n, nb, D_MODEL).transpose(1, 0, 2)
    stk = lambda ts: jnp.stack(ts, axis=0)
    ret_states = lambda ts, g: _bd_to_state(stk(ts).reshape(depth * g, RET_WIDTH, RET_WIDTH)).reshape(
        depth, g, RET_HEADS, 64, 64)
    conv_states = lambda ts, c: stk(ts).reshape(depth, 2, nb, c).transpose(0, 2, 1, 3)
    from_fm = lambda a: jnp.transpose(a.reshape(depth, b, FOX_HEADS, FOX_HEAD_DIM, s), (0, 1, 4, 2, 3))
    return (xp, y_sample,
            from_fm(kbuf), from_fm(vbuf),
            stk(p_lf), ret_states(p_ret, b), stk(p_conv), stk(p_ffn),
            stk(s_k), stk(s_v), stk(s_lf), ret_states(s_ret, nb), conv_states(s_conv, CONV_DIM),
            conv_states(s_ffn, D_FF))
```

```python
import functools

import numpy as np
import jax
import jax.numpy as jnp
from jax import lax
from jax.experimental import pallas as pl
from jax.experimental.pallas import tpu as pltpu

F32 = jnp.float32
BF16 = jnp.bfloat16

D_MODEL = 1024
RET_HEADS = 4
RET_WIDTH = 256
FOX_HEADS = 8
FOX_HEAD_DIM = 64
FOX_WIDTH = 512
CONV_DIM = 256
D_FF = 2816
RET_HEAD_DIM = RET_WIDTH // RET_HEADS
RET_K_SCALE = RET_HEAD_DIM ** -0.5
PROJ_DIM = 4 * RET_WIDTH + 3 * FOX_WIDTH + FOX_HEADS + 3 * CONV_DIM
COLS_RET = (0, 4 * RET_WIDTH)
COLS_FOX = (COLS_RET[1], COLS_RET[1] + 3 * FOX_WIDTH)
COLS_CONV = (COLS_FOX[1], COLS_FOX[1] + 3 * CONV_DIM)
COLS_FORGET = (COLS_CONV[1], COLS_CONV[1] + 128)
PROJ_PAD = COLS_FORGET[1]
ROPE_BASE = 10000.0
NORM_EPS = 1e-6
LANES = 128
SUBLANES = 8
VMEM_LIMIT = 56 * 1024 * 1024

TOKEN_TILE = 512
RET_BLOCK = 256
MLP_ROW_BLOCK = 256
FOX_TILE = 512
FOX_SAMPLE_CHUNK = 2048
FOX_VROWS = 80
LOG2E = 1.4426950408889634
FOX_Q_SCALE = FOX_HEAD_DIM ** -0.5 * LOG2E
MXU_DEPTH = 256
FF_CHUNKS = ((0, D_FF),)


def _dot(a, b):
    return jnp.dot(a, b, preferred_element_type=F32)


def _dot_nt(a, b):
    return lax.dot_general(a, b, (((1,), (1,)), ((), ())), preferred_element_type=F32)


def _split2(x):
    hi = x.astype(BF16)
    lo = (x - hi.astype(F32)).astype(BF16)
    return hi, lo


def _group_sum(x, ones_bd):
    hi, lo = _split2(x)
    return _dot(hi, ones_bd) + _dot(lo, ones_bd)


def _rms(x, g):
    ms = jnp.mean(x * x, axis=-1, keepdims=True)
    return x * lax.rsqrt(ms + NORM_EPS) * g


def _dwconv_carry(u, w_ref, carry_ref, first):
    tm = u.shape[0]

    if first is not None:
        @pl.when(first)
        def _():
            carry_ref[...] = jnp.zeros(carry_ref.shape, F32)

    row = lax.broadcasted_iota(jnp.int32, u.shape, 0)
    c6 = carry_ref[6:7, :]
    c7 = carry_ref[7:8, :]
    um1 = jnp.where(row == 0, c7, pltpu.roll(u, 1, axis=0))
    um2 = jnp.where(row == 0, c6, jnp.where(row == 1, c7, pltpu.roll(u, 2, axis=0)))
    y = w_ref[0:1, :] * um2 + w_ref[1:2, :] * um1 + w_ref[2:3, :] * u
    carry_ref[...] = u[tm - SUBLANES:tm, :]
    return y


def _dwconv_tmajor(u, w_ref, hist, nb):
    tm = u.shape[0]
    full = jnp.concatenate([hist, u], axis=0)
    y = (w_ref[0:1, :] * full[0:tm] + w_ref[1:2, :] * full[nb:nb + tm]
         + w_ref[2:3, :] * full[2 * nb:2 * nb + tm])
    return y, full[tm:tm + 2 * nb]


def _inproj_kernel(*refs, tmajor, nb):
    if tmajor:
        (x_ref, nw_ref, w_ref, cos_ref, sin_ref, gq_ref, gk_ref, fb_ref, cw_ref, bd_ref, hist_ref,
         ret_ref, fq_ref, fk32_ref, fv32_ref, fkb_ref, fvb_ref, lf_ref, lft_ref, oc_ref, st_ref) = refs
    else:
        (x_ref, nw_ref, w_ref, cos_ref, sin_ref, gq_ref, gk_ref, fb_ref, cw_ref, bd_ref,
         _, _, ret_ref, fq_ref, fk32_ref, fv32_ref, fkb_ref, fvb_ref, lf_ref, lft_ref, oc_ref, st_ref,
         carry_ref) = refs
    h = _rms(x_ref[...], nw_ref[...]).astype(BF16)

    a = _dot(h, w_ref[:, COLS_RET[0]:COLS_RET[1]])
    f = _dot(h, w_ref[:, COLS_FOX[0]:COLS_FOX[1]])
    c = _dot(h, w_ref[:, COLS_CONV[0]:COLS_CONV[1]])
    z = _dot(h, w_ref[:, COLS_FORGET[0]:COLS_FORGET[1]]) + fb_ref[...]

    cos = cos_ref[...]
    sin = sin_ref[...]
    q1, q2, k1, k2 = a[:, 0:128], a[:, 128:256], a[:, 256:384], a[:, 384:512]
    ret_ref[:, 0:128] = (q1 * cos - q2 * sin).astype(BF16)
    ret_ref[:, 128:256] = (q1 * sin + q2 * cos).astype(BF16)
    ret_ref[:, 256:384] = ((k1 * cos - k2 * sin) * RET_K_SCALE).astype(BF16)
    ret_ref[:, 384:512] = ((k1 * sin + k2 * cos) * RET_K_SCALE).astype(BF16)
    ret_ref[:, 512:1024] = a[:, 512:1024].astype(BF16)

    fq, fk, fv = f[:, 0:512], f[:, 512:1024], f[:, 1024:1536]
    bd = bd_ref[...]
    inv_d = 1.0 / FOX_HEAD_DIM
    ssq = _dot((fq * fq).astype(BF16), bd)
    fq_ref[...] = (fq * lax.rsqrt(ssq * inv_d + NORM_EPS) * gq_ref[...] * FOX_Q_SCALE).astype(BF16)
    ssk = _group_sum(fk * fk, bd)
    fkn = fk * lax.rsqrt(ssk * inv_d + NORM_EPS) * gk_ref[...]
    fkb_ref[...] = fkn.astype(BF16)
    if tmajor:
        fk32_ref[...] = fkn
        fv32_ref[...] = fv
        fvb_ref[...] = fv.astype(BF16)
    else:
        fk32_ref[...] = fkn.T
        fvt = fv.T
        fv32_ref[...] = fvt
        fvb_ref[...] = fvt.astype(BF16)

    cb, cc, ch = c[:, 0:256], c[:, 256:512], c[:, 512:768]
    u = cc * ch
    if tmajor:
        y, new_hist = _dwconv_tmajor(u, cw_ref, hist_ref[...], nb)
        st_ref[...] = new_hist
    else:
        y = _dwconv_carry(u, cw_ref, carry_ref, pl.program_id(1) == 0)
        st_ref[...] = carry_ref[6:8, :]
    oc_ref[...] = (cb * y).astype(BF16)

    ls = jnp.minimum(z, 0.0) - jnp.log1p(jnp.exp(-jnp.abs(z)))
    lf_ref[...] = ls[:, 0:FOX_HEADS]
    lft_ref[...] = ls.T[0:FOX_HEADS, :]


def _inproj(l, x, prm, cos, sin, bd, *, kbuf=None, vbuf=None, hist=None, nb=0):
    tmajor = hist is not None
    g, r, _ = x.shape
    tm = min(TOKEN_TILE, r)
    grid = (g, r // tm)
    depth = prm["w_in"].shape[0]
    lsel = lambda b, i: (l, 0, 0)
    in_specs = [
        pl.BlockSpec((None, tm, D_MODEL), lambda b, i: (b, i, 0)),
        pl.BlockSpec((None, 1, D_MODEL), lsel),
        pl.BlockSpec((None, D_MODEL, PROJ_PAD), lsel, pipeline_mode=pl.Buffered(1)),
        pl.BlockSpec((tm, LANES), lambda b, i: (i, 0)),
        pl.BlockSpec((tm, LANES), lambda b, i: (i, 0)),
        pl.BlockSpec((None, 1, FOX_WIDTH), lsel),
        pl.BlockSpec((None, 1, FOX_WIDTH), lsel),
        pl.BlockSpec((None, 1, LANES), lsel),
        pl.BlockSpec((None, 3, CONV_DIM), lsel),
        pl.BlockSpec((FOX_WIDTH, FOX_WIDTH), lambda b, i: (0, 0)),
    ]
    args = [x, prm["norm_mix"], prm["w_in"], cos, sin, prm["gq"], prm["gk"], prm["fb"], prm["conv_w"], bd]
    tok = lambda width: pl.BlockSpec((None, tm, width), lambda b, i: (b, i, 0))
    if tmajor:
        in_specs.append(pl.BlockSpec((2 * nb, CONV_DIM), lambda b, i: (0, 0)))
        args.append(hist)
        kv_shape = jax.ShapeDtypeStruct((g, r, FOX_WIDTH), F32)
        kv_spec = tok(FOX_WIDTH)
        vb_shape = jax.ShapeDtypeStruct((g, r, FOX_WIDTH), BF16)
        vb_spec = tok(FOX_WIDTH)
        st_shape = jax.ShapeDtypeStruct((2 * nb, CONV_DIM), F32)
        st_spec = pl.BlockSpec((2 * nb, CONV_DIM), lambda b, i: (0, 0))
        aliases = {}
        scratch = []
    else:
        in_specs += [pl.BlockSpec(memory_space=pl.ANY), pl.BlockSpec(memory_space=pl.ANY)]
        args += [kbuf, vbuf]
        kv_shape = jax.ShapeDtypeStruct((depth, g, FOX_WIDTH, r), F32)
        kv_spec = pl.BlockSpec((None, None, FOX_WIDTH, tm), lambda b, i: (l, b, 0, i))
        vb_shape = jax.ShapeDtypeStruct((g, FOX_WIDTH, r), BF16)
        vb_spec = pl.BlockSpec((None, FOX_WIDTH, tm), lambda b, i: (b, 0, i))
        st_shape = jax.ShapeDtypeStruct((g, 2, CONV_DIM), F32)
        st_spec = pl.BlockSpec((None, 2, CONV_DIM), lambda b, i: (b, 0, 0))
        aliases = {10: 2, 11: 3}
        scratch = [pltpu.VMEM((SUBLANES, CONV_DIM), F32)]
    out_shape = [
        jax.ShapeDtypeStruct((g, r, 1024), BF16),
        jax.ShapeDtypeStruct((g, r, FOX_WIDTH), BF16),
        kv_shape, kv_shape,
        jax.ShapeDtypeStruct((g, r, FOX_WIDTH), BF16),
        vb_shape,
        jax.ShapeDtypeStruct((g, r, FOX_HEADS), F32),
        jax.ShapeDtypeStruct((g, FOX_HEADS, r), F32),
        jax.ShapeDtypeStruct((g, r, CONV_DIM), BF16),
        st_shape,
    ]
    out_specs = [tok(1024), tok(FOX_WIDTH), kv_spec, kv_spec, tok(FOX_WIDTH), vb_spec,
                 tok(FOX_HEADS), pl.BlockSpec((None, FOX_HEADS, tm), lambda b, i: (b, 0, i)),
                 tok(CONV_DIM), st_spec]
    return pl.pallas_call(
        functools.partial(_inproj_kernel, tmajor=tmajor, nb=nb),
        grid=grid, in_specs=in_specs, out_specs=out_specs, out_shape=out_shape,
        scratch_shapes=scratch, input_output_aliases=aliases,
        compiler_params=pltpu.CompilerParams(dimension_semantics=("arbitrary", "arbitrary"),
                                             vmem_limit_bytes=VMEM_LIMIT),
        name="inproj",
    )(*args)


def _ret_kernel(x_ref, s0_ref, gain_ref, dec_ref, qs_ref, ks_ref, gbm_ref, bm_ref, bdv_ref,
                o_ref, sout_ref, s_ref, *, blk):
    @pl.when(pl.program_id(1) == 0)
    def _():
        s_ref[...] = s0_ref[...]

    inv_d = 1.0 / RET_HEAD_DIM
    bdv = bdv_ref[...]
    lane = lax.broadcasted_iota(jnp.int32, (1, RET_WIDTH), 1)
    head_k = (lane % LANES) // (RET_HEAD_DIM // 2)
    head_v = lane // RET_HEAD_DIM
    nsb = x_ref.shape[0] // blk
    intra = []
    for sb in range(nsb):
        rows = slice(sb * blk, (sb + 1) * blk)
        q = x_ref[rows, 0:256]
        qst = jnp.concatenate([jnp.where(head_k == hd, q, jnp.zeros_like(q)) for hd in range(RET_HEADS)], axis=0)
        a = (_dot_nt(qst, x_ref[rows, 256:512]) * dec_ref[...]).astype(BF16)
        oa = _dot(a, x_ref[rows, 512:768])
        acc = jnp.where(head_v == 0, oa[0:blk, :], 0.0)
        for hd in range(1, RET_HEADS):
            acc = acc + jnp.where(head_v == hd, oa[hd * blk:(hd + 1) * blk, :], 0.0)
        intra.append(acc)
    for sb in range(nsb):
        rows = slice(sb * blk, (sb + 1) * blk)
        q = x_ref[rows, 0:256]
        k = x_ref[rows, 256:512]
        v = x_ref[rows, 512:768]
        gate = x_ref[rows, 768:1024].astype(F32)
        state = s_ref[...]
        qw = (q.astype(F32) * qs_ref[...]).astype(BF16)
        o = intra[sb] + _dot(qw, state.astype(BF16))
        kw_t = (k.astype(F32) * ks_ref[...]).T.astype(BF16)
        s_ref[...] = gbm_ref[...] * state + bm_ref[...] * _dot(kw_t, v)
        hi, lo = _split2(o)
        mu2 = _dot(jnp.concatenate([hi, lo], axis=0), bdv)
        d = o - (mu2[0:blk] + mu2[blk:2 * blk]) * inv_d
        hi, lo = _split2(d * d)
        var2 = _dot(jnp.concatenate([hi, lo], axis=0), bdv)
        var = (var2[0:blk] + var2[blk:2 * blk]) * inv_d
        on = d * lax.rsqrt(var + NORM_EPS) * gain_ref[...]
        o_ref[rows, :] = (on * (gate / (1.0 + jnp.exp(-gate)))).astype(BF16)
    sout_ref[...] = s_ref[...]


def _ret_constants(blk, chunk_len):
    log_g = jnp.log1p(-jnp.exp2(-5.0 - jnp.arange(RET_HEADS, dtype=F32)))
    lane = np.arange(RET_WIDTH)
    head_k = (lane % LANES) // (RET_HEAD_DIM // 2)
    head_v = lane // RET_HEAD_DIM
    i = jnp.arange(blk, dtype=F32)
    diff = i[:, None] - i[None, :]
    dec = jnp.where(diff[None] >= 0.0, jnp.exp(jnp.maximum(diff, 0.0)[None] * log_g[:, None, None]), 0.0)
    dec = dec.reshape(RET_HEADS * blk, blk)
    lg_k = log_g[head_k]
    qs = jnp.exp((i + 1.0)[:, None] * lg_k[None, :])
    ks = jnp.exp((chunk_len - 1.0 - i)[:, None] * lg_k[None, :])
    gbm = jnp.broadcast_to(jnp.exp(chunk_len * lg_k)[:, None], (RET_WIDTH, RET_WIDTH))
    bm = jnp.asarray(head_k[:, None] == head_v[None, :], F32)
    bdv = jnp.asarray(head_v[:, None] == head_v[None, :], BF16)
    return dec, qs, ks, gbm, bm, bdv


def _retention(l, ret_in, s0, gain, consts, blk, rt):
    g, r, _ = ret_in.shape
    grid = (g, r // rt)
    const2 = lambda shape: pl.BlockSpec(shape, lambda b, i: (0,) * len(shape))
    dec, qs, ks, gbm, bm, bdv = consts
    in_specs = [
        pl.BlockSpec((None, rt, 1024), lambda b, i: (b, i, 0)),
        pl.BlockSpec((None, RET_WIDTH, RET_WIDTH), lambda b, i: (b, 0, 0)),
        pl.BlockSpec((None, 1, RET_WIDTH), lambda b, i: (l, 0, 0)),
        const2(dec.shape), const2(qs.shape), const2(ks.shape), const2(gbm.shape), const2(bm.shape),
        const2(bdv.shape),
    ]
    return pl.pallas_call(
        functools.partial(_ret_kernel, blk=blk),
        grid=grid, in_specs=in_specs,
        out_specs=[pl.BlockSpec((None, rt, RET_WIDTH), lambda b, i: (b, i, 0)),
                   pl.BlockSpec((None, RET_WIDTH, RET_WIDTH), lambda b, i: (b, 0, 0))],
        out_shape=[jax.ShapeDtypeStruct((g, r, RET_WIDTH), BF16),
                   jax.ShapeDtypeStruct((g, RET_WIDTH, RET_WIDTH), F32)],
        scratch_shapes=[pltpu.VMEM((RET_WIDTH, RET_WIDTH), F32)],
        compiler_params=pltpu.CompilerParams(dimension_semantics=("arbitrary", "arbitrary"),
                                             vmem_limit_bytes=VMEM_LIMIT),
        name="retention",
    )(ret_in, s0, gain, dec, qs, ks, gbm, bm, bdv)


def _state_to_bd(st):
    g = st.shape[0]
    t = jnp.einsum("bhpie,hg->bphige", st.reshape(g, RET_HEADS, 2, 32, 64), jnp.eye(RET_HEADS, dtype=st.dtype))
    return t.reshape(g, RET_WIDTH, RET_WIDTH)


def _bd_to_state(sbd):
    g = sbd.shape[0]
    t = jnp.einsum("bphihe->bhpie", sbd.reshape(g, 2, RET_HEADS, 32, RET_HEADS, 64))
    return t.reshape(g, RET_HEADS, 64, 64)


def _cumsum_kernel(x_ref, u_ref, o_ref):
    nc = x_ref.shape[1] // LANES
    xs = jnp.concatenate([x_ref[:, LANES * j:LANES * (j + 1)] for j in range(nc)], axis=0)
    x1 = xs.astype(BF16)
    r1 = xs - x1.astype(F32)
    x2 = r1.astype(BF16)
    x3 = (r1 - x2.astype(F32)).astype(BF16)
    tri = u_ref[...]
    loc = _dot(x1, tri) + _dot(x2, tri) + _dot(x3, tri)
    off = jnp.zeros((SUBLANES, 1), F32)
    for j in range(nc):
        lj = loc[SUBLANES * j:SUBLANES * (j + 1), :]
        o_ref[:, LANES * j:LANES * (j + 1)] = (lj + off) * (-LOG2E)
        off = off + lj[:, LANES - 1:LANES]


def _cumsum_lanes(x):
    n, h, w = x.shape
    tri = jnp.asarray(np.arange(LANES)[:, None] <= np.arange(LANES)[None, :], BF16)
    return pl.pallas_call(
        _cumsum_kernel,
        grid=(n,),
        in_specs=[pl.BlockSpec((None, h, w), lambda b: (b, 0, 0)),
                  pl.BlockSpec((LANES, LANES), lambda b: (0, 0))],
        out_specs=pl.BlockSpec((None, h, w), lambda b: (b, 0, 0)),
        out_shape=jax.ShapeDtypeStruct((n, h, w), F32),
        compiler_params=pltpu.CompilerParams(dimension_semantics=("arbitrary",)),
        name="forget_cumsum",
    )(x, tri)


def _fox_prompt_kernel(q_ref, k_ref, vt_ref, nc_ref, o_ref, va_ref, ncol_ref, qh_ref, sd_ref, sa_ref, sb_ref,
                       m_ref, acc_ref, *, tile):
    nq = q_ref.shape[0] // tile
    lane = lax.broadcasted_iota(jnp.int32, (1, LANES), 1)
    half = tile // 2
    krow = lax.broadcasted_iota(jnp.int32, (half, tile), 0)
    qcol = lax.broadcasted_iota(jnp.int32, (half, tile), 1)
    brow = lax.broadcasted_iota(jnp.int32, (half, half), 0)
    bcol = lax.broadcasted_iota(jnp.int32, (half, half), 1)
    for j in range(nq):
        for hd in range(2):
            va_ref[hd, j, 0:FOX_HEAD_DIM, :] = vt_ref[FOX_HEAD_DIM * hd:FOX_HEAD_DIM * (hd + 1),
                                                       tile * j:tile * (j + 1)]
            va_ref[hd, j, FOX_HEAD_DIM:FOX_VROWS, :] = jnp.ones((FOX_VROWS - FOX_HEAD_DIM, tile), BF16)
        rows8 = jnp.concatenate([nc_ref[j], jnp.zeros((SUBLANES - 2, tile), F32)], axis=0)
        ncol_ref[j] = rows8.T

    def set_q(slot, qi):
        q = q_ref[pl.ds(pl.multiple_of(qi * tile, tile), tile), :]
        for hd in range(2):
            qh_ref[slot, hd] = jnp.where(lane // FOX_HEAD_DIM == hd, q, jnp.zeros_like(q))

    def park(dst_ref, qslot, kj, heads=(0, 1)):
        kt = k_ref[pl.ds(pl.multiple_of(kj * tile, tile), tile), :]
        ncol = ncol_ref[kj]
        for hd in heads:
            dst_ref[hd] = _dot_nt(kt, qh_ref[qslot, hd]) + ncol[:, hd:hd + 1]

    def consume(src_ref, kj, heads=(0, 1)):
        for hd in heads:
            s = src_ref[hd]
            m_prev = m_ref[hd]
            m_new = jnp.maximum(m_prev, jnp.max(s, axis=0, keepdims=True))
            alpha = jnp.exp2(m_prev - m_new)
            p = jnp.exp2(s - m_new).astype(BF16)
            acc_ref[hd] = alpha * acc_ref[hd] + _dot(va_ref[hd, kj], p)
            m_ref[hd] = m_new

    def park_diag(qslot, kj, heads=(0, 1)):
        k0 = pl.multiple_of(kj * tile, tile)
        for hd in heads:
            top = (_dot_nt(k_ref[pl.ds(k0, half), :], qh_ref[qslot, hd])
                   + ncol_ref[kj, 0:half, :][:, hd:hd + 1])
            sd_ref[hd, 0:half, :] = jnp.where(krow <= qcol, top, -jnp.inf)
            bot = (_dot_nt(k_ref[pl.ds(pl.multiple_of(k0 + half, half), half), :], qh_ref[qslot, hd, half:tile, :])
                   + ncol_ref[kj, half:tile, :][:, hd:hd + 1])
            sd_ref[hd, half:tile, half:tile] = jnp.where(brow <= bcol, bot, -jnp.inf)

    def consume_diag(kj, heads=(0, 1)):
        for hd in heads:
            top_l = sd_ref[hd, 0:half, 0:half]
            top_r = sd_ref[hd, 0:half, half:tile]
            bot = sd_ref[hd, half:tile, half:tile]
            m_l = jnp.max(top_l, axis=0, keepdims=True)
            m_r = jnp.maximum(jnp.max(top_r, axis=0, keepdims=True), jnp.max(bot, axis=0, keepdims=True))
            va_a = va_ref[hd, kj, :, 0:half]
            acc_ref[hd, :, 0:half] = _dot(va_a, jnp.exp2(top_l - m_l).astype(BF16))
            acc_ref[hd, :, half:tile] = (_dot(va_a, jnp.exp2(top_r - m_r).astype(BF16))
                                         + _dot(va_ref[hd, kj, :, half:tile], jnp.exp2(bot - m_r).astype(BF16)))
            m_ref[hd, :, 0:half] = m_l
            m_ref[hd, :, half:tile] = m_r

    def finish(qi):
        ot = jnp.concatenate(
            [acc_ref[hd, 0:FOX_HEAD_DIM, :] / acc_ref[hd, FOX_HEAD_DIM:FOX_HEAD_DIM + 1, :] for hd in range(2)],
            axis=0)
        o_ref[pl.ds(pl.multiple_of(qi * tile, tile), tile), :] = ot.T.astype(BF16)

    def per_head(first_fn, second_fn):
        for hd in range(2):
            first_fn((hd,))
            second_fn((hd,))

    def next_diag(qi):
        nxt = jnp.minimum(qi + 1, nq - 1)
        set_q(1 - qi % 2, nxt)
        return 1 - qi % 2, nxt

    set_q(0, 0)
    park_diag(0, 0)

    def q_body(qi, carry):
        qs = qi % 2

        def head(qi):
            per_head(lambda h: park(sa_ref, qs, 0, h), lambda h: consume_diag(qi, h))

            def kv_pair(jp, c2):
                j = 2 * jp + 1
                per_head(lambda h: park(sb_ref, qs, j, h), lambda h: consume(sa_ref, j - 1, h))
                per_head(lambda h: park(sa_ref, qs, j + 1, h), lambda h: consume(sb_ref, j, h))
                return c2

            lax.fori_loop(0, (qi - 1) // 2, kv_pair, 0)

        @pl.when(qi == 0)
        def _():
            nslot, nxt = next_diag(qi)
            per_head(lambda h: consume_diag(0, h), lambda h: park_diag(nslot, nxt, h))
            finish(qi)

        @pl.when(jnp.logical_and(qi > 0, qi % 2 == 0))
        def _():
            head(qi)
            per_head(lambda h: park(sb_ref, qs, qi - 1, h), lambda h: consume(sa_ref, qi - 2, h))
            nslot, nxt = next_diag(qi)
            per_head(lambda h: park_diag(nslot, nxt, h), lambda h: consume(sb_ref, qi - 1, h))
            finish(qi)

        @pl.when(qi % 2 == 1)
        def _():
            head(qi)
            nslot, nxt = next_diag(qi)
            per_head(lambda h: park_diag(nslot, nxt, h), lambda h: consume(sa_ref, qi - 1, h))
            finish(qi)

        return carry

    lax.fori_loop(0, nq, q_body, 0)


def _fox_prompt(q, k, vt, nc_rows, tile):
    b, s, _ = q.shape
    nt = s // tile
    qk_spec = pl.BlockSpec((None, s, LANES), lambda bi, hp: (bi, 0, hp))
    return pl.pallas_call(
        functools.partial(_fox_prompt_kernel, tile=tile),
        grid=(b, FOX_HEADS // 2),
        in_specs=[qk_spec, qk_spec, pl.BlockSpec((None, LANES, s), lambda bi, hp: (bi, hp, 0)),
                  pl.BlockSpec((None, None, nt, 2, tile), lambda bi, hp: (bi, hp, 0, 0, 0))],
        out_specs=qk_spec,
        out_shape=jax.ShapeDtypeStruct((b, s, FOX_WIDTH), BF16),
        scratch_shapes=[pltpu.VMEM((2, nt, FOX_VROWS, tile), BF16), pltpu.VMEM((nt, tile, SUBLANES), F32),
                        pltpu.VMEM((2, 2, tile, LANES), BF16),
                        pltpu.VMEM((2, tile, tile), F32), pltpu.VMEM((2, tile, tile), F32),
                        pltpu.VMEM((2, tile, tile), F32),
                        pltpu.VMEM((2, 1, tile), F32), pltpu.VMEM((2, FOX_VROWS, tile), F32)],
        compiler_params=pltpu.CompilerParams(dimension_semantics=("arbitrary", "arbitrary"),
                                             vmem_limit_bytes=VMEM_LIMIT),
        name="fox_prompt",
    )(q, k, vt, nc_rows)


def _fox_sample_kernel(q_ref, kt_ref, vt_ref, knt_ref, vnt_ref, ncc_ref, ncn_ref, o_ref,
                       m_ref, l_ref, acc_ref):
    chunk = pl.program_id(1)
    n = q_ref.shape[0]
    lane = lax.broadcasted_iota(jnp.int32, (1, FOX_WIDTH), 1)

    @pl.when(chunk == 0)
    def _():
        m_ref[...] = jnp.full(m_ref.shape, -jnp.inf, F32)
        l_ref[...] = jnp.zeros(l_ref.shape, F32)
        acc_ref[...] = jnp.zeros(acc_ref.shape, F32)

    q = q_ref[...]
    qbd = jnp.concatenate(
        [jnp.where(lane // FOX_HEAD_DIM == hd, q, jnp.zeros_like(q)) for hd in range(FOX_HEADS)], axis=0)

    def head_rows(nc):
        return jnp.concatenate(
            [jnp.broadcast_to(nc[hd:hd + 1, :], (n, nc.shape[1])) for hd in range(FOX_HEADS)], axis=0)

    def update(s, vt):
        m_prev = m_ref[...]
        m_new = jnp.maximum(m_prev, jnp.max(s, axis=-1, keepdims=True))
        alpha = jnp.exp2(m_prev - m_new)
        p = jnp.exp2(s - m_new[:, 0:1])
        l_ref[...] = alpha * l_ref[...] + jnp.sum(p, axis=-1, keepdims=True)
        acc_ref[...] = alpha[:, 0:1] * acc_ref[...] + _dot_nt(p.astype(BF16), vt)
        m_ref[...] = m_new

    update(_dot(qbd, kt_ref[...].astype(BF16)) + head_rows(ncc_ref[...]), vt_ref[...].astype(BF16))

    @pl.when(chunk == pl.num_programs(1) - 1)
    def _():
        r = lax.broadcasted_iota(jnp.int32, (FOX_HEADS * n, n), 0)
        c = lax.broadcasted_iota(jnp.int32, (FOX_HEADS * n, n), 1)
        sn = _dot(qbd, knt_ref[...].astype(BF16)) + head_rows(ncn_ref[:, 0:n])
        update(jnp.where(c <= r % n, sn, -jnp.inf), vnt_ref[...].astype(BF16))
        o_all = acc_ref[...] / l_ref[:, 0:1]
        o = jnp.zeros((n, FOX_WIDTH), F32)
        for hd in range(FOX_HEADS):
            o = o + jnp.where(lane // FOX_HEAD_DIM == hd, o_all[n * hd:n * (hd + 1), :], 0.0)
        o_ref[...] = o.astype(BF16)


def _fox_sample(l, q, knt, vnt, cache_kt, cache_vt, nc):
    nb, n, _ = q.shape
    past = cache_kt.shape[3]
    ck = min(FOX_SAMPLE_CHUNK, past)
    q_spec = pl.BlockSpec((None, n, FOX_WIDTH), lambda b, c: (b, 0, 0))
    new_spec = pl.BlockSpec((None, FOX_WIDTH, n), lambda b, c: (b, 0, 0))
    cache_spec = pl.BlockSpec((None, None, FOX_WIDTH, ck), lambda b, c: (l, b, 0, c))
    rows = FOX_HEADS * n
    return pl.pallas_call(
        _fox_sample_kernel,
        grid=(nb, past // ck),
        in_specs=[q_spec, cache_spec, cache_spec, new_spec, new_spec,
                  pl.BlockSpec((None, FOX_HEADS, ck), lambda b, c: (b, 0, c)),
                  pl.BlockSpec((None, FOX_HEADS, LANES), lambda b, c: (b, 0, past // LANES))],
        out_specs=q_spec,
        out_shape=jax.ShapeDtypeStruct((nb, n, FOX_WIDTH), BF16),
        scratch_shapes=[pltpu.VMEM((rows, LANES), F32), pltpu.VMEM((rows, LANES), F32),
                        pltpu.VMEM((rows, FOX_WIDTH), F32)],
        compiler_params=pltpu.CompilerParams(dimension_semantics=("arbitrary", "arbitrary"),
                                             vmem_limit_bytes=VMEM_LIMIT),
        name="fox_sample",
    )(q, cache_kt, cache_vt, knt, vnt, nc, nc)


def _mlp_kernel(*refs, tmajor, nb):
    if tmajor:
        (x_ref, oret_ref, ofox_ref, oconv_ref, wout_ref, nw_ref, wg_ref, wu_ref, wd_ref, fcw_ref, hist_ref,
         y_ref, st_ref) = refs
    else:
        (x_ref, oret_ref, ofox_ref, oconv_ref, wout_ref, nw_ref, wg_ref, wu_ref, wd_ref, fcw_ref,
         y_ref, st_ref, carry_ref) = refs
    tm = x_ref.shape[0]
    rb = tm if tmajor else min(MLP_ROW_BLOCK, tm)
    for r0 in range(0, tm, rb):
        rows = slice(r0, r0 + rb)
        x1 = (x_ref[rows, :] + _dot(oret_ref[rows, :], wout_ref[0:256, :])
              + _dot(ofox_ref[rows, :], wout_ref[256:768, :]) + _dot(oconv_ref[rows, :], wout_ref[768:1024, :]))
        h2 = _rms(x1, nw_ref[...]).astype(BF16)
        down = None
        for c0, c1 in FF_CHUNKS:
            cols = slice(c0, c1)
            gate_pre = _dot(h2, wg_ref[:, cols])
            if tmajor:
                gate_c, new_hist = _dwconv_tmajor(gate_pre, fcw_ref.at[:, cols], hist_ref[:, cols], nb)
                st_ref[:, cols] = new_hist
            else:
                first = (pl.program_id(1) == 0) if r0 == 0 else None
                gate_c = _dwconv_carry(gate_pre, fcw_ref.at[:, cols], carry_ref.at[:, cols], first)
                if r0 + rb == tm:
                    st_ref[:, cols] = carry_ref[6:8, cols]
            up = _dot(h2, wu_ref[:, cols])
            act = (gate_c / (1.0 + jnp.exp(-gate_c)) * up).astype(BF16)
            part = _dot(act, wd_ref[cols, :])
            down = part if down is None else down + part
        y_ref[rows, :] = x1 + down


def _mlp(l, x, oret, ofox, oconv, prm, *, hist=None, nb=0):
    tmajor = hist is not None
    g, r, _ = x.shape
    tm = min(TOKEN_TILE, r)
    grid = (g, r // tm)
    lsel = lambda b, i: (l, 0, 0)
    tok = lambda width: pl.BlockSpec((None, tm, width), lambda b, i: (b, i, 0))
    wspec = lambda rows, cols: pl.BlockSpec((None, rows, cols), lsel, pipeline_mode=pl.Buffered(1))
    in_specs = [tok(D_MODEL), tok(RET_WIDTH), tok(FOX_WIDTH), tok(CONV_DIM),
                wspec(D_MODEL, D_MODEL), pl.BlockSpec((None, 1, D_MODEL), lsel),
                wspec(D_MODEL, D_FF), wspec(D_MODEL, D_FF), wspec(D_FF, D_MODEL),
                pl.BlockSpec((None, 3, D_FF), lsel)]
    args = [x, oret, ofox, oconv, prm["w_out"], prm["norm_ffn"], prm["w_gate"], prm["w_up"], prm["w_down"],
            prm["ffn_conv_w"]]
    if tmajor:
        in_specs.append(pl.BlockSpec((2 * nb, D_FF), lambda b, i: (0, 0)))
        args.append(hist)
        st_shape = jax.ShapeDtypeStruct((2 * nb, D_FF), F32)
        st_spec = pl.BlockSpec((2 * nb, D_FF), lambda b, i: (0, 0))
        scratch = []
    else:
        st_shape = jax.ShapeDtypeStruct((g, 2, D_FF), F32)
        st_spec = pl.BlockSpec((None, 2, D_FF), lambda b, i: (b, 0, 0))
        scratch = [pltpu.VMEM((SUBLANES, D_FF), F32)]
    return pl.pallas_call(
        functools.partial(_mlp_kernel, tmajor=tmajor, nb=nb),
        grid=grid, in_specs=in_specs,
        out_specs=[tok(D_MODEL), st_spec],
        out_shape=[jax.ShapeDtypeStruct((g, r, D_MODEL), F32), st_shape],
        scratch_shapes=scratch,
        compiler_params=pltpu.CompilerParams(dimension_semantics=("arbitrary", "arbitrary"),
                                             vmem_limit_bytes=VMEM_LIMIT),
        name="outproj_mlp",
    )(*args)


def _permute_w_in(w_in):
    w = w_in.astype(BF16)
    depth, rows, _ = w.shape

    def halves_first(blk):
        return blk.reshape(depth, rows, RET_HEADS, 2, RET_HEAD_DIM // 2).transpose(0, 1, 3, 2, 4).reshape(
            depth, rows, RET_WIDTH)

    fl0 = 4 * RET_WIDTH + 3 * FOX_WIDTH
    parts = [halves_first(w[..., 0:RET_WIDTH]), halves_first(w[..., RET_WIDTH:2 * RET_WIDTH]),
             w[..., 2 * RET_WIDTH:fl0], w[..., fl0 + FOX_HEADS:PROJ_DIM], w[..., fl0:fl0 + FOX_HEADS],
             jnp.zeros((depth, rows, PROJ_PAD - PROJ_DIM), BF16)]
    return jnp.concatenate(parts, axis=2)


def _rope_tables(pos):
    half = 32
    inv_freq = ROPE_BASE ** (-jnp.arange(half, dtype=F32) / half)
    ang = pos.astype(F32)[:, None] * inv_freq[None, :]
    return jnp.tile(jnp.cos(ang), (1, RET_HEADS)), jnp.tile(jnp.sin(ang), (1, RET_HEADS))


def _to_bmajor(a, n, nb):
    return a.reshape(n, nb, a.shape[-1]).transpose(1, 0, 2)


def _to_tmajor(a):
    nb, n, c = a.shape
    return a.transpose(1, 0, 2).reshape(1, n * nb, c)


def kernel(x_prompt, x_sample, cache_fox_k, cache_fox_v, cache_fox_logf, state_ret, state_conv, state_ffn_conv,
           norm_mix, w_in, ret_gn_gain, fox_q_gain, fox_k_gain, fox_f_bias, conv_w, w_out, norm_ffn, w_gate,
           w_up, ffn_conv_w, w_down):
    depth = w_in.shape[0]
    b, s, _ = x_prompt.shape
    nb, n, _ = x_sample.shape
    past = cache_fox_k.shape[2]

    w_perm = _permute_w_in(w_in)
    prm = {
        "w_in": w_perm,
        "norm_mix": norm_mix.reshape(depth, 1, D_MODEL),
        "gq": fox_q_gain.reshape(depth, 1, FOX_WIDTH),
        "gk": fox_k_gain.reshape(depth, 1, FOX_WIDTH),
        "fb": jnp.pad(fox_f_bias, ((0, 0), (0, LANES - FOX_HEADS))).reshape(depth, 1, LANES),
        "conv_w": conv_w,
        "w_out": w_out.astype(BF16),
        "norm_ffn": norm_ffn.reshape(depth, 1, D_MODEL),
        "w_gate": w_gate.astype(BF16),
        "w_up": w_up.astype(BF16),
        "w_down": w_down.astype(BF16),
        "ffn_conv_w": ffn_conv_w,
    }
    gain = ret_gn_gain.reshape(depth, 1, RET_WIDTH)
    lane = np.arange(FOX_WIDTH)
    bd = jnp.asarray(lane[:, None] // FOX_HEAD_DIM == lane[None, :] // FOX_HEAD_DIM, BF16)

    cos_p, sin_p = _rope_tables(jnp.arange(s, dtype=jnp.int32))
    pos_s = past + jnp.arange(n, dtype=jnp.int32)
    cos_s, sin_s = _rope_tables(jnp.repeat(pos_s, nb))

    ret_blk = min(RET_BLOCK, s)
    ret_rt = min(4 * TOKEN_TILE, s)
    consts_p = _ret_constants(ret_blk, float(ret_blk))
    samp_rows = LANES
    consts_s = _ret_constants(samp_rows, float(n))
    fox_tile = min(FOX_TILE, s)

    cache_k = jnp.transpose(cache_fox_k, (0, 1, 3, 4, 2)).reshape(depth, nb, FOX_WIDTH, past)
    cache_v = jnp.transpose(cache_fox_v, (0, 1, 3, 4, 2)).reshape(depth, nb, FOX_WIDTH, past)
    cache_lf_t = jnp.transpose(cache_fox_logf, (0, 1, 3, 2))
    c_width = -(-(past + n) // LANES) * LANES

    kbuf = jnp.zeros((depth, b, FOX_WIDTH, s), F32)
    vbuf = jnp.zeros((depth, b, FOX_WIDTH, s), F32)
    zero_state = jnp.zeros((b, RET_WIDTH, RET_WIDTH), F32)
    hist_c_all = jnp.transpose(state_conv, (0, 2, 1, 3)).reshape(depth, 2 * nb, CONV_DIM)
    hist_f_all = jnp.transpose(state_ffn_conv, (0, 2, 1, 3)).reshape(depth, 2 * nb, D_FF)
    state_bd_all = _state_to_bd(state_ret.reshape(depth * nb, RET_HEADS, 64, 64)).reshape(
        depth, nb, RET_WIDTH, RET_WIDTH)

    xp = x_prompt
    xs = jnp.transpose(x_sample, (1, 0, 2)).reshape(1, n * nb, D_MODEL)
    p_lf, p_ret, p_conv, p_ffn = [], [], [], []
    s_k, s_v, s_lf, s_ret, s_conv, s_ffn = [], [], [], [], [], []
    for l in range(depth):
        ret_in, fq, kbuf, vbuf, fkb, fvb, lf, lf_t, oconv, conv_st = _inproj(
            l, xp, prm, cos_p, sin_p, bd, kbuf=kbuf, vbuf=vbuf)
        oret, sbd = _retention(l, ret_in, zero_state, gain, consts_p, ret_blk, ret_rt)
        c_rows = _cumsum_lanes(lf_t)
        nc_rows = c_rows.reshape(b, FOX_HEADS // 2, 2, s // fox_tile, fox_tile).transpose(0, 1, 3, 2, 4)
        ofox = _fox_prompt(fq, fkb, fvb, nc_rows, fox_tile)
        xp, ffn_st = _mlp(l, xp, oret, ofox, oconv, prm)
        p_lf.append(lf)
        p_ret.append(sbd)
        p_conv.append(conv_st)
        p_ffn.append(ffn_st)

        hist_c = hist_c_all[l]
        ret_in, fq, fk32, fv32, fkb, fvb, lf, _, oconv, conv_st = _inproj(
            l, xs, prm, cos_s, sin_s, bd, hist=hist_c, nb=nb)
        ret_b = jnp.pad(_to_bmajor(ret_in, n, nb), ((0, 0), (0, samp_rows - n), (0, 0)))
        oret, sbd = _retention(l, ret_b, state_bd_all[l], gain, consts_s, samp_rows, samp_rows)
        lf_b = _to_bmajor(lf, n, nb)
        lf_all = jnp.concatenate(
            [cache_lf_t[l], jnp.transpose(lf_b, (0, 2, 1)),
             jnp.zeros((nb, FOX_HEADS, c_width - past - n), F32)], axis=2)
        nc = _cumsum_lanes(lf_all)
        k_new = _to_bmajor(fk32, n, nb)
        v_new = _to_bmajor(fv32, n, nb)
        s_k.append(k_new.reshape(nb, n, FOX_HEADS, FOX_HEAD_DIM))
        s_v.append(v_new.reshape(nb, n, FOX_HEADS, FOX_HEAD_DIM))
        ofox = _fox_sample(l, _to_bmajor(fq, n, nb), jnp.transpose(k_new, (0, 2, 1)),
                           jnp.transpose(v_new, (0, 2, 1)), cache_k, cache_v, nc)
        hist_f = hist_f_all[l]
        xs, ffn_st = _mlp(l, xs, _to_tmajor(oret[:, :n]), _to_tmajor(ofox), oconv, prm, hist=hist_f, nb=nb)
        s_lf.append(lf_b)
        s_ret.append(sbd)
        s_conv.append(conv_st)
        s_ffn.append(ffn_st)

    y_sample = xs.reshape(n, nb, D_MODEL).transpose(1, 0, 2)
    stk = lambda ts: jnp.stack(ts, axis=0)
    ret_states = lambda ts, g: _bd_to_state(stk(ts).reshape(depth * g, RET_WIDTH, RET_WIDTH)).reshape(
        depth, g, RET_HEADS, 64, 64)
    conv_states = lambda ts, c: stk(ts).reshape(depth, 2, nb, c).transpose(0, 2, 1, 3)
    from_fm = lambda a: jnp.transpose(a.reshape(depth, b, FOX_HEADS, FOX_HEAD_DIM, s), (0, 1, 4, 2, 3))
    return (xp, y_sample,
            from_fm(kbuf), from_fm(vbuf),
            stk(p_lf), ret_states(p_ret, b), stk(p_conv), stk(p_ffn),
            stk(s_k), stk(s_v), stk(s_lf), ret_states(s_ret, nb), conv_states(s_conv, CONV_DIM),
            conv_states(s_ffn, D_FF))
```
